```python
import jax, jax.numpy as jnp
from jax import lax
import numpy as np

D_MODEL = 2048
BATCH = 4
SEQ = 4096
DEPTH = 2

CHUNK = 64
EPS = 1e-6
MIX_HALF = D_MODEL // 2

CONV_WIDTH = 3
CONV_DIM = MIX_HALF
ATT_HEADS = 8
ATT_HEAD_DIM = MIX_HALF // ATT_HEADS
ATT_PAST_CHUNKS = 8
ATT_BAND = (ATT_PAST_CHUNKS + 1) * CHUNK
REL_CLIP = 256
REL_TABLE = (CHUNK - 1) + REL_CLIP + 1

HGRN_HEADS = 8
HGRN_EXPAND = 128
HGRN_DK_TOTAL = HGRN_HEADS * HGRN_EXPAND
HGRN_DV_TOTAL = MIX_HALF
HGRN_DV = HGRN_DV_TOTAL // HGRN_HEADS
GLA_HEADS = 4
GLA_DV_TOTAL = MIX_HALF
GLA_DK_TOTAL = GLA_DV_TOTAL // 2
GLA_DK = GLA_DK_TOTAL // GLA_HEADS
GLA_DV = GLA_DV_TOTAL // GLA_HEADS
GLA_GATE_RANK = 16
GLA_GATE_NORMALIZER = 16.0

EVEN_SIZES = (CONV_DIM, CONV_DIM, CONV_DIM, MIX_HALF, MIX_HALF, MIX_HALF)
ODD_SIZES = (HGRN_DK_TOTAL, HGRN_DK_TOTAL, HGRN_DV_TOTAL, HGRN_DV_TOTAL,
             GLA_DK_TOTAL, GLA_DK_TOTAL, GLA_DV_TOTAL, GLA_DV_TOTAL, GLA_GATE_RANK)
EVEN_IN = 3 * CONV_DIM + 3 * MIX_HALF
ODD_IN = 2 * HGRN_DK_TOTAL + 2 * HGRN_DV_TOTAL + 2 * GLA_DK_TOTAL + 2 * GLA_DV_TOTAL + GLA_GATE_RANK
D_FF = 4 * D_MODEL
N_EVEN = (DEPTH + 1) // 2
N_ODD = DEPTH // 2

kernel_name = "hybrid_conv_chunkattn_hgrn2_gla_block"


def rmsnorm(x, g):
    x32 = x.astype(jnp.float32)
    y = x32 * lax.rsqrt(jnp.mean(x32 * x32, axis=-1, keepdims=True) + EPS)
    return (y * g.astype(jnp.float32)).astype(x.dtype)


def split_cols(t, sizes):
    idx = [int(v) for v in np.cumsum(sizes)[:-1]]
    return jnp.split(t, idx, axis=-1)


def short_conv_mixer(b_gate, c_gate, h, conv_w):
    u = c_gate * h
    s = u.shape[1]
    up = jnp.pad(u, ((0, 0), (CONV_WIDTH - 1, 0), (0, 0)))
    y = conv_w[0] * up[:, 0:s]
    for j in range(1, CONV_WIDTH):
        y = y + conv_w[j] * up[:, j:j + s]
    return b_gate * y


def chunked_band_attention(q, k, v, rel_bias):
    b, s, h, dh = q.shape
    nc = s // CHUNK
    pad = ((0, 0), (ATT_PAST_CHUNKS * CHUNK, 0), (0, 0), (0, 0))
    kp = jnp.pad(k, pad).reshape(b, nc + ATT_PAST_CHUNKS, CHUNK, h, dh)
    vp = jnp.pad(v, pad).reshape(b, nc + ATT_PAST_CHUNKS, CHUNK, h, dh)
    k_band = jnp.concatenate([kp[:, j:j + nc] for j in range(ATT_PAST_CHUNKS + 1)], axis=2)
    v_band = jnp.concatenate([vp[:, j:j + nc] for j in range(ATT_PAST_CHUNKS + 1)], axis=2)
    qc = q.reshape(b, nc, CHUNK, h, dh)
    scores = jnp.einsum('bnqhd,bnkhd->bhnqk', qc, k_band).astype(jnp.float32) * (dh ** -0.5)
    qi = np.arange(CHUNK)[:, None]
    ki = np.arange(ATT_BAND)[None, :]
    rel = ATT_PAST_CHUNKS * CHUNK + qi - ki
    idx = np.clip(rel, -(CHUNK - 1), REL_CLIP) + (CHUNK - 1)
    bias = rel_bias[:, idx].astype(jnp.float32)
    valid = (np.arange(nc)[:, None] + ki // CHUNK) >= ATT_PAST_CHUNKS
    scores = scores + bias[None, :, None]
    scores = jnp.where(valid[None, None, :, None, :], scores, -1e30)
    p = jax.nn.softmax(scores, axis=-1).astype(v.dtype)
    o = jnp.einsum('bhnqk,bnkhd->bnqhd', p, v_band)
    return o.reshape(b, s, h * dh)


def chunk_gated_recurrence(q, k, v, log_a):
    out_dtype = v.dtype
    b, s, h, dk = q.shape
    dv = v.shape[-1]
    nc = s // CHUNK

    def to_chunks(t):
        return t.astype(jnp.float32).reshape(b, nc, CHUNK, h, t.shape[-1]).transpose(1, 0, 3, 2, 4)

    qc, kc, vc = to_chunks(q), to_chunks(k), to_chunks(v)
    bc = jnp.cumsum(to_chunks(log_a), axis=3)
    causal = np.tril(np.ones((CHUNK, CHUNK), dtype=bool))[:, :, None]

    def step(state, inp):
        q_, k_, v_, b_ = inp
        diff = b_[:, :, :, None, :] - b_[:, :, None, :, :]
        decay = jnp.exp(jnp.where(causal, diff, -jnp.inf))
        scores = jnp.einsum('bhtd,bhsd,bhtsd->bhts', q_, k_, decay)
        o = jnp.einsum('bhts,bhse->bhte', scores, v_) + jnp.einsum('bhtd,bhde->bhte', q_ * jnp.exp(b_), state)
        b_last = b_[:, :, -1:, :]
        state = jnp.exp(b_last[:, :, 0, :])[..., None] * state + jnp.einsum('bhsd,bhse->bhde', k_ * jnp.exp(b_last - b_), v_)
        return state, o

    s0 = jnp.zeros((b, h, dk, dv), jnp.float32)
    _, o = lax.scan(step, s0, (qc, kc, vc, bc))
    return o.transpose(1, 0, 3, 2, 4).reshape(b, s, h, dv).astype(out_dtype)


def hgrn2_mixer(q_raw, f_raw, i_raw, g_raw, lb, norm_g):
    b, s, _ = q_raw.shape
    q = jax.nn.silu(q_raw).reshape(b, s, HGRN_HEADS, HGRN_EXPAND)
    f = lb + (1.0 - lb) * jax.nn.sigmoid(f_raw.astype(jnp.float32))
    log_f = jnp.log(f).reshape(b, s, HGRN_HEADS, HGRN_EXPAND)
    k = (1.0 - f).reshape(b, s, HGRN_HEADS, HGRN_EXPAND)
    i = i_raw.reshape(b, s, HGRN_HEADS, HGRN_DV)
    o = chunk_gated_recurrence(q, k, i, log_f)
    o = rmsnorm(o, norm_g.reshape(HGRN_HEADS, HGRN_DV)).reshape(b, s, HGRN_DV_TOTAL)
    return o * jax.nn.silu(g_raw)


def gla_mixer(q_raw, k_raw, v_raw, r_raw, a_lr, wa2, ba, norm_g):
    b, s, _ = q_raw.shape
    q = q_raw.reshape(b, s, GLA_HEADS, GLA_DK) * (GLA_DK ** -0.5)
    k = k_raw.reshape(b, s, GLA_HEADS, GLA_DK)
    v = v_raw.reshape(b, s, GLA_HEADS, GLA_DV)
    log_a = jax.nn.log_sigmoid((a_lr @ wa2 + ba).astype(jnp.float32)) / GLA_GATE_NORMALIZER
    o = chunk_gated_recurrence(q, k, v, log_a.reshape(b, s, GLA_HEADS, GLA_DK))
    o = rmsnorm(o, norm_g.reshape(GLA_HEADS, GLA_DV)).reshape(b, s, GLA_DV_TOTAL)
    return o * jax.nn.silu(r_raw)


def setup_inputs(seed: int = 0) -> dict:
    key = jax.random.key(seed)
    ks = jax.random.split(key, 16)
    f32 = jnp.float32
    nrm = lambda k, shape, sc: jax.random.normal(k, shape, f32) * sc
    return {
        "x": nrm(ks[0], (BATCH, SEQ, D_MODEL), 1.0),
        "norm_g": 1.0 + nrm(ks[1], (DEPTH, 4, D_MODEL), 0.02),
        "even_w_in": nrm(ks[2], (N_EVEN, D_MODEL, EVEN_IN), D_MODEL ** -0.5),
        "even_conv_w": nrm(ks[3], (N_EVEN, CONV_WIDTH, CONV_DIM), CONV_WIDTH ** -0.5),
        "even_rel_bias": nrm(ks[4], (N_EVEN, ATT_HEADS, REL_TABLE), 0.1),
        "even_w_out": nrm(ks[5], (N_EVEN, D_MODEL, D_MODEL), D_MODEL ** -0.5),
        "odd_w_in": nrm(ks[6], (N_ODD, D_MODEL, ODD_IN), D_MODEL ** -0.5),
        "hgrn_lb": nrm(ks[7], (DEPTH, HGRN_DK_TOTAL), 0.5),
        "hgrn_norm_g": 1.0 + nrm(ks[8], (N_ODD, HGRN_DV_TOTAL), 0.02),
        "gla_wa2": nrm(ks[9], (N_ODD, GLA_GATE_RANK, GLA_DK_TOTAL), GLA_GATE_RANK ** -0.5),
        "gla_ba": nrm(ks[10], (N_ODD, GLA_DK_TOTAL), 0.01),
        "gla_norm_g": 1.0 + nrm(ks[11], (N_ODD, GLA_DV_TOTAL), 0.02),
        "odd_w_out": nrm(ks[12], (N_ODD, D_MODEL, D_MODEL), D_MODEL ** -0.5),
        "mlp_w1": nrm(ks[13], (DEPTH, D_MODEL, D_FF), D_MODEL ** -0.5),
        "mlp_w2": nrm(ks[14], (DEPTH, D_FF, D_MODEL), D_FF ** -0.5),
    }


def reference(x, norm_g, even_w_in, even_conv_w, even_rel_bias, even_w_out,
              odd_w_in, hgrn_lb, hgrn_norm_g, gla_wa2, gla_ba, gla_norm_g, odd_w_out,
              mlp_w1, mlp_w2):
    b, s, _ = x.shape
    lb_soft = jax.nn.softmax(hgrn_lb.astype(jnp.float32), axis=0)
    lb_all = jnp.cumsum(lb_soft, axis=0) - lb_soft[0]
    h = x
    for l in range(DEPTH):
        g = norm_g[l]
        u = rmsnorm(h, g[0])
        if l % 2 == 0:
            e = l // 2
            proj = u @ even_w_in[e]
            b_gate, c_gate, hc, q, k, v = split_cols(proj, EVEN_SIZES)
            ya = short_conv_mixer(b_gate, c_gate, hc, even_conv_w[e])
            shp = (b, s, ATT_HEADS, ATT_HEAD_DIM)
            yb = chunked_band_attention(q.reshape(shp), k.reshape(shp), v.reshape(shp), even_rel_bias[e])
            y = jnp.concatenate([ya, yb], axis=-1) @ even_w_out[e]
        else:
            o_i = l // 2
            proj = u @ odd_w_in[o_i]
            hq, hf, hi, hg, gq, gk, gv, gr, ga = split_cols(proj, ODD_SIZES)
            yc = hgrn2_mixer(hq, hf, hi, hg, lb_all[l], hgrn_norm_g[o_i])
            yd = gla_mixer(gq, gk, gv, gr, ga, gla_wa2[o_i], gla_ba[o_i], gla_norm_g[o_i])
            y = jnp.concatenate([yc, yd], axis=-1) @ odd_w_out[o_i]
        h = h + rmsnorm(y, g[1])
        u = rmsnorm(h, g[2])
        z = jnp.square(jax.nn.relu(u @ mlp_w1[l])) @ mlp_w2[l]
        h = h + rmsnorm(z, g[3])
    return h
```

```python
import functools

import numpy as np
import jax
import jax.numpy as jnp
from jax import lax
from jax.experimental import pallas as pl
from jax.experimental.pallas import tpu as pltpu

F32 = jnp.float32
BF16 = jnp.bfloat16

EPS = 1e-6
CHUNK = 64
SUB = 8
LANES = 128
N_SUB = CHUNK // SUB

CONV_WIDTH = 3
ATT_HEADS = 8
ATT_PAST_CHUNKS = 8
REL_CLIP = 256
HGRN_HEADS = 8
GLA_HEADS = 4
GLA_GATE_RANK = 16
GLA_GATE_NORMALIZER = 16.0

VMEM_LIMIT = 56 * 1024 * 1024


def _params(*sem):
    return pltpu.CompilerParams(dimension_semantics=sem, vmem_limit_bytes=VMEM_LIMIT)


def _rms(x, g):
    ms = jnp.mean(x * x, axis=-1, keepdims=True)
    return x * lax.rsqrt(ms + EPS) * g


def _sigmoid(x):
    return 1.0 / (1.0 + jnp.exp(-x))


def _rms_matmul_kernel(h_ref, g_ref, w_ref, o_ref, u_ref):
    @pl.when(pl.program_id(1) == 0)
    def _():
        u_ref[...] = _rms(h_ref[...], g_ref[...]).astype(BF16)

    o_ref[...] = jnp.dot(u_ref[...], w_ref[...], preferred_element_type=F32).astype(o_ref.dtype)


def _rms_matmul_extra_kernel(h_ref, g_ref, w_ref, wx_ref, o_ref, ox_ref, u_ref):
    @pl.when(pl.program_id(1) == 0)
    def _():
        u = _rms(h_ref[...], g_ref[...]).astype(BF16)
        u_ref[...] = u
        ox_ref[...] = jnp.dot(u, wx_ref[...], preferred_element_type=F32)

    o_ref[...] = jnp.dot(u_ref[...], w_ref[...], preferred_element_type=F32).astype(o_ref.dtype)


def rms_matmul(h, g, w, w_extra=None, *, tm=1024, tn=1024):
    t, d = h.shape
    n = w.shape[1]
    grid = (t // tm, n // tn)
    h_spec = pl.BlockSpec((tm, d), lambda i, j: (i, 0))
    g_spec = pl.BlockSpec((1, d), lambda i, j: (0, 0))
    w_spec = pl.BlockSpec((d, tn), lambda i, j: (0, j))
    o_spec = pl.BlockSpec((tm, tn), lambda i, j: (i, j))
    scratch = [pltpu.VMEM((tm, d), BF16)]
    if w_extra is None:
        return pl.pallas_call(
            _rms_matmul_kernel, grid=grid,
            in_specs=[h_spec, g_spec, w_spec], out_specs=o_spec,
            out_shape=jax.ShapeDtypeStruct((t, n), F32),
            scratch_shapes=scratch, compiler_params=_params("parallel", "arbitrary"),
        )(h, g, w)
    nx = w_extra.shape[1]
    return pl.pallas_call(
        _rms_matmul_extra_kernel, grid=grid,
        in_specs=[h_spec, g_spec, w_spec, pl.BlockSpec((d, nx), lambda i, j: (0, 0))],
        out_specs=[o_spec, pl.BlockSpec((tm, nx), lambda i, j: (i, 0))],
        out_shape=[jax.ShapeDtypeStruct((t, n), F32), jax.ShapeDtypeStruct((t, nx), F32)],
        scratch_shapes=scratch, compiler_params=_params("parallel", "arbitrary"),
    )(h, g, w, w_extra)


def _conv_kernel(b_ref, c_ref, hc_ref, cprev_ref, hprev_ref, w_ref, o_ref, u_ref, *, blocks_per_seq):
    tm = o_ref.shape[0]
    first = (pl.program_id(0) % blocks_per_seq) == 0
    prev = cprev_ref[...] * hprev_ref[...]
    u_ref[0:SUB, :] = jnp.where(first, 0.0, prev)
    u_ref[SUB:, :] = c_ref[...] * hc_ref[...]
    w = w_ref[...]
    y = w[0:1] * u_ref[pl.ds(SUB - 2, tm), :]
    for j in range(1, CONV_WIDTH):
        y = y + w[j:j + 1] * u_ref[pl.ds(SUB - 2 + j, tm), :]
    o_ref[...] = (b_ref[...] * y).astype(o_ref.dtype)


def gated_short_conv(proj, conv_w, seq, *, tm=512):
    t = proj.shape[0]
    c = conv_w.shape[1]
    rows = tm // SUB
    prev = lambda col: pl.BlockSpec((SUB, c), lambda i: (jnp.maximum(i * rows - 1, 0), col))
    cur = lambda col: pl.BlockSpec((tm, c), lambda i: (i, col))
    return pl.pallas_call(
        functools.partial(_conv_kernel, blocks_per_seq=seq // tm),
        grid=(t // tm,),
        in_specs=[cur(0), cur(1), cur(2), prev(1), prev(2),
                  pl.BlockSpec((CONV_WIDTH, c), lambda i: (0, 0))],
        out_specs=pl.BlockSpec((tm, c), lambda i: (i, 0)),
        out_shape=jax.ShapeDtypeStruct((t, c), BF16),
        scratch_shapes=[pltpu.VMEM((tm + SUB, c), F32)],
        compiler_params=_params("parallel"),
    )(proj, proj, proj, proj, proj, conv_w)


def _band_bias(rel_bias, tq, tk):
    qr = np.arange(tq)[:, None]
    kc = np.arange(tk)[None, :]
    off = kc // CHUNK - qr // CHUNK
    pos = off * CHUNK + kc % CHUNK
    rel = ATT_PAST_CHUNKS * CHUNK + qr % CHUNK - pos
    idx = np.clip(rel, -(CHUNK - 1), REL_CLIP) + (CHUNK - 1)
    inside = (off >= 0) & (off <= ATT_PAST_CHUNKS)
    return jnp.where(inside[None], rel_bias[:, idx].astype(F32), -1e30)


def _attn_kernel(q_ref, *refs, n_kblk, scale):
    k_refs, v_refs = refs[:n_kblk], refs[n_kblk:2 * n_kblk]
    bias_ref, o_ref = refs[2 * n_kblk], refs[2 * n_kblk + 1]
    tq = q_ref.shape[0]
    i = pl.program_id(2)
    q = q_ref[...].astype(BF16)
    k = jnp.concatenate([r[...] for r in k_refs], axis=0).astype(BF16)
    v = jnp.concatenate([r[...] for r in v_refs], axis=0).astype(BF16)
    s = lax.dot_general(q, k, (((1,), (1,)), ((), ())), preferred_element_type=F32) * scale
    s = s + bias_ref[0]
    kblk = lax.broadcasted_iota(jnp.int32, s.shape, 1) // tq
    s = jnp.where(kblk + i >= n_kblk - 1, s, -1e30)
    m = jnp.max(s, axis=-1, keepdims=True)
    p = jnp.exp(s - m)
    l = jnp.sum(p, axis=-1, keepdims=True)
    o = jnp.dot(p.astype(BF16), v, preferred_element_type=F32)
    o_ref[...] = (o / l).astype(o_ref.dtype)


def band_attention(proj, rel_bias, batch, seq, col0, *, tq=256):
    t = proj.shape[0]
    heads = rel_bias.shape[0]
    dh = LANES
    past = ATT_PAST_CHUNKS * CHUNK
    n_kblk = past // tq + 1
    nq = seq // tq
    bias = _band_bias(rel_bias, tq, n_kblk * tq)
    cq, ck, cv = col0 // dh, col0 // dh + heads, col0 // dh + 2 * heads

    def kv_spec(cbase, j):
        return pl.BlockSpec(
            (tq, dh), lambda b, h, i: (b * nq + jnp.maximum(i - (n_kblk - 1) + j, 0), cbase + h))

    in_specs = [pl.BlockSpec((tq, dh), lambda b, h, i: (b * nq + i, cq + h))]
    in_specs += [kv_spec(ck, j) for j in range(n_kblk)]
    in_specs += [kv_spec(cv, j) for j in range(n_kblk)]
    in_specs += [pl.BlockSpec((1, tq, n_kblk * tq), lambda b, h, i: (h, 0, 0))]
    return pl.pallas_call(
        functools.partial(_attn_kernel, n_kblk=n_kblk, scale=dh ** -0.5),
        grid=(batch, heads, nq),
        in_specs=in_specs,
        out_specs=pl.BlockSpec((tq, dh), lambda b, h, i: (b * nq + i, h)),
        out_shape=jax.ShapeDtypeStruct((t, heads * dh), BF16),
        compiler_params=_params("parallel", "parallel", "arbitrary"),
    )(*([proj] * (1 + 2 * n_kblk)), bias)


def _outproj_kernel(ya_ref, yb_ref, w_ref, h_ref, g_ref, o_ref):
    y = jnp.concatenate([ya_ref[...], yb_ref[...]], axis=1)
    y = jnp.dot(y, w_ref[...], preferred_element_type=F32)
    o_ref[...] = h_ref[...] + _rms(y, g_ref[...])


def outproj_residual(ya, yb, w, h, g, *, tm=512):
    t, d = h.shape
    row = lambda a: pl.BlockSpec((tm, a.shape[1]), lambda i: (i, 0))
    return pl.pallas_call(
        _outproj_kernel, grid=(t // tm,),
        in_specs=[row(ya), row(yb), pl.BlockSpec(w.shape, lambda i: (0, 0)), row(h),
                  pl.BlockSpec((1, d), lambda i: (0, 0))],
        out_specs=row(h),
        out_shape=jax.ShapeDtypeStruct((t, d), F32),
        compiler_params=_params("parallel"),
    )(ya, yb, w, h, g)


def _mlp_kernel(h_ref, g_in_ref, w1_ref, w2_ref, g_out_ref, o_ref, u_ref, acc_ref):
    j = pl.program_id(1)

    @pl.when(j == 0)
    def _():
        u_ref[...] = _rms(h_ref[...], g_in_ref[...]).astype(BF16)

    a = jnp.dot(u_ref[...], w1_ref[...], preferred_element_type=F32)
    a = jnp.square(jnp.maximum(a, 0.0)).astype(BF16)
    z = jnp.dot(a, w2_ref[...], preferred_element_type=F32)

    @pl.when(j == 0)
    def _():
        acc_ref[...] = z

    @pl.when(j > 0)
    def _():
        acc_ref[...] += z

    @pl.when(j == pl.num_programs(1) - 1)
    def _():
        o_ref[...] = h_ref[...] + _rms(acc_ref[...], g_out_ref[...])


def mlp_residual(h, g_in, w1, w2, g_out, *, tm=512, tf=1024):
    t, d = h.shape
    ff = w1.shape[1]
    vec = pl.BlockSpec((1, d), lambda i, j: (0, 0))
    return pl.pallas_call(
        _mlp_kernel, grid=(t // tm, ff // tf),
        in_specs=[pl.BlockSpec((tm, d), lambda i, j: (i, 0)), vec,
                  pl.BlockSpec((d, tf), lambda i, j: (0, j)),
                  pl.BlockSpec((tf, d), lambda i, j: (j, 0)), vec],
        out_specs=pl.BlockSpec((tm, d), lambda i, j: (i, 0)),
        out_shape=jax.ShapeDtypeStruct((t, d), F32),
        scratch_shapes=[pltpu.VMEM((tm, d), BF16), pltpu.VMEM((tm, d), F32)],
        compiler_params=_params("parallel", "arbitrary"),
    )(h, g_in, w1, w2, g_out)


def _split3_bf16(x):
    hi = x.astype(BF16)
    r = x - hi.astype(F32)
    mid = r.astype(BF16)
    lo = (r - mid.astype(F32)).astype(BF16)
    return hi, mid, lo


def _nt_dot(a, b):
    return lax.dot_general(a, b, (((1,), (1,)), ((), ())), preferred_element_type=F32)


def _gated_recurrence_block(q, k, v, log_a, st_ref):
    r, dk = q.shape
    nch, nsb = r // CHUNK, r // SUB
    row = lax.broadcasted_iota(jnp.int32, (CHUNK, CHUNK), 0)
    col = lax.broadcasted_iota(jnp.int32, (CHUNK, CHUNK), 1)
    tri = (row >= col).astype(BF16)
    parts = _split3_bf16(log_a)
    b = jnp.concatenate(
        [sum(jnp.dot(tri, p[c * CHUNK:(c + 1) * CHUNK], preferred_element_type=F32) for p in parts)
         for c in range(nch)], axis=0)

    b3 = b.reshape(nch, CHUNK, dk)
    q3 = q.reshape(nch, CHUNK, dk)
    b_last = b3[:, CHUNK - 1:CHUNK, :]
    qb = (q * jnp.exp(b)).astype(BF16)
    kb = (k.reshape(nch, CHUNK, dk) * jnp.exp(b_last - b3)).reshape(r, dk).astype(BF16)
    d_last = jnp.exp(b_last)

    b4 = b.reshape(nsb, SUB, dk)
    q4 = q.reshape(nsb, SUB, dk)
    k4 = k.reshape(nsb, SUB, dk)

    kt3 = (k4 * jnp.exp(b4[:, SUB - 1:SUB, :] - b4)).reshape(nch, CHUNK, dk).astype(BF16)
    sub_of_row = lax.broadcasted_iota(jnp.int32, (1, CHUNK, 1), 1) // SUB
    q_slots, k_slots = [], []
    for j in range(N_SUB - 1):
        lo = (j + 1) * SUB
        end_j = b3[:, lo - 1:lo, :]
        qt = (q3[:, lo:, :] * jnp.exp(b3[:, lo:, :] - end_j)).astype(BF16)
        q_slots.append(jnp.concatenate([jnp.zeros((nch, lo, dk), BF16), qt], axis=1))
        k_slots.append(jnp.where(sub_of_row == j, kt3, jnp.zeros_like(kt3)))
    q_cat = jnp.concatenate(q_slots, axis=2)
    k_cat = jnp.concatenate(k_slots, axis=2)

    t_in = lax.broadcasted_iota(jnp.int32, (1, SUB, 1), 1)
    prods = []
    for s in range(SUB):
        arg = jnp.where(t_in >= s, b4 - b4[:, s:s + 1, :], -jnp.inf)
        prods.append((q4 * k4[:, s:s + 1, :] * jnp.exp(arg)).reshape(r, dk).astype(BF16))
    sums = jnp.dot(jnp.concatenate(prods, axis=0), jnp.ones((dk, LANES), BF16),
                   preferred_element_type=F32)
    lane = lax.broadcasted_iota(jnp.int32, (r, LANES), 1)
    sub_start = (lax.broadcasted_iota(jnp.int32, (r, LANES), 0) % CHUNK) // SUB * SUB
    diag = jnp.zeros((r, LANES), F32)
    for s in range(SUB):
        diag = jnp.where(lane == sub_start + s, sums[s * r:(s + 1) * r], diag)

    vb = v.astype(BF16)
    outs = []
    st = st_ref[...]
    for c in range(nch):
        rows = slice(c * CHUNK, (c + 1) * CHUNK)
        scores = _nt_dot(q_cat[c], k_cat[c]) + diag[rows, :CHUNK]
        o = jnp.dot(scores.astype(BF16), vb[rows], preferred_element_type=F32)
        o = o + _nt_dot(qb[rows], st.astype(BF16))
        outs.append(o)
        upd = jnp.dot(v[rows].T.astype(BF16), kb[rows], preferred_element_type=F32)
        st = st * d_last[c] + upd
    st_ref[...] = st
    return jnp.concatenate(outs, axis=0)


def _hgrn_kernel(q_ref, f_ref, i_ref, g_ref, lb_ref, ng_ref, o_ref, st_ref, *, layer):
    @pl.when(pl.program_id(2) == 0)
    def _():
        st_ref[...] = jnp.zeros_like(st_ref)

    lb_raw = lb_ref[...]
    e = jnp.exp(lb_raw - jnp.max(lb_raw, axis=0, keepdims=True))
    soft = e / jnp.sum(e, axis=0, keepdims=True)
    lb = jnp.sum(soft[:layer + 1], axis=0, keepdims=True) - soft[0:1]

    q_raw = q_ref[...]
    q = q_raw * _sigmoid(q_raw)
    f = lb + (1.0 - lb) * _sigmoid(f_ref[...])
    o = _gated_recurrence_block(q, 1.0 - f, i_ref[...], jnp.log(f), st_ref)
    g_raw = g_ref[...]
    o_ref[...] = (_rms(o, ng_ref[...]) * (g_raw * _sigmoid(g_raw))).astype(o_ref.dtype)


def hgrn2_mixer(proj, lb, norm_g, batch, seq, layer, *, rows=512):
    t = proj.shape[0]
    d = LANES
    nblk = seq // rows
    col = lambda base: pl.BlockSpec((rows, d), lambda b, h, i: (b * nblk + i, base + h))
    return pl.pallas_call(
        functools.partial(_hgrn_kernel, layer=layer),
        grid=(batch, HGRN_HEADS, nblk),
        in_specs=[col(0), col(HGRN_HEADS), col(2 * HGRN_HEADS), col(3 * HGRN_HEADS),
                  pl.BlockSpec((lb.shape[0], d), lambda b, h, i: (0, h)),
                  pl.BlockSpec((1, d), lambda b, h, i: (0, h))],
        out_specs=pl.BlockSpec((rows, d), lambda b, h, i: (b * nblk + i, h)),
        out_shape=jax.ShapeDtypeStruct((t, HGRN_HEADS * d), BF16),
        scratch_shapes=[pltpu.VMEM((d, d), F32)],
        compiler_params=_params("parallel", "parallel", "arbitrary"),
    )(proj, proj, proj, proj, lb, norm_g)


def _gla_kernel(q_ref, k_ref, v_ref, r_ref, a_ref, wa_ref, ba_ref, ng_ref, o_ref, st_ref, *, q_scale):
    @pl.when(pl.program_id(2) == 0)
    def _():
        st_ref[...] = jnp.zeros_like(st_ref)

    x = jnp.dot(a_ref[...].astype(BF16), wa_ref[...], preferred_element_type=F32) + ba_ref[...]
    log_sig = jnp.minimum(x, 0.0) - jnp.log(1.0 + jnp.exp(-jnp.abs(x)))
    log_a = log_sig / GLA_GATE_NORMALIZER
    o = _gated_recurrence_block(q_ref[...] * q_scale, k_ref[...], v_ref[...], log_a, st_ref)
    r_raw = r_ref[...]
    o_ref[...] = (_rms(o, ng_ref[...]) * (r_raw * _sigmoid(r_raw))).astype(o_ref.dtype)


def gla_mixer(proj, a_lr, wa2, ba, norm_g, batch, seq, col0, *, rows=512):
    t = proj.shape[0]
    dk = LANES
    dv = 2 * dk
    heads = GLA_HEADS
    nblk = seq // rows
    cq, ck = col0 // dk, col0 // dk + heads
    cv = (col0 + 2 * heads * dk) // dv
    cr = cv + heads
    spec = lambda w, base: pl.BlockSpec((rows, w), lambda b, h, i: (b * nblk + i, base + h))
    return pl.pallas_call(
        functools.partial(_gla_kernel, q_scale=dk ** -0.5),
        grid=(batch, heads, nblk),
        in_specs=[spec(dk, cq), spec(dk, ck), spec(dv, cv), spec(dv, cr),
                  pl.BlockSpec((rows, a_lr.shape[1]), lambda b, h, i: (b * nblk + i, 0)),
                  pl.BlockSpec((wa2.shape[0], dk), lambda b, h, i: (0, h)),
                  pl.BlockSpec((1, dk), lambda b, h, i: (0, h)),
                  pl.BlockSpec((1, dv), lambda b, h, i: (0, h))],
        out_specs=pl.BlockSpec((rows, dv), lambda b, h, i: (b * nblk + i, h)),
        out_shape=jax.ShapeDtypeStruct((t, heads * dv), BF16),
        scratch_shapes=[pltpu.VMEM((dv, dk), F32)],
        compiler_params=_params("parallel", "parallel", "arbitrary"),
    )(proj, proj, proj, proj, a_lr, wa2, ba, norm_g)


def kernel(x, norm_g, even_w_in, even_conv_w, even_rel_bias, even_w_out, odd_w_in, hgrn_lb, hgrn_norm_g,
           gla_wa2, gla_ba, gla_norm_g, odd_w_out, mlp_w1, mlp_w2):
    batch, seq, d = x.shape
    depth = norm_g.shape[0]
    half = d // 2
    h = x.reshape(batch * seq, d)
    for l in range(depth):
        g = norm_g[l][:, None, :]
        if l % 2 == 0:
            e = l // 2
            proj = rms_matmul(h, g[0], even_w_in[e].astype(BF16))
            ya = gated_short_conv(proj, even_conv_w[e], seq)
            yb = band_attention(proj, even_rel_bias[e], batch, seq, CONV_WIDTH * half)
            w_out = even_w_out[e]
        else:
            o = l // 2
            w_in = odd_w_in[o]
            n_main = w_in.shape[1] - GLA_GATE_RANK
            w_gate = jnp.pad(w_in[:, n_main:], ((0, 0), (0, LANES - GLA_GATE_RANK))).astype(BF16)
            proj, a_lr = rms_matmul(h, g[0], w_in[:, :n_main].astype(BF16), w_gate)
            ya = hgrn2_mixer(proj, hgrn_lb, hgrn_norm_g[o][None, :], batch, seq, l)
            wa2 = jnp.pad(gla_wa2[o], ((0, LANES - GLA_GATE_RANK), (0, 0))).astype(BF16)
            yb = gla_mixer(proj, a_lr, wa2, gla_ba[o][None, :], gla_norm_g[o][None, :], batch, seq,
                           4 * HGRN_HEADS * LANES)
            w_out = odd_w_out[o]
        h = outproj_residual(ya, yb, w_out.astype(BF16), h, g[1])
        h = mlp_residual(h, g[2], mlp_w1[l].astype(BF16), mlp_w2[l].astype(BF16), g[3])
    return h.reshape(batch, seq, d)
```

```python
import functools

import numpy as np
import jax
import jax.numpy as jnp
from jax import lax
from jax.experimental import pallas as pl
from jax.experimental.pallas import tpu as pltpu

F32 = jnp.float32
BF16 = jnp.bfloat16

EPS = 1e-6
LOG2_E = 1.4426950408889634
CHUNK = 64
SUB = 8
LANES = 128
N_SUB = CHUNK // SUB

CONV_WIDTH = 3
ATT_HEADS = 8
ATT_PAST_CHUNKS = 8
REL_CLIP = 256
HGRN_HEADS = 8
GLA_HEADS = 4
GLA_GATE_RANK = 16
GLA_GATE_NORMALIZER = 16.0

VMEM_LIMIT = 56 * 1024 * 1024


def _params(*sem):
    return pltpu.CompilerParams(dimension_semantics=sem, vmem_limit_bytes=VMEM_LIMIT)


def _rms(x, g):
    ms = jnp.mean(x * x, axis=-1, keepdims=True)
    return x * lax.rsqrt(ms + EPS) * g


def _sigmoid(x):
    return 1.0 / (1.0 + jnp.exp(-x))


def _nt_dot(a, b):
    return lax.dot_general(a, b, (((1,), (1,)), ((), ())), preferred_element_type=F32)


def _rms_matmul_kernel(h_ref, g_ref, w_ref, o_ref, u_ref):
    @pl.when(pl.program_id(1) == 0)
    def _():
        u_ref[...] = _rms(h_ref[...], g_ref[...]).astype(BF16)

    o_ref[...] = jnp.dot(u_ref[...], w_ref[...], preferred_element_type=F32).astype(o_ref.dtype)


def _rms_matmul_extra_kernel(h_ref, g_ref, w_ref, wx_ref, o_ref, ox_ref, u_ref):
    @pl.when(pl.program_id(1) == 0)
    def _():
        u = _rms(h_ref[...], g_ref[...]).astype(BF16)
        u_ref[...] = u
        ox_ref[...] = jnp.dot(u, wx_ref[...], preferred_element_type=F32)

    o_ref[...] = jnp.dot(u_ref[...], w_ref[...], preferred_element_type=F32).astype(o_ref.dtype)


def rms_matmul(h, g, w, layer, n, w_extra=None, *, tm=1024, tn=1024):
    t, d = h.shape
    grid = (t // tm, n // tn)
    h_spec = pl.BlockSpec((tm, d), lambda i, j: (i, 0))
    g_spec = pl.BlockSpec((1, d), lambda i, j: (0, 0))
    w_spec = pl.BlockSpec((None, d, tn), lambda i, j: (layer, 0, j))
    o_spec = pl.BlockSpec((tm, tn), lambda i, j: (i, j))
    scratch = [pltpu.VMEM((tm, d), BF16)]
    if w_extra is None:
        return pl.pallas_call(
            _rms_matmul_kernel, grid=grid,
            in_specs=[h_spec, g_spec, w_spec], out_specs=o_spec,
            out_shape=jax.ShapeDtypeStruct((t, n), F32),
            scratch_shapes=scratch, compiler_params=_params("parallel", "arbitrary"),
        )(h, g, w)
    nx = w_extra.shape[1]
    return pl.pallas_call(
        _rms_matmul_extra_kernel, grid=grid,
        in_specs=[h_spec, g_spec, w_spec, pl.BlockSpec((d, nx), lambda i, j: (0, 0))],
        out_specs=[o_spec, pl.BlockSpec((tm, nx), lambda i, j: (i, 0))],
        out_shape=[jax.ShapeDtypeStruct((t, n), F32), jax.ShapeDtypeStruct((t, nx), F32)],
        scratch_shapes=scratch, compiler_params=_params("parallel", "arbitrary"),
    )(h, g, w, w_extra)


def _conv_kernel(b_ref, c_ref, hc_ref, cprev_ref, hprev_ref, w_ref, o_ref, u_ref, *, blocks_per_seq):
    tm = o_ref.shape[0]
    first = (pl.program_id(0) % blocks_per_seq) == 0
    prev = cprev_ref[...] * hprev_ref[...]
    u_ref[0:SUB, :] = jnp.where(first, 0.0, prev)
    u_ref[SUB:, :] = c_ref[...] * hc_ref[...]
    w = w_ref[...]
    y = w[0:1] * u_ref[pl.ds(SUB - 2, tm), :]
    for j in range(1, CONV_WIDTH):
        y = y + w[j:j + 1] * u_ref[pl.ds(SUB - 2 + j, tm), :]
    o_ref[...] = (b_ref[...] * y).astype(o_ref.dtype)


def gated_short_conv(proj, conv_w, seq, *, tm=512):
    t = proj.shape[0]
    c = conv_w.shape[1]
    rows = tm // SUB
    prev = lambda col: pl.BlockSpec((SUB, c), lambda i: (jnp.maximum(i * rows - 1, 0), col))
    cur = lambda col: pl.BlockSpec((tm, c), lambda i: (i, col))
    return pl.pallas_call(
        functools.partial(_conv_kernel, blocks_per_seq=seq // tm),
        grid=(t // tm,),
        in_specs=[cur(0), cur(1), cur(2), prev(1), prev(2),
                  pl.BlockSpec((CONV_WIDTH, c), lambda i: (0, 0))],
        out_specs=pl.BlockSpec((tm, c), lambda i: (i, 0)),
        out_shape=jax.ShapeDtypeStruct((t, c), BF16),
        scratch_shapes=[pltpu.VMEM((tm + SUB, c), F32)],
        compiler_params=_params("parallel"),
    )(proj, proj, proj, proj, proj, conv_w)


def _band_bias(rel_bias, tq, n_kblk):
    band = (ATT_PAST_CHUNKS + 1) * CHUNK
    heads = rel_bias.shape[0]
    rel_bias = rel_bias.astype(F32)
    n_far = band - REL_CLIP
    ext = jnp.concatenate([jnp.broadcast_to(rel_bias[:, -1:], (heads, n_far)), rel_bias[:, ::-1][:, 1:]], axis=1)
    rows = jnp.stack([ext[:, CHUNK - 1 - qi:CHUNK - 1 - qi + band] for qi in range(CHUNK)], axis=1)
    tk = n_kblk * tq
    neg = lambda w: jnp.full((heads, CHUNK, w), -1e30, F32)
    blocks = [jnp.concatenate([neg(c * CHUNK), rows, neg(tk - band - c * CHUNK)], axis=2)
              for c in range(tq // CHUNK)]
    base = jnp.concatenate(blocks, axis=1)
    kblk = np.arange(tk)[None, :] // tq
    variants = [jnp.where(kblk + v >= n_kblk - 1, base, -1e30) for v in range(n_kblk)]
    return jnp.stack(variants, axis=0)


def _attn_kernel(q_ref, *refs, n_kblk, heads, scale):
    k_refs, v_refs = refs[:n_kblk], refs[n_kblk:2 * n_kblk]
    bias_ref, o_ref = refs[2 * n_kblk], refs[2 * n_kblk + 1]
    dh = q_ref.shape[1] // heads
    for h in range(heads):
        cols = slice(h * dh, (h + 1) * dh)
        q = (q_ref[:, cols] * scale).astype(BF16)
        k = jnp.concatenate([r[:, cols] for r in k_refs], axis=0).astype(BF16)
        v = jnp.concatenate([r[:, cols] for r in v_refs], axis=0).astype(BF16)
        s = _nt_dot(q, k) + bias_ref[h]
        m = jnp.max(s, axis=-1, keepdims=True)
        p = jnp.exp(s - m)
        l = jnp.sum(p, axis=-1, keepdims=True)
        o = jnp.dot(p.astype(BF16), v, preferred_element_type=F32)
        o_ref[:, cols] = (o / l).astype(o_ref.dtype)


def band_attention(proj, rel_bias, batch, seq, col0, *, tq=256):
    t = proj.shape[0]
    heads = rel_bias.shape[0]
    dh = LANES
    width = heads * dh
    n_kblk = ATT_PAST_CHUNKS * CHUNK // tq + 1
    nq = seq // tq
    bias = _band_bias(rel_bias, tq, n_kblk)
    cq = col0 // width

    def kv_spec(cblk, j):
        return pl.BlockSpec(
            (tq, width), lambda b, i: (b * nq + jnp.maximum(i - (n_kblk - 1) + j, 0), cblk))

    in_specs = [pl.BlockSpec((tq, width), lambda b, i: (b * nq + i, cq))]
    in_specs += [kv_spec(cq + 1, j) for j in range(n_kblk)]
    in_specs += [kv_spec(cq + 2, j) for j in range(n_kblk)]
    in_specs += [pl.BlockSpec((None, heads, tq, n_kblk * tq),
                              lambda b, i: (jnp.minimum(i, n_kblk - 1), 0, 0, 0))]
    return pl.pallas_call(
        functools.partial(_attn_kernel, n_kblk=n_kblk, heads=heads, scale=dh ** -0.5),
        grid=(batch, nq),
        in_specs=in_specs,
        out_specs=pl.BlockSpec((tq, width), lambda b, i: (b * nq + i, 0)),
        out_shape=jax.ShapeDtypeStruct((t, width), BF16),
        compiler_params=_params("parallel", "arbitrary"),
    )(*([proj] * (1 + 2 * n_kblk)), bias)


def _outproj_kernel(ya_ref, yb_ref, w_ref, h_ref, g_ref, o_ref):
    y = jnp.concatenate([ya_ref[...], yb_ref[...]], axis=1)
    y = jnp.dot(y, w_ref[...], preferred_element_type=F32)
    o_ref[...] = h_ref[...] + _rms(y, g_ref[...])


def outproj_residual(ya, yb, w, layer, h, g, *, tm=512):
    t, d = h.shape
    row = lambda a: pl.BlockSpec((tm, a.shape[1]), lambda i: (i, 0))
    return pl.pallas_call(
        _outproj_kernel, grid=(t // tm,),
        in_specs=[row(ya), row(yb), pl.BlockSpec((None,) + w.shape[1:], lambda i: (layer, 0, 0)), row(h),
                  pl.BlockSpec((1, d), lambda i: (0, 0))],
        out_specs=row(h),
        out_shape=jax.ShapeDtypeStruct((t, d), F32),
        compiler_params=_params("parallel"),
    )(ya, yb, w, h, g)


def _mlp_kernel(h_ref, g_in_ref, w1_ref, w2_ref, g_out_ref, o_ref, u_ref, acc_ref):
    j = pl.program_id(1)

    @pl.when(j == 0)
    def _():
        u_ref[...] = _rms(h_ref[...], g_in_ref[...]).astype(BF16)

    a = jnp.dot(u_ref[...], w1_ref[...], preferred_element_type=F32)
    a = jnp.square(jnp.maximum(a, 0.0)).astype(BF16)
    z = jnp.dot(a, w2_ref[...], preferred_element_type=F32)

    @pl.when(j == 0)
    def _():
        acc_ref[...] = z

    @pl.when(j > 0)
    def _():
        acc_ref[...] += z

    @pl.when(j == pl.num_programs(1) - 1)
    def _():
        o_ref[...] = h_ref[...] + _rms(acc_ref[...], g_out_ref[...])


def mlp_residual(h, g_in, w1, w2, layer, g_out, *, tm=512, tf=1024):
    t, d = h.shape
    ff = w1.shape[2]
    vec = pl.BlockSpec((1, d), lambda i, j: (0, 0))
    return pl.pallas_call(
        _mlp_kernel, grid=(t // tm, ff // tf),
        in_specs=[pl.BlockSpec((tm, d), lambda i, j: (i, 0)), vec,
                  pl.BlockSpec((None, d, tf), lambda i, j: (layer, 0, j)),
                  pl.BlockSpec((None, tf, d), lambda i, j: (layer, j, 0)), vec],
        out_specs=pl.BlockSpec((tm, d), lambda i, j: (i, 0)),
        out_shape=jax.ShapeDtypeStruct((t, d), F32),
        scratch_shapes=[pltpu.VMEM((tm, d), BF16), pltpu.VMEM((tm, d), F32)],
        compiler_params=_params("parallel", "arbitrary"),
    )(h, g_in, w1, w2, g_out)


def _split3_bf16(x):
    hi = x.astype(BF16)
    r = x - hi.astype(F32)
    mid = r.astype(BF16)
    lo = (r - mid.astype(F32)).astype(BF16)
    return hi, mid, lo


def _gated_recurrence_block(q, k, v, log_a, st_ref):
    r, dk = q.shape
    nch, nsb = r // CHUNK, r // SUB
    row = lax.broadcasted_iota(jnp.int32, (CHUNK, CHUNK), 0)
    col = lax.broadcasted_iota(jnp.int32, (CHUNK, CHUNK), 1)
    tri = (row >= col).astype(BF16)
    parts = _split3_bf16(log_a * LOG2_E)
    b = jnp.concatenate(
        [sum(jnp.dot(tri, p[c * CHUNK:(c + 1) * CHUNK], preferred_element_type=F32) for p in parts)
         for c in range(nch)], axis=0)

    b3 = b.reshape(nch, CHUNK, dk)
    q3 = q.reshape(nch, CHUNK, dk)
    b_last = b3[:, CHUNK - 1:CHUNK, :]
    qb = (q * jnp.exp2(b)).astype(BF16)
    kb = (k.reshape(nch, CHUNK, dk) * jnp.exp2(b_last - b3)).reshape(r, dk).astype(BF16)
    d_last = jnp.exp2(b_last)

    b4 = b.reshape(nsb, SUB, dk)
    q4 = q.reshape(nsb, SUB, dk)
    k4 = k.reshape(nsb, SUB, dk)

    kt3 = (k4 * jnp.exp2(b4[:, SUB - 1:SUB, :] - b4)).reshape(nch, CHUNK, dk).astype(BF16)
    sub_of_row = lax.broadcasted_iota(jnp.int32, (1, CHUNK, 1), 1) // SUB
    q_slots, k_slots = [], []
    for j in range(N_SUB - 1):
        lo = (j + 1) * SUB
        end_j = b3[:, lo - 1:lo, :]
        qt = (q3[:, lo:, :] * jnp.exp2(b3[:, lo:, :] - end_j)).astype(BF16)
        q_slots.append(jnp.concatenate([jnp.zeros((nch, lo, dk), BF16), qt], axis=1))
        k_slots.append(jnp.where(sub_of_row == j, kt3, jnp.zeros_like(kt3)))
    q_cat = jnp.concatenate(q_slots, axis=2)
    k_cat = jnp.concatenate(k_slots, axis=2)

    t_in = lax.broadcasted_iota(jnp.int32, (1, SUB, 1), 1)
    prods = []
    for s in range(SUB):
        arg = jnp.where(t_in >= s, b4 - b4[:, s:s + 1, :], -jnp.inf)
        prods.append((q4 * k4[:, s:s + 1, :] * jnp.exp2(arg)).reshape(r, dk).astype(BF16))
    slot = lax.broadcasted_iota(jnp.int32, (SUB * dk, LANES), 0) // dk
    to_lane = (slot == lax.broadcasted_iota(jnp.int32, (SUB * dk, LANES), 1)).astype(BF16)
    diag = jnp.dot(jnp.concatenate(prods, axis=1), to_lane, preferred_element_type=F32)
    diag4 = diag.reshape(nch, N_SUB, SUB, LANES)
    diag = jnp.stack(
        [diag4[:, 0]] + [pltpu.roll(diag4[:, j].reshape(nch * SUB, LANES), j * SUB, axis=1)
                         .reshape(nch, SUB, LANES) for j in range(1, N_SUB)], axis=1).reshape(r, LANES)

    vb = v.astype(BF16)
    outs = []
    st = st_ref[...]
    for c in range(nch):
        rows = slice(c * CHUNK, (c + 1) * CHUNK)
        scores = _nt_dot(q_cat[c], k_cat[c]) + diag[rows, :CHUNK]
        o = jnp.dot(scores.astype(BF16), vb[rows], preferred_element_type=F32)
        o = o + _nt_dot(qb[rows], st.astype(BF16))
        outs.append(o)
        upd = jnp.dot(v[rows].T.astype(BF16), kb[rows], preferred_element_type=F32)
        st = st * d_last[c] + upd
    st_ref[...] = st
    return jnp.concatenate(outs, axis=0)


def _hgrn_kernel(q_ref, f_ref, i_ref, g_ref, lb_ref, ng_ref, o_ref, st_ref, *, layer):
    @pl.when(pl.program_id(2) == 0)
    def _():
        st_ref[...] = jnp.zeros_like(st_ref)

    lb_raw = lb_ref[...]
    e = jnp.exp(lb_raw - jnp.max(lb_raw, axis=0, keepdims=True))
    soft = e / jnp.sum(e, axis=0, keepdims=True)
    lb = jnp.sum(soft[:layer + 1], axis=0, keepdims=True) - soft[0:1]

    q_raw = q_ref[...]
    q = q_raw * _sigmoid(q_raw)
    f = lb + (1.0 - lb) * _sigmoid(f_ref[...])
    o = _gated_recurrence_block(q, 1.0 - f, i_ref[...], jnp.log(f), st_ref)
    g_raw = g_ref[...]
    o_ref[...] = (_rms(o, ng_ref[...]) * (g_raw * _sigmoid(g_raw))).astype(o_ref.dtype)


def hgrn2_mixer(proj, lb, norm_g, batch, seq, layer, *, rows=512):
    t = proj.shape[0]
    d = LANES
    nblk = seq // rows
    col = lambda base: pl.BlockSpec((rows, d), lambda b, h, i: (b * nblk + i, base + h))
    return pl.pallas_call(
        functools.partial(_hgrn_kernel, layer=layer),
        grid=(batch, HGRN_HEADS, nblk),
        in_specs=[col(0), col(HGRN_HEADS), col(2 * HGRN_HEADS), col(3 * HGRN_HEADS),
                  pl.BlockSpec((lb.shape[0], d), lambda b, h, i: (0, h)),
                  pl.BlockSpec((1, d), lambda b, h, i: (0, h))],
        out_specs=pl.BlockSpec((rows, d), lambda b, h, i: (b * nblk + i, h)),
        out_shape=jax.ShapeDtypeStruct((t, HGRN_HEADS * d), BF16),
        scratch_shapes=[pltpu.VMEM((d, d), F32)],
        compiler_params=_params("parallel", "parallel", "arbitrary"),
    )(proj, proj, proj, proj, lb, norm_g)


def _gla_kernel(q_ref, k_ref, v_ref, r_ref, a_ref, wa_ref, ba_ref, ng_ref, o_ref, st_ref, *, q_scale):
    @pl.when(pl.program_id(2) == 0)
    def _():
        st_ref[...] = jnp.zeros_like(st_ref)

    x = jnp.dot(a_ref[...].astype(BF16), wa_ref[...], preferred_element_type=F32) + ba_ref[...]
    log_sig = jnp.minimum(x, 0.0) - jnp.log(1.0 + jnp.exp(-jnp.abs(x)))
    log_a = log_sig / GLA_GATE_NORMALIZER
    o = _gated_recurrence_block(q_ref[...] * q_scale, k_ref[...], v_ref[...], log_a, st_ref)
    r_raw = r_ref[...]
    o_ref[...] = (_rms(o, ng_ref[...]) * (r_raw * _sigmoid(r_raw))).astype(o_ref.dtype)


def gla_mixer(proj, a_lr, wa2, ba, norm_g, batch, seq, col0, *, rows=512):
    t = proj.shape[0]
    dk = LANES
    dv = 2 * dk
    heads = GLA_HEADS
    nblk = seq // rows
    cq, ck = col0 // dk, col0 // dk + heads
    cv = (col0 + 2 * heads * dk) // dv
    cr = cv + heads
    spec = lambda w, base: pl.BlockSpec((rows, w), lambda b, h, i: (b * nblk + i, base + h))
    return pl.pallas_call(
        functools.partial(_gla_kernel, q_scale=dk ** -0.5),
        grid=(batch, heads, nblk),
        in_specs=[spec(dk, cq), spec(dk, ck), spec(dv, cv), spec(dv, cr),
                  pl.BlockSpec((rows, a_lr.shape[1]), lambda b, h, i: (b * nblk + i, 0)),
                  pl.BlockSpec((wa2.shape[0], dk), lambda b, h, i: (0, h)),
                  pl.BlockSpec((1, dk), lambda b, h, i: (0, h)),
                  pl.BlockSpec((1, dv), lambda b, h, i: (0, h))],
        out_specs=pl.BlockSpec((rows, dv), lambda b, h, i: (b * nblk + i, h)),
        out_shape=jax.ShapeDtypeStruct((t, heads * dv), BF16),
        scratch_shapes=[pltpu.VMEM((dv, dk), F32)],
        compiler_params=_params("parallel", "parallel", "arbitrary"),
    )(proj, proj, proj, proj, a_lr, wa2, ba, norm_g)


def kernel(x, norm_g, even_w_in, even_conv_w, even_rel_bias, even_w_out, odd_w_in, hgrn_lb, hgrn_norm_g,
           gla_wa2, gla_ba, gla_norm_g, odd_w_out, mlp_w1, mlp_w2):
    batch, seq, d = x.shape
    depth = norm_g.shape[0]
    half = d // 2
    even_w_in_b, even_w_out_b = even_w_in.astype(BF16), even_w_out.astype(BF16)
    odd_w_in_b, odd_w_out_b = odd_w_in.astype(BF16), odd_w_out.astype(BF16)
    mlp_w1_b, mlp_w2_b = mlp_w1.astype(BF16), mlp_w2.astype(BF16)
    n_odd_main = odd_w_in.shape[2] - GLA_GATE_RANK
    lane_pad = LANES - GLA_GATE_RANK
    h = x.reshape(batch * seq, d)
    for l in range(depth):
        g = norm_g[l][:, None, :]
        e = l // 2
        if l % 2 == 0:
            proj = rms_matmul(h, g[0], even_w_in_b, e, even_w_in.shape[2])
            ya = gated_short_conv(proj, even_conv_w[e], seq)
            yb = band_attention(proj, even_rel_bias[e], batch, seq, CONV_WIDTH * half)
            w_out = even_w_out_b
        else:
            w_gate = jnp.pad(odd_w_in[e][:, n_odd_main:], ((0, 0), (0, lane_pad))).astype(BF16)
            proj, a_lr = rms_matmul(h, g[0], odd_w_in_b, e, n_odd_main, w_gate)
            ya = hgrn2_mixer(proj, hgrn_lb, hgrn_norm_g[e][None, :], batch, seq, l)
            wa2 = jnp.pad(gla_wa2[e], ((0, lane_pad), (0, 0))).astype(BF16)
            yb = gla_mixer(proj, a_lr, wa2, gla_ba[e][None, :], gla_norm_g[e][None, :], batch, seq,
                           4 * HGRN_HEADS * LANES)
            w_out = odd_w_out_b
        h = outproj_residual(ya, yb, w_out, e, h, g[1])
        h = mlp_residual(h, g[2], mlp_w1_b, mlp_w2_b, l, g[3])
    return h.reshape(batch, seq, d)
```

```python
import functools

import numpy as np
import jax
import jax.numpy as jnp
from jax import lax
from jax.experimental import pallas as pl
from jax.experimental.pallas import tpu as pltpu

F32 = jnp.float32
BF16 = jnp.bfloat16

EPS = 1e-6
LOG2_E = 1.4426950408889634
CHUNK = 64
SUB = 8
LANES = 128
N_SUB = CHUNK // SUB

CONV_WIDTH = 3
ATT_HEADS = 8
ATT_PAST_CHUNKS = 8
REL_CLIP = 256
HGRN_HEADS = 8
GLA_HEADS = 4
GLA_GATE_RANK = 16
GLA_GATE_NORMALIZER = 16.0

VMEM_LIMIT = 56 * 1024 * 1024


def _params(*sem):
    return pltpu.CompilerParams(dimension_semantics=sem, vmem_limit_bytes=VMEM_LIMIT)


def _rms(x, g):
    ms = jnp.mean(x * x, axis=-1, keepdims=True)
    return x * lax.rsqrt(ms + EPS) * g


def _sigmoid(x):
    return 1.0 / (1.0 + jnp.exp(-x))


def _nt_dot(a, b):
    return lax.dot_general(a, b, (((1,), (1,)), ((), ())), preferred_element_type=F32)


def _rms_matmul_kernel(h_ref, g_ref, w_ref, *refs, ncol):
    if len(refs) == 4:
        wx_ref, o_ref, ox_ref, u_ref = refs
    else:
        (o_ref, u_ref), wx_ref, ox_ref = refs, None, None
    j = pl.program_id(1)
    tn = o_ref.shape[1]

    @pl.when(j == 0)
    def _():
        u = _rms(h_ref[...], g_ref[...]).astype(BF16)
        u_ref[...] = u
        if wx_ref is not None:
            ox_ref[...] = jnp.dot(u, wx_ref[...], preferred_element_type=F32)

    for jj in range(ncol):
        @pl.when(j == jj)
        def _():
            o_ref[...] = jnp.dot(u_ref[...], w_ref[:, jj * tn:(jj + 1) * tn], preferred_element_type=F32)


def rms_matmul(h, g, w, layer, n, w_extra=None, *, tm=512, tn=2048):
    t, d = h.shape
    ncol = n // tn
    resident = pl.Buffered(1)
    in_specs = [pl.BlockSpec((tm, d), lambda i, j: (i, 0)),
                pl.BlockSpec((1, d), lambda i, j: (0, 0)),
                pl.BlockSpec((None, d, w.shape[2]), lambda i, j: (layer, 0, 0), pipeline_mode=resident)]
    out_specs = [pl.BlockSpec((tm, tn), lambda i, j: (i, j))]
    out_shape = [jax.ShapeDtypeStruct((t, n), F32)]
    args = [h, g, w]
    if w_extra is not None:
        nx = w_extra.shape[1]
        in_specs.append(pl.BlockSpec((d, nx), lambda i, j: (0, 0), pipeline_mode=resident))
        out_specs.append(pl.BlockSpec((tm, nx), lambda i, j: (i, 0)))
        out_shape.append(jax.ShapeDtypeStruct((t, nx), F32))
        args.append(w_extra)
    out = pl.pallas_call(
        functools.partial(_rms_matmul_kernel, ncol=ncol), grid=(t // tm, ncol),
        in_specs=in_specs, out_specs=out_specs, out_shape=out_shape,
        scratch_shapes=[pltpu.VMEM((tm, d), BF16)],
        compiler_params=_params("parallel", "arbitrary"),
    )(*args)
    return out if w_extra is not None else out[0]


def _conv_kernel(b_ref, c_ref, hc_ref, cprev_ref, hprev_ref, w_ref, o_ref, u_ref, *, blocks_per_seq):
    tm = o_ref.shape[0]
    first = (pl.program_id(0) % blocks_per_seq) == 0
    prev = cprev_ref[...] * hprev_ref[...]
    u_ref[0:SUB, :] = jnp.where(first, 0.0, prev)
    u_ref[SUB:, :] = c_ref[...] * hc_ref[...]
    w = w_ref[...]
    y = w[0:1] * u_ref[pl.ds(SUB - 2, tm), :]
    for j in range(1, CONV_WIDTH):
        y = y + w[j:j + 1] * u_ref[pl.ds(SUB - 2 + j, tm), :]
    o_ref[...] = (b_ref[...] * y).astype(o_ref.dtype)


def gated_short_conv(proj, conv_w, seq, *, tm=512):
    t = proj.shape[0]
    c = conv_w.shape[1]
    rows = tm // SUB
    prev = lambda col: pl.BlockSpec((SUB, c), lambda i: (jnp.maximum(i * rows - 1, 0), col))
    cur = lambda col: pl.BlockSpec((tm, c), lambda i: (i, col))
    return pl.pallas_call(
        functools.partial(_conv_kernel, blocks_per_seq=seq // tm),
        grid=(t // tm,),
        in_specs=[cur(0), cur(1), cur(2), prev(1), prev(2),
                  pl.BlockSpec((CONV_WIDTH, c), lambda i: (0, 0))],
        out_specs=pl.BlockSpec((tm, c), lambda i: (i, 0)),
        out_shape=jax.ShapeDtypeStruct((t, c), BF16),
        scratch_shapes=[pltpu.VMEM((tm + SUB, c), F32)],
        compiler_params=_params("parallel"),
    )(proj, proj, proj, proj, proj, conv_w)


def _band_bias(rel_bias, tq, n_kblk):
    band = (ATT_PAST_CHUNKS + 1) * CHUNK
    heads = rel_bias.shape[0]
    rel_bias = rel_bias.astype(F32)
    n_far = band - REL_CLIP
    ext = jnp.concatenate([jnp.broadcast_to(rel_bias[:, -1:], (heads, n_far)), rel_bias[:, ::-1][:, 1:]], axis=1)
    rows = jnp.stack([ext[:, CHUNK - 1 - qi:CHUNK - 1 - qi + band] for qi in range(CHUNK)], axis=1)
    tk = n_kblk * tq
    neg = lambda w: jnp.full((heads, CHUNK, w), -1e30, F32)
    blocks = [jnp.concatenate([neg(c * CHUNK), rows, neg(tk - band - c * CHUNK)], axis=2)
              for c in range(tq // CHUNK)]
    base = jnp.concatenate(blocks, axis=1)
    kblk = np.arange(tk)[None, :] // tq
    variants = [jnp.where(kblk + v >= n_kblk - 1, base, -1e30) for v in range(n_kblk)]
    return jnp.stack(variants, axis=0)


def _attn_kernel(q_ref, *refs, n_kblk, heads, scale):
    k_refs, v_refs = refs[:n_kblk], refs[n_kblk:2 * n_kblk]
    bias_ref, o_ref = refs[2 * n_kblk], refs[2 * n_kblk + 1]
    dh = q_ref.shape[1] // heads
    for h in range(heads):
        cols = slice(h * dh, (h + 1) * dh)
        q = (q_ref[:, cols] * scale).astype(BF16)
        k = jnp.concatenate([r[:, cols] for r in k_refs], axis=0).astype(BF16)
        v = jnp.concatenate([r[:, cols] for r in v_refs], axis=0).astype(BF16)
        s = _nt_dot(q, k) + bias_ref[h]
        m = jnp.max(s, axis=-1, keepdims=True)
        p = jnp.exp(s - m)
        l = jnp.sum(p, axis=-1, keepdims=True)
        o = jnp.dot(p.astype(BF16), v, preferred_element_type=F32)
        o_ref[:, cols] = (o / l).astype(o_ref.dtype)


def band_attention(proj, rel_bias, batch, seq, col0, *, tq=256):
    t = proj.shape[0]
    heads = rel_bias.shape[0]
    dh = LANES
    width = heads * dh
    n_kblk = ATT_PAST_CHUNKS * CHUNK // tq + 1
    nq = seq // tq
    bias = _band_bias(rel_bias, tq, n_kblk)
    cq = col0 // width

    def kv_spec(cblk, j):
        return pl.BlockSpec(
            (tq, width), lambda b, i: (b * nq + jnp.maximum(i - (n_kblk - 1) + j, 0), cblk))

    in_specs = [pl.BlockSpec((tq, width), lambda b, i: (b * nq + i, cq))]
    in_specs += [kv_spec(cq + 1, j) for j in range(n_kblk)]
    in_specs += [kv_spec(cq + 2, j) for j in range(n_kblk)]
    in_specs += [pl.BlockSpec((None, heads, tq, n_kblk * tq),
                              lambda b, i: (jnp.minimum(i, n_kblk - 1), 0, 0, 0))]
    return pl.pallas_call(
        functools.partial(_attn_kernel, n_kblk=n_kblk, heads=heads, scale=dh ** -0.5),
        grid=(batch, nq),
        in_specs=in_specs,
        out_specs=pl.BlockSpec((tq, width), lambda b, i: (b * nq + i, 0)),
        out_shape=jax.ShapeDtypeStruct((t, width), BF16),
        compiler_params=_params("parallel", "arbitrary"),
    )(*([proj] * (1 + 2 * n_kblk)), bias)


def _outproj_kernel(ya_ref, yb_ref, w_ref, h_ref, g_ref, o_ref):
    y = jnp.concatenate([ya_ref[...], yb_ref[...]], axis=1)
    y = jnp.dot(y, w_ref[...], preferred_element_type=F32)
    o_ref[...] = h_ref[...] + _rms(y, g_ref[...])


def outproj_residual(ya, yb, w, layer, h, g, *, tm=512):
    t, d = h.shape
    row = lambda a: pl.BlockSpec((tm, a.shape[1]), lambda i: (i, 0))
    return pl.pallas_call(
        _outproj_kernel, grid=(t // tm,),
        in_specs=[row(ya), row(yb), pl.BlockSpec((None,) + w.shape[1:], lambda i: (layer, 0, 0)), row(h),
                  pl.BlockSpec((1, d), lambda i: (0, 0))],
        out_specs=row(h),
        out_shape=jax.ShapeDtypeStruct((t, d), F32),
        compiler_params=_params("parallel"),
    )(ya, yb, w, h, g)


def _mlp_kernel(h_ref, g_in_ref, w1_ref, w2_ref, g_out_ref, o_ref, u_ref):
    j = pl.program_id(1)

    @pl.when(j == 0)
    def _():
        u_ref[...] = _rms(h_ref[...], g_in_ref[...]).astype(BF16)
        o_ref[...] = jnp.zeros_like(o_ref)

    a = jnp.dot(u_ref[...], w1_ref[...], preferred_element_type=F32)
    a = jnp.square(jnp.maximum(a, 0.0)).astype(BF16)
    o_ref[...] += jnp.dot(a, w2_ref[...], preferred_element_type=F32)

    @pl.when(j == pl.num_programs(1) - 1)
    def _():
        o_ref[...] = h_ref[...] + _rms(o_ref[...], g_out_ref[...])


def mlp_residual(h, g_in, w1, w2, layer, g_out, *, tm=1024, tf=512):
    t, d = h.shape
    ff = w1.shape[2]
    vec = pl.BlockSpec((1, d), lambda i, j: (0, 0))
    return pl.pallas_call(
        _mlp_kernel, grid=(t // tm, ff // tf),
        in_specs=[pl.BlockSpec((tm, d), lambda i, j: (i, 0)), vec,
                  pl.BlockSpec((None, d, tf), lambda i, j: (layer, 0, j)),
                  pl.BlockSpec((None, tf, d), lambda i, j: (layer, j, 0)), vec],
        out_specs=pl.BlockSpec((tm, d), lambda i, j: (i, 0)),
        out_shape=jax.ShapeDtypeStruct((t, d), F32),
        scratch_shapes=[pltpu.VMEM((tm, d), BF16)],
        compiler_params=_params("parallel", "arbitrary"),
    )(h, g_in, w1, w2, g_out)


def _split3_bf16(x):
    hi = x.astype(BF16)
    r = x - hi.astype(F32)
    mid = r.astype(BF16)
    lo = (r - mid.astype(F32)).astype(BF16)
    return hi, mid, lo


def _gated_recurrence_block(q, k, v, log_a, st_ref):
    r, dk = q.shape
    nch, nsb = r // CHUNK, r // SUB
    row = lax.broadcasted_iota(jnp.int32, (CHUNK, CHUNK), 0)
    col = lax.broadcasted_iota(jnp.int32, (CHUNK, CHUNK), 1)
    tri = (row >= col).astype(BF16)
    parts = _split3_bf16(log_a * LOG2_E)
    b = jnp.concatenate(
        [sum(jnp.dot(tri, p[c * CHUNK:(c + 1) * CHUNK], preferred_element_type=F32) for p in parts)
         for c in range(nch)], axis=0)

    b3 = b.reshape(nch, CHUNK, dk)
    q3 = q.reshape(nch, CHUNK, dk)
    b_last = b3[:, CHUNK - 1:CHUNK, :]
    qb = (q * jnp.exp2(b)).astype(BF16)
    kb = (k.reshape(nch, CHUNK, dk) * jnp.exp2(b_last - b3)).reshape(r, dk).astype(BF16)
    d_last = jnp.exp2(b_last)

    b4 = b.reshape(nsb, SUB, dk)
    q4 = q.reshape(nsb, SUB, dk)
    k4 = k.reshape(nsb, SUB, dk)

    kt3 = (k4 * jnp.exp2(b4[:, SUB - 1:SUB, :] - b4)).reshape(nch, CHUNK, dk).astype(BF16)
    sub_of_row = lax.broadcasted_iota(jnp.int32, (1, CHUNK, 1), 1) // SUB
    q_slots, k_slots = [], []
    for j in range(N_SUB - 1):
        lo = (j + 1) * SUB
        end_j = b3[:, lo - 1:lo, :]
        qt = (q3[:, lo:, :] * jnp.exp2(b3[:, lo:, :] - end_j)).astype(BF16)
        q_slots.append(jnp.concatenate([jnp.zeros((nch, lo, dk), BF16), qt], axis=1))
        k_slots.append(jnp.where(sub_of_row == j, kt3, jnp.zeros_like(kt3)))
    q_cat = jnp.concatenate(q_slots, axis=2)
    k_cat = jnp.concatenate(k_slots, axis=2)

    t_in = lax.broadcasted_iota(jnp.int32, (1, SUB, 1), 1)
    prods = []
    for s in range(SUB):
        arg = jnp.where(t_in >= s, b4 - b4[:, s:s + 1, :], -jnp.inf)
        prods.append((q4 * k4[:, s:s + 1, :] * jnp.exp2(arg)).reshape(r, dk).astype(BF16))
    slot = lax.broadcasted_iota(jnp.int32, (SUB * dk, LANES), 0) // dk
    to_lane = (slot == lax.broadcasted_iota(jnp.int32, (SUB * dk, LANES), 1)).astype(BF16)
    diag = jnp.dot(jnp.concatenate(prods, axis=1), to_lane, preferred_element_type=F32)
    diag4 = diag.reshape(nch, N_SUB, SUB, LANES)
    diag = jnp.stack(
        [diag4[:, 0]] + [pltpu.roll(diag4[:, j].reshape(nch * SUB, LANES), j * SUB, axis=1)
                         .reshape(nch, SUB, LANES) for j in range(1, N_SUB)], axis=1).reshape(r, LANES)

    vb = v.astype(BF16)
    outs = []
    st = st_ref[...]
    for c in range(nch):
        rows = slice(c * CHUNK, (c + 1) * CHUNK)
        scores = _nt_dot(q_cat[c], k_cat[c]) + diag[rows, :CHUNK]
        o = jnp.dot(scores.astype(BF16), vb[rows], preferred_element_type=F32)
        o = o + _nt_dot(qb[rows], st.astype(BF16))
        outs.append(o)
        upd = jnp.dot(v[rows].T.astype(BF16), kb[rows], preferred_element_type=F32)
        st = st * d_last[c] + upd
    st_ref[...] = st
    return jnp.concatenate(outs, axis=0)


def _hgrn_kernel(q_ref, f_ref, i_ref, g_ref, lb_ref, ng_ref, o_ref, st_ref, *, layer):
    @pl.when(pl.program_id(2) == 0)
    def _():
        st_ref[...] = jnp.zeros_like(st_ref)

    lb_raw = lb_ref[...]
    e = jnp.exp(lb_raw - jnp.max(lb_raw, axis=0, keepdims=True))
    soft = e / jnp.sum(e, axis=0, keepdims=True)
    lb = jnp.sum(soft[:layer + 1], axis=0, keepdims=True) - soft[0:1]

    q_raw = q_ref[...]
    q = q_raw * _sigmoid(q_raw)
    f = lb + (1.0 - lb) * _sigmoid(f_ref[...])
    o = _gated_recurrence_block(q, 1.0 - f, i_ref[...], jnp.log(f), st_ref)
    g_raw = g_ref[...]
    o_ref[...] = (_rms(o, ng_ref[...]) * (g_raw * _sigmoid(g_raw))).astype(o_ref.dtype)


def hgrn2_mixer(proj, lb, norm_g, batch, seq, layer, *, rows=512):
    t = proj.shape[0]
    d = LANES
    nblk = seq // rows
    col = lambda base: pl.BlockSpec((rows, d), lambda b, h, i: (b * nblk + i, base + h))
    return pl.pallas_call(
        functools.partial(_hgrn_kernel, layer=layer),
        grid=(batch, HGRN_HEADS, nblk),
        in_specs=[col(0), col(HGRN_HEADS), col(2 * HGRN_HEADS), col(3 * HGRN_HEADS),
                  pl.BlockSpec((lb.shape[0], d), lambda b, h, i: (0, h)),
                  pl.BlockSpec((1, d), lambda b, h, i: (0, h))],
        out_specs=pl.BlockSpec((rows, d), lambda b, h, i: (b * nblk + i, h)),
        out_shape=jax.ShapeDtypeStruct((t, HGRN_HEADS * d), BF16),
        scratch_shapes=[pltpu.VMEM((d, d), F32)],
        compiler_params=_params("parallel", "parallel", "arbitrary"),
    )(proj, proj, proj, proj, lb, norm_g)


def _gla_kernel(q_ref, k_ref, v_ref, r_ref, a_ref, wa_ref, ba_ref, ng_ref, o_ref, st_ref, *, q_scale):
    @pl.when(pl.program_id(2) == 0)
    def _():
        st_ref[...] = jnp.zeros_like(st_ref)

    x = jnp.dot(a_ref[...].astype(BF16), wa_ref[...], preferred_element_type=F32) + ba_ref[...]
    log_sig = jnp.minimum(x, 0.0) - jnp.log(1.0 + jnp.exp(-jnp.abs(x)))
    log_a = log_sig / GLA_GATE_NORMALIZER
    o = _gated_recurrence_block(q_ref[...] * q_scale, k_ref[...], v_ref[...], log_a, st_ref)
    r_raw = r_ref[...]
    o_ref[...] = (_rms(o, ng_ref[...]) * (r_raw * _sigmoid(r_raw))).astype(o_ref.dtype)


def gla_mixer(proj, a_lr, wa2, ba, norm_g, batch, seq, col0, *, rows=512):
    t = proj.shape[0]
    dk = LANES
    dv = 2 * dk
    heads = GLA_HEADS
    nblk = seq // rows
    cq, ck = col0 // dk, col0 // dk + heads
    cv = (col0 + 2 * heads * dk) // dv
    cr = cv + heads
    spec = lambda w, base: pl.BlockSpec((rows, w), lambda b, h, i: (b * nblk + i, base + h))
    return pl.pallas_call(
        functools.partial(_gla_kernel, q_scale=dk ** -0.5),
        grid=(batch, heads, nblk),
        in_specs=[spec(dk, cq), spec(dk, ck), spec(dv, cv), spec(dv, cr),
                  pl.BlockSpec((rows, a_lr.shape[1]), lambda b, h, i: (b * nblk + i, 0)),
                  pl.BlockSpec((wa2.shape[0], dk), lambda b, h, i: (0, h)),
                  pl.BlockSpec((1, dk), lambda b, h, i: (0, h)),
                  pl.BlockSpec((1, dv), lambda b, h, i: (0, h))],
        out_specs=pl.BlockSpec((rows, dv), lambda b, h, i: (b * nblk + i, h)),
        out_shape=jax.ShapeDtypeStruct((t, heads * dv), BF16),
        scratch_shapes=[pltpu.VMEM((dv, dk), F32)],
        compiler_params=_params("parallel", "parallel", "arbitrary"),
    )(proj, proj, proj, proj, a_lr, wa2, ba, norm_g)


def kernel(x, norm_g, even_w_in, even_conv_w, even_rel_bias, even_w_out, odd_w_in, hgrn_lb, hgrn_norm_g,
           gla_wa2, gla_ba, gla_norm_g, odd_w_out, mlp_w1, mlp_w2):
    batch, seq, d = x.shape
    depth = norm_g.shape[0]
    half = d // 2
    even_w_in_b, even_w_out_b = even_w_in.astype(BF16), even_w_out.astype(BF16)
    odd_w_in_b, odd_w_out_b = odd_w_in.astype(BF16), odd_w_out.astype(BF16)
    mlp_w1_b, mlp_w2_b = mlp_w1.astype(BF16), mlp_w2.astype(BF16)
    n_odd_main = odd_w_in.shape[2] - GLA_GATE_RANK
    lane_pad = LANES - GLA_GATE_RANK
    h = x.reshape(batch * seq, d)
    for l in range(depth):
        g = norm_g[l][:, None, :]
        e = l // 2
        if l % 2 == 0:
            proj = rms_matmul(h, g[0], even_w_in_b, e, even_w_in.shape[2])
            ya = gated_short_conv(proj, even_conv_w[e], seq)
            yb = band_attention(proj, even_rel_bias[e], batch, seq, CONV_WIDTH * half)
            w_out = even_w_out_b
        else:
            w_gate = jnp.pad(odd_w_in[e][:, n_odd_main:], ((0, 0), (0, lane_pad))).astype(BF16)
            proj, a_lr = rms_matmul(h, g[0], odd_w_in_b, e, n_odd_main, w_gate, tn=n_odd_main // 4)
            ya = hgrn2_mixer(proj, hgrn_lb, hgrn_norm_g[e][None, :], batch, seq, l)
            wa2 = jnp.pad(gla_wa2[e], ((0, lane_pad), (0, 0))).astype(BF16)
            yb = gla_mixer(proj, a_lr, wa2, gla_ba[e][None, :], gla_norm_g[e][None, :], batch, seq,
                           4 * HGRN_HEADS * LANES)
            w_out = odd_w_out_b
        h = outproj_residual(ya, yb, w_out, e, h, g[1])
        h = mlp_residual(h, g[2], mlp_w1_b, mlp_w2_b, l, g[3])
    return h.reshape(batch, seq, d)
```

```python
import functools

import numpy as np
import jax
import jax.numpy as jnp
from jax import lax
from jax.experimental import pallas as pl
from jax.experimental.pallas import tpu as pltpu

F32 = jnp.float32
BF16 = jnp.bfloat16

EPS = 1e-6
LOG2_E = 1.4426950408889634
CHUNK = 64
SUB = 8
LANES = 128
N_SUB = CHUNK // SUB

CONV_WIDTH = 3
ATT_HEADS = 8
ATT_PAST_CHUNKS = 8
REL_CLIP = 256
HGRN_HEADS = 8
GLA_HEADS = 4
GLA_GATE_RANK = 16
GLA_GATE_NORMALIZER = 16.0

VMEM_LIMIT = 56 * 1024 * 1024


def _params(*sem):
    return pltpu.CompilerParams(dimension_semantics=sem, vmem_limit_bytes=VMEM_LIMIT)


def _rms(x, g):
    ms = jnp.mean(x * x, axis=-1, keepdims=True)
    return x * lax.rsqrt(ms + EPS) * g


def _sigmoid(x):
    return 1.0 / (1.0 + jnp.exp(-x))


def _nt_dot(a, b):
    return lax.dot_general(a, b, (((1,), (1,)), ((), ())), preferred_element_type=F32)


def _rms_matmul_kernel(h_ref, g_ref, w_ref, *refs, ncol):
    if len(refs) == 4:
        wx_ref, o_ref, ox_ref, u_ref = refs
    else:
        (o_ref, u_ref), wx_ref, ox_ref = refs, None, None
    j = pl.program_id(1)
    tn = o_ref.shape[1]

    @pl.when(j == 0)
    def _():
        u = _rms(h_ref[...], g_ref[...]).astype(BF16)
        u_ref[...] = u
        if wx_ref is not None:
            ox_ref[...] = jnp.dot(u, wx_ref[...], preferred_element_type=F32)

    for jj in range(ncol):
        @pl.when(j == jj)
        def _():
            o_ref[...] = jnp.dot(u_ref[...], w_ref[:, jj * tn:(jj + 1) * tn], preferred_element_type=F32)


def rms_matmul(h, g, w, layer, n, w_extra=None, *, tm=512, tn=2048):
    t, d = h.shape
    ncol = n // tn
    resident = pl.Buffered(1)
    in_specs = [pl.BlockSpec((tm, d), lambda i, j: (i, 0)),
                pl.BlockSpec((1, d), lambda i, j: (0, 0)),
                pl.BlockSpec((None, d, w.shape[2]), lambda i, j: (layer, 0, 0), pipeline_mode=resident)]
    out_specs = [pl.BlockSpec((tm, tn), lambda i, j: (i, j))]
    out_shape = [jax.ShapeDtypeStruct((t, n), F32)]
    args = [h, g, w]
    if w_extra is not None:
        nx = w_extra.shape[1]
        in_specs.append(pl.BlockSpec((d, nx), lambda i, j: (0, 0), pipeline_mode=resident))
        out_specs.append(pl.BlockSpec((tm, nx), lambda i, j: (i, 0)))
        out_shape.append(jax.ShapeDtypeStruct((t, nx), F32))
        args.append(w_extra)
    out = pl.pallas_call(
        functools.partial(_rms_matmul_kernel, ncol=ncol), grid=(t // tm, ncol),
        in_specs=in_specs, out_specs=out_specs, out_shape=out_shape,
        scratch_shapes=[pltpu.VMEM((tm, d), BF16)],
        compiler_params=_params("parallel", "arbitrary"),
    )(*args)
    return out if w_extra is not None else out[0]


def _conv_kernel(b_ref, c_ref, hc_ref, cprev_ref, hprev_ref, w_ref, o_ref, u_ref, *, blocks_per_seq):
    tm = o_ref.shape[0]
    first = (pl.program_id(0) % blocks_per_seq) == 0
    prev = cprev_ref[...] * hprev_ref[...]
    u_ref[0:SUB, :] = jnp.where(first, 0.0, prev)
    u_ref[SUB:, :] = c_ref[...] * hc_ref[...]
    w = w_ref[...]
    y = w[0:1] * u_ref[pl.ds(SUB - 2, tm), :]
    for j in range(1, CONV_WIDTH):
        y = y + w[j:j + 1] * u_ref[pl.ds(SUB - 2 + j, tm), :]
    o_ref[...] = (b_ref[...] * y).astype(o_ref.dtype)


def gated_short_conv(proj, conv_w, seq, *, tm=512):
    t = proj.shape[0]
    c = conv_w.shape[1]
    rows = tm // SUB
    prev = lambda col: pl.BlockSpec((SUB, c), lambda i: (jnp.maximum(i * rows - 1, 0), col))
    cur = lambda col: pl.BlockSpec((tm, c), lambda i: (i, col))
    return pl.pallas_call(
        functools.partial(_conv_kernel, blocks_per_seq=seq // tm),
        grid=(t // tm,),
        in_specs=[cur(0), cur(1), cur(2), prev(1), prev(2),
                  pl.BlockSpec((CONV_WIDTH, c), lambda i: (0, 0))],
        out_specs=pl.BlockSpec((tm, c), lambda i: (i, 0)),
        out_shape=jax.ShapeDtypeStruct((t, c), BF16),
        scratch_shapes=[pltpu.VMEM((tm + SUB, c), F32)],
        compiler_params=_params("parallel"),
    )(proj, proj, proj, proj, proj, conv_w)


def _band_bias(rel_bias, tq, n_kblk):
    band = (ATT_PAST_CHUNKS + 1) * CHUNK
    heads = rel_bias.shape[0]
    rel_bias = rel_bias.astype(F32) * LOG2_E
    n_far = band - REL_CLIP
    ext = jnp.concatenate([jnp.broadcast_to(rel_bias[:, -1:], (heads, n_far)), rel_bias[:, ::-1][:, 1:]], axis=1)
    rows = jnp.stack([ext[:, CHUNK - 1 - qi:CHUNK - 1 - qi + band] for qi in range(CHUNK)], axis=1)
    tk = n_kblk * tq
    neg = lambda w: jnp.full((heads, CHUNK, w), -1e30, F32)
    blocks = [jnp.concatenate([neg(c * CHUNK), rows, neg(tk - band - c * CHUNK)], axis=2)
              for c in range(tq // CHUNK)]
    base = jnp.concatenate(blocks, axis=1)
    kblk = np.arange(tk)[None, :] // tq
    variants = [jnp.where(kblk + v >= n_kblk - 1, base, -1e30) for v in range(n_kblk)]
    return jnp.stack(variants, axis=0)


def _attn_kernel(q_ref, *refs, n_kblk, heads, scale):
    k_refs, v_refs = refs[:n_kblk], refs[n_kblk:2 * n_kblk]
    bias_ref, o_ref = refs[2 * n_kblk], refs[2 * n_kblk + 1]
    dh = q_ref.shape[1] // heads
    head_cols = [slice(h * dh, (h + 1) * dh) for h in range(heads)]
    scores = []
    for h, cols in enumerate(head_cols):
        q = (q_ref[:, cols] * scale).astype(BF16)
        k = jnp.concatenate([r[:, cols] for r in k_refs], axis=0).astype(BF16)
        scores.append(_nt_dot(q, k) + bias_ref[h])
    for cols, s in zip(head_cols, scores):
        v = jnp.concatenate([r[:, cols] for r in v_refs], axis=0).astype(BF16)
        m = jnp.max(s, axis=-1, keepdims=True)
        p = jnp.exp2(s - m)
        l = jnp.sum(p, axis=-1, keepdims=True)
        o = jnp.dot(p.astype(BF16), v, preferred_element_type=F32)
        o_ref[:, cols] = (o / l).astype(o_ref.dtype)


def band_attention(proj, rel_bias, batch, seq, col0, *, tq=256):
    t = proj.shape[0]
    heads = rel_bias.shape[0]
    dh = LANES
    width = heads * dh
    n_kblk = ATT_PAST_CHUNKS * CHUNK // tq + 1
    nq = seq // tq
    bias = _band_bias(rel_bias, tq, n_kblk)
    cq = col0 // width

    def kv_spec(cblk, j):
        return pl.BlockSpec(
            (tq, width), lambda b, i: (b * nq + jnp.maximum(i - (n_kblk - 1) + j, 0), cblk))

    in_specs = [pl.BlockSpec((tq, width), lambda b, i: (b * nq + i, cq))]
    in_specs += [kv_spec(cq + 1, j) for j in range(n_kblk)]
    in_specs += [kv_spec(cq + 2, j) for j in range(n_kblk)]
    in_specs += [pl.BlockSpec((None, heads, tq, n_kblk * tq),
                              lambda b, i: (jnp.minimum(i, n_kblk - 1), 0, 0, 0))]
    return pl.pallas_call(
        functools.partial(_attn_kernel, n_kblk=n_kblk, heads=heads, scale=dh ** -0.5 * LOG2_E),
        grid=(batch, nq),
        in_specs=in_specs,
        out_specs=pl.BlockSpec((tq, width), lambda b, i: (b * nq + i, 0)),
        out_shape=jax.ShapeDtypeStruct((t, width), BF16),
        compiler_params=_params("parallel", "arbitrary"),
    )(*([proj] * (1 + 2 * n_kblk)), bias)


def _outproj_kernel(ya_ref, yb_ref, w_ref, h_ref, g_ref, o_ref):
    y = jnp.concatenate([ya_ref[...], yb_ref[...]], axis=1)
    y = jnp.dot(y, w_ref[...], preferred_element_type=F32)
    o_ref[...] = h_ref[...] + _rms(y, g_ref[...])


def outproj_residual(ya, yb, w, layer, h, g, *, tm=512):
    t, d = h.shape
    row = lambda a: pl.BlockSpec((tm, a.shape[1]), lambda i: (i, 0))
    return pl.pallas_call(
        _outproj_kernel, grid=(t // tm,),
        in_specs=[row(ya), row(yb), pl.BlockSpec((None,) + w.shape[1:], lambda i: (layer, 0, 0)), row(h),
                  pl.BlockSpec((1, d), lambda i: (0, 0))],
        out_specs=row(h),
        out_shape=jax.ShapeDtypeStruct((t, d), F32),
        compiler_params=_params("parallel"),
    )(ya, yb, w, h, g)


def _mlp_kernel(h_ref, g_in_ref, w1_ref, w2_ref, g_out_ref, o_ref, u_ref):
    j = pl.program_id(1)

    @pl.when(j == 0)
    def _():
        u_ref[...] = _rms(h_ref[...], g_in_ref[...]).astype(BF16)
        o_ref[...] = jnp.zeros_like(o_ref)

    a = jnp.dot(u_ref[...], w1_ref[...], preferred_element_type=F32)
    a = jnp.square(jnp.maximum(a, 0.0)).astype(BF16)
    o_ref[...] += jnp.dot(a, w2_ref[...], preferred_element_type=F32)

    @pl.when(j == pl.num_programs(1) - 1)
    def _():
        o_ref[...] = h_ref[...] + _rms(o_ref[...], g_out_ref[...])


def mlp_residual(h, g_in, w1, w2, layer, g_out, *, tm=1024, tf=512):
    t, d = h.shape
    ff = w1.shape[2]
    vec = pl.BlockSpec((1, d), lambda i, j: (0, 0))
    return pl.pallas_call(
        _mlp_kernel, grid=(t // tm, ff // tf),
        in_specs=[pl.BlockSpec((tm, d), lambda i, j: (i, 0)), vec,
                  pl.BlockSpec((None, d, tf), lambda i, j: (layer, 0, j)),
                  pl.BlockSpec((None, tf, d), lambda i, j: (layer, j, 0)), vec],
        out_specs=pl.BlockSpec((tm, d), lambda i, j: (i, 0)),
        out_shape=jax.ShapeDtypeStruct((t, d), F32),
        scratch_shapes=[pltpu.VMEM((tm, d), BF16)],
        compiler_params=_params("parallel", "arbitrary"),
    )(h, g_in, w1, w2, g_out)


def _split3_bf16(x):
    hi = x.astype(BF16)
    r = x - hi.astype(F32)
    mid = r.astype(BF16)
    lo = (r - mid.astype(F32)).astype(BF16)
    return hi, mid, lo


def _recurrence_operands(q, k, v, log_a):
    r, dk = q.shape
    nch, nsb = r // CHUNK, r // SUB
    row = lax.broadcasted_iota(jnp.int32, (CHUNK, CHUNK), 0)
    col = lax.broadcasted_iota(jnp.int32, (CHUNK, CHUNK), 1)
    tri = (row >= col).astype(BF16)
    parts = _split3_bf16(log_a * LOG2_E)
    b = jnp.concatenate(
        [sum(jnp.dot(tri, p[c * CHUNK:(c + 1) * CHUNK], preferred_element_type=F32) for p in parts)
         for c in range(nch)], axis=0)

    b3 = b.reshape(nch, CHUNK, dk)
    q3 = q.reshape(nch, CHUNK, dk)
    b_last = b3[:, CHUNK - 1:CHUNK, :]
    qb = (q * jnp.exp2(b)).astype(BF16)
    kb = (k.reshape(nch, CHUNK, dk) * jnp.exp2(b_last - b3)).reshape(r, dk).astype(BF16)
    d_last = jnp.exp2(b_last)

    b4 = b.reshape(nsb, SUB, dk)
    q4 = q.reshape(nsb, SUB, dk)
    k4 = k.reshape(nsb, SUB, dk)

    kt3 = (k4 * jnp.exp2(b4[:, SUB - 1:SUB, :] - b4)).reshape(nch, CHUNK, dk).astype(BF16)
    sub_of_row = lax.broadcasted_iota(jnp.int32, (1, CHUNK, 1), 1) // SUB
    q_slots, k_slots = [], []
    for j in range(N_SUB - 1):
        lo = (j + 1) * SUB
        end_j = b3[:, lo - 1:lo, :]
        qt = (q3[:, lo:, :] * jnp.exp2(b3[:, lo:, :] - end_j)).astype(BF16)
        q_slots.append(jnp.concatenate([jnp.zeros((nch, lo, dk), BF16), qt], axis=1))
        k_slots.append(jnp.where(sub_of_row == j, kt3, jnp.zeros_like(kt3)))
    q_cat = jnp.concatenate(q_slots, axis=2)
    k_cat = jnp.concatenate(k_slots, axis=2)

    t_in = lax.broadcasted_iota(jnp.int32, (1, SUB, 1), 1)
    prods = []
    for s in range(SUB):
        arg = jnp.where(t_in >= s, b4 - b4[:, s:s + 1, :], -jnp.inf)
        prods.append((q4 * k4[:, s:s + 1, :] * jnp.exp2(arg)).reshape(r, dk).astype(BF16))
    slot = lax.broadcasted_iota(jnp.int32, (SUB * dk, LANES), 0) // dk
    to_lane = (slot == lax.broadcasted_iota(jnp.int32, (SUB * dk, LANES), 1)).astype(BF16)
    diag = jnp.dot(jnp.concatenate(prods, axis=1), to_lane, preferred_element_type=F32)
    diag4 = diag.reshape(nch, N_SUB, SUB, LANES)
    diag = jnp.stack(
        [diag4[:, 0]] + [pltpu.roll(diag4[:, j].reshape(nch * SUB, LANES), j * SUB, axis=1)
                         .reshape(nch, SUB, LANES) for j in range(1, N_SUB)], axis=1).reshape(r, LANES)

    return dict(q_cat=q_cat, k_cat=k_cat, diag=diag, qb=qb, kb=kb, d_last=d_last, v=v, vb=v.astype(BF16))


def _gated_recurrence_heads(heads, st_ref):
    ops = [_recurrence_operands(*hd) for hd in heads]
    nch = ops[0]["q_cat"].shape[0]
    chunk = lambda c: slice(c * CHUNK, (c + 1) * CHUNK)
    scores = [[_nt_dot(p["q_cat"][c], p["k_cat"][c]) + p["diag"][chunk(c), :CHUNK] for p in ops]
              for c in range(nch)]
    intra = [[jnp.dot(scores[c][i].astype(BF16), p["vb"][chunk(c)], preferred_element_type=F32)
              for i, p in enumerate(ops)] for c in range(nch)]
    upd = [[jnp.dot(p["v"][chunk(c)].T.astype(BF16), p["kb"][chunk(c)], preferred_element_type=F32)
            for p in ops] for c in range(nch)]
    sts = [st_ref[i] for i in range(len(ops))]
    outs = [[] for _ in ops]
    for c in range(nch):
        for i, p in enumerate(ops):
            outs[i].append(intra[c][i] + _nt_dot(p["qb"][chunk(c)], sts[i].astype(BF16)))
            sts[i] = sts[i] * p["d_last"][c] + upd[c][i]
    for i, st in enumerate(sts):
        st_ref[i] = st
    return [jnp.concatenate(o, axis=0) for o in outs]


def _hgrn_kernel(q_ref, f_ref, i_ref, g_ref, lb_ref, ng_ref, o_ref, st_ref, *, layer):
    @pl.when(pl.program_id(2) == 0)
    def _():
        st_ref[...] = jnp.zeros_like(st_ref)

    lb_raw = lb_ref[...]
    e = jnp.exp(lb_raw - jnp.max(lb_raw, axis=0, keepdims=True))
    soft = e / jnp.sum(e, axis=0, keepdims=True)
    lb_all = jnp.sum(soft[:layer + 1], axis=0, keepdims=True) - soft[0:1]

    d = LANES
    head_cols = [slice(hh * d, (hh + 1) * d) for hh in range(st_ref.shape[0])]
    heads = []
    for cols in head_cols:
        lb = lb_all[:, cols]
        q_raw = q_ref[:, cols]
        f = lb + (1.0 - lb) * _sigmoid(f_ref[:, cols])
        heads.append((q_raw * _sigmoid(q_raw), 1.0 - f, i_ref[:, cols], jnp.log(f)))
    for cols, o in zip(head_cols, _gated_recurrence_heads(heads, st_ref)):
        g_raw = g_ref[:, cols]
        o_ref[:, cols] = (_rms(o, ng_ref[:, cols]) * (g_raw * _sigmoid(g_raw))).astype(o_ref.dtype)


def hgrn2_mixer(proj, lb, norm_g, batch, seq, layer, *, rows=512, heads_per_step=4):
    t = proj.shape[0]
    d = LANES
    w = heads_per_step * d
    groups = HGRN_HEADS // heads_per_step
    nblk = seq // rows
    col = lambda base: pl.BlockSpec((rows, w), lambda b, h, i: (b * nblk + i, base + h))
    return pl.pallas_call(
        functools.partial(_hgrn_kernel, layer=layer),
        grid=(batch, groups, nblk),
        in_specs=[col(0), col(groups), col(2 * groups), col(3 * groups),
                  pl.BlockSpec((lb.shape[0], w), lambda b, h, i: (0, h)),
                  pl.BlockSpec((1, w), lambda b, h, i: (0, h))],
        out_specs=pl.BlockSpec((rows, w), lambda b, h, i: (b * nblk + i, h)),
        out_shape=jax.ShapeDtypeStruct((t, HGRN_HEADS * d), BF16),
        scratch_shapes=[pltpu.VMEM((heads_per_step, d, d), F32)],
        compiler_params=_params("parallel", "parallel", "arbitrary"),
    )(proj, proj, proj, proj, lb, norm_g)


def _gla_kernel(q_ref, k_ref, v_ref, r_ref, a_ref, wa_ref, ba_ref, ng_ref, o_ref, st_ref, *, q_scale):
    @pl.when(pl.program_id(2) == 0)
    def _():
        st_ref[...] = jnp.zeros_like(st_ref)

    x = jnp.dot(a_ref[...].astype(BF16), wa_ref[...], preferred_element_type=F32) + ba_ref[...]
    log_sig = jnp.minimum(x, 0.0) - jnp.log(1.0 + jnp.exp(-jnp.abs(x)))
    log_a = log_sig / GLA_GATE_NORMALIZER
    nh, dv, dk = st_ref.shape
    k_cols = [slice(hh * dk, (hh + 1) * dk) for hh in range(nh)]
    v_cols = [slice(hh * dv, (hh + 1) * dv) for hh in range(nh)]
    heads = [(q_ref[:, kc] * q_scale, k_ref[:, kc], v_ref[:, vc], log_a[:, kc]) for kc, vc in zip(k_cols, v_cols)]
    for vc, o in zip(v_cols, _gated_recurrence_heads(heads, st_ref)):
        r_raw = r_ref[:, vc]
        o_ref[:, vc] = (_rms(o, ng_ref[:, vc]) * (r_raw * _sigmoid(r_raw))).astype(o_ref.dtype)


def gla_mixer(proj, a_lr, wa2, ba, norm_g, batch, seq, col0, *, rows=512, heads_per_step=4):
    t = proj.shape[0]
    dk = LANES
    dv = 2 * dk
    wk, wv = heads_per_step * dk, heads_per_step * dv
    groups = GLA_HEADS // heads_per_step
    nblk = seq // rows
    cq, ck = col0 // wk, col0 // wk + groups
    cv = (col0 + 2 * GLA_HEADS * dk) // wv
    cr = cv + groups
    spec = lambda w, base: pl.BlockSpec((rows, w), lambda b, h, i: (b * nblk + i, base + h))
    return pl.pallas_call(
        functools.partial(_gla_kernel, q_scale=dk ** -0.5),
        grid=(batch, groups, nblk),
        in_specs=[spec(wk, cq), spec(wk, ck), spec(wv, cv), spec(wv, cr),
                  pl.BlockSpec((rows, a_lr.shape[1]), lambda b, h, i: (b * nblk + i, 0)),
                  pl.BlockSpec((wa2.shape[0], wk), lambda b, h, i: (0, h)),
                  pl.BlockSpec((1, wk), lambda b, h, i: (0, h)),
                  pl.BlockSpec((1, wv), lambda b, h, i: (0, h))],
        out_specs=pl.BlockSpec((rows, wv), lambda b, h, i: (b * nblk + i, h)),
        out_shape=jax.ShapeDtypeStruct((t, GLA_HEADS * dv), BF16),
        scratch_shapes=[pltpu.VMEM((heads_per_step, dv, dk), F32)],
        compiler_params=_params("parallel", "parallel", "arbitrary"),
    )(proj, proj, proj, proj, a_lr, wa2, ba, norm_g)


def kernel(x, norm_g, even_w_in, even_conv_w, even_rel_bias, even_w_out, odd_w_in, hgrn_lb, hgrn_norm_g,
           gla_wa2, gla_ba, gla_norm_g, odd_w_out, mlp_w1, mlp_w2):
    batch, seq, d = x.shape
    depth = norm_g.shape[0]
    half = d // 2
    even_w_in_b, even_w_out_b = even_w_in.astype(BF16), even_w_out.astype(BF16)
    n_odd_main = odd_w_in.shape[2] - GLA_GATE_RANK
    odd_w_in_b, odd_w_out_b = odd_w_in[:, :, :n_odd_main].astype(BF16), odd_w_out.astype(BF16)
    mlp_w1_b, mlp_w2_b = mlp_w1.astype(BF16), mlp_w2.astype(BF16)
    lane_pad = LANES - GLA_GATE_RANK
    h = x.reshape(batch * seq, d)
    for l in range(depth):
        g = norm_g[l][:, None, :]
        e = l // 2
        if l % 2 == 0:
            proj = rms_matmul(h, g[0], even_w_in_b, e, even_w_in.shape[2])
            ya = gated_short_conv(proj, even_conv_w[e], seq)
            yb = band_attention(proj, even_rel_bias[e], batch, seq, CONV_WIDTH * half)
            w_out = even_w_out_b
        else:
            w_gate = jnp.pad(odd_w_in[e][:, n_odd_main:], ((0, 0), (0, lane_pad))).astype(BF16)
            proj, a_lr = rms_matmul(h, g[0], odd_w_in_b, e, n_odd_main, w_gate, tn=n_odd_main // 4)
            ya = hgrn2_mixer(proj, hgrn_lb, hgrn_norm_g[e][None, :], batch, seq, l)
            wa2 = jnp.pad(gla_wa2[e], ((0, lane_pad), (0, 0))).astype(BF16)
            yb = gla_mixer(proj, a_lr, wa2, gla_ba[e][None, :], gla_norm_g[e][None, :], batch, seq,
                           4 * HGRN_HEADS * LANES)
            w_out = odd_w_out_b
        h = outproj_residual(ya, yb, w_out, e, h, g[1])
        h = mlp_residual(h, g[2], mlp_w1_b, mlp_w2_b, l, g[3])
    return h.reshape(batch, seq, d)
```

```python
import functools

import numpy as np
import jax
import jax.numpy as jnp
from jax import lax
from jax.experimental import pallas as pl
from jax.experimental.pallas import tpu as pltpu

F32 = jnp.float32
BF16 = jnp.bfloat16

EPS = 1e-6
LOG2_E = 1.4426950408889634
CHUNK = 64
SUB = 8
LANES = 128
N_SUB = CHUNK // SUB

CONV_WIDTH = 3
ATT_HEADS = 8
ATT_PAST_CHUNKS = 8
REL_CLIP = 256
HGRN_HEADS = 8
GLA_HEADS = 4
GLA_GATE_RANK = 16
GLA_GATE_NORMALIZER = 16.0

VMEM_LIMIT = 56 * 1024 * 1024


def _params(*sem):
    return pltpu.CompilerParams(dimension_semantics=sem, vmem_limit_bytes=VMEM_LIMIT)


def _rms(x, g):
    ms = jnp.mean(x * x, axis=-1, keepdims=True)
    return x * lax.rsqrt(ms + EPS) * g


def _sigmoid(x):
    return 1.0 / (1.0 + jnp.exp(-x))


def _nt_dot(a, b):
    return lax.dot_general(a, b, (((1,), (1,)), ((), ())), preferred_element_type=F32)


def _rms_matmul_kernel(h_ref, g_ref, w_ref, *refs, col_steps, has_extra):
    u_ref = refs[-1]
    wx_ref, ox_ref = (refs[0], refs[-2]) if has_extra else (None, None)
    o_refs = refs[1:-2] if has_extra else refs[:-1]
    j = pl.program_id(1)

    @pl.when(j == 0)
    def _():
        u = _rms(h_ref[...], g_ref[...]).astype(BF16)
        u_ref[...] = u
        if has_extra:
            ox_ref[...] = jnp.dot(u, wx_ref[...], preferred_element_type=F32)

    for jj, (out_idx, col0, tn) in enumerate(col_steps):
        @pl.when(j == jj)
        def _():
            o_ref = o_refs[out_idx]
            y = jnp.dot(u_ref[...], w_ref[:, col0:col0 + tn], preferred_element_type=F32)
            o_ref[...] = y.astype(o_ref.dtype)


def rms_matmul(h, g, w, layer, outs, w_extra=None, *, tm=512):
    t, d = h.shape
    resident = pl.Buffered(1)
    in_specs = [pl.BlockSpec((tm, d), lambda i, j: (i, 0)),
                pl.BlockSpec((1, d), lambda i, j: (0, 0)),
                pl.BlockSpec((None, d, w.shape[2]), lambda i, j: (layer, 0, 0), pipeline_mode=resident)]
    args = [h, g, w]
    if w_extra is not None:
        in_specs.append(pl.BlockSpec(w_extra.shape, lambda i, j: (0, 0), pipeline_mode=resident))
        args.append(w_extra)
    out_specs, out_shape, col_steps = [], [], []
    col = 0
    for k, (ncols, dtype, nblk) in enumerate(outs):
        tn = ncols // nblk
        j0 = len(col_steps)
        out_specs.append(pl.BlockSpec(
            (tm, tn), lambda i, j, j0=j0, nblk=nblk: (i, jnp.clip(j - j0, 0, nblk - 1))))
        out_shape.append(jax.ShapeDtypeStruct((t, ncols), dtype))
        col_steps += [(k, col + b * tn, tn) for b in range(nblk)]
        col += ncols
    if w_extra is not None:
        nx = w_extra.shape[1]
        out_specs.append(pl.BlockSpec((tm, nx), lambda i, j: (i, 0)))
        out_shape.append(jax.ShapeDtypeStruct((t, nx), F32))
    return pl.pallas_call(
        functools.partial(_rms_matmul_kernel, col_steps=tuple(col_steps), has_extra=w_extra is not None),
        grid=(t // tm, len(col_steps)),
        in_specs=in_specs, out_specs=out_specs, out_shape=out_shape,
        scratch_shapes=[pltpu.VMEM((tm, d), BF16)],
        compiler_params=_params("parallel", "arbitrary"),
    )(*args)


def _conv_kernel(b_ref, c_ref, hc_ref, cprev_ref, hprev_ref, w_ref, o_ref, u_ref, *, blocks_per_seq):
    tm = o_ref.shape[0]
    first = (pl.program_id(0) % blocks_per_seq) == 0
    prev = cprev_ref[...] * hprev_ref[...]
    u_ref[0:SUB, :] = jnp.where(first, 0.0, prev)
    u_ref[SUB:, :] = c_ref[...] * hc_ref[...]
    w = w_ref[...]
    y = w[0:1] * u_ref[pl.ds(SUB - 2, tm), :]
    for j in range(1, CONV_WIDTH):
        y = y + w[j:j + 1] * u_ref[pl.ds(SUB - 2 + j, tm), :]
    o_ref[...] = (b_ref[...] * y).astype(o_ref.dtype)


def gated_short_conv(proj, conv_w, seq, *, tm=512):
    t = proj.shape[0]
    c = conv_w.shape[1]
    rows = tm // SUB
    prev = lambda col: pl.BlockSpec((SUB, c), lambda i: (jnp.maximum(i * rows - 1, 0), col))
    cur = lambda col: pl.BlockSpec((tm, c), lambda i: (i, col))
    return pl.pallas_call(
        functools.partial(_conv_kernel, blocks_per_seq=seq // tm),
        grid=(t // tm,),
        in_specs=[cur(0), cur(1), cur(2), prev(1), prev(2),
                  pl.BlockSpec((CONV_WIDTH, c), lambda i: (0, 0))],
        out_specs=pl.BlockSpec((tm, c), lambda i: (i, 0)),
        out_shape=jax.ShapeDtypeStruct((t, c), BF16),
        scratch_shapes=[pltpu.VMEM((tm + SUB, c), F32)],
        compiler_params=_params("parallel"),
    )(proj, proj, proj, proj, proj, conv_w)


def _band_bias(rel_bias, tq, n_kblk):
    band = (ATT_PAST_CHUNKS + 1) * CHUNK
    heads = rel_bias.shape[0]
    rel_bias = rel_bias.astype(F32) * LOG2_E
    n_far = band - REL_CLIP
    ext = jnp.concatenate([jnp.broadcast_to(rel_bias[:, -1:], (heads, n_far)), rel_bias[:, ::-1][:, 1:]], axis=1)
    rows = jnp.stack([ext[:, CHUNK - 1 - qi:CHUNK - 1 - qi + band] for qi in range(CHUNK)], axis=1)
    tk = n_kblk * tq
    neg = lambda w: jnp.full((heads, CHUNK, w), -1e30, F32)
    blocks = [jnp.concatenate([neg(c * CHUNK), rows, neg(tk - band - c * CHUNK)], axis=2)
              for c in range(tq // CHUNK)]
    base = jnp.concatenate(blocks, axis=1)
    kblk = np.arange(tk)[None, :] // tq
    variants = [jnp.where(kblk + v >= n_kblk - 1, base, -1e30) for v in range(n_kblk)]
    return jnp.stack(variants, axis=0)


def _attn_kernel(q_ref, *refs, n_kblk, heads, scale):
    k_refs, v_refs = refs[:n_kblk], refs[n_kblk:2 * n_kblk]
    bias_ref, o_ref = refs[2 * n_kblk], refs[2 * n_kblk + 1]
    dh = q_ref.shape[1] // heads
    head_cols = [slice(h * dh, (h + 1) * dh) for h in range(heads)]
    scores = []
    for h, cols in enumerate(head_cols):
        q = (q_ref[:, cols].astype(F32) * scale).astype(BF16)
        k = jnp.concatenate([r[:, cols] for r in k_refs], axis=0).astype(BF16)
        scores.append(_nt_dot(q, k) + bias_ref[h])
    for cols, s in zip(head_cols, scores):
        v = jnp.concatenate([r[:, cols] for r in v_refs], axis=0).astype(BF16)
        m = jnp.max(s, axis=-1, keepdims=True)
        p = jnp.exp2(s - m)
        l = jnp.sum(p, axis=-1, keepdims=True)
        o = jnp.dot(p.astype(BF16), v, preferred_element_type=F32)
        o_ref[:, cols] = (o / l).astype(o_ref.dtype)


def band_attention(proj, rel_bias, batch, seq, col0, *, tq=256):
    t = proj.shape[0]
    heads = rel_bias.shape[0]
    dh = LANES
    width = heads * dh
    n_kblk = ATT_PAST_CHUNKS * CHUNK // tq + 1
    nq = seq // tq
    bias = _band_bias(rel_bias, tq, n_kblk)
    cq = col0 // width

    def kv_spec(cblk, j):
        return pl.BlockSpec(
            (tq, width), lambda b, i: (b * nq + jnp.maximum(i - (n_kblk - 1) + j, 0), cblk))

    in_specs = [pl.BlockSpec((tq, width), lambda b, i: (b * nq + i, cq))]
    in_specs += [kv_spec(cq + 1, j) for j in range(n_kblk)]
    in_specs += [kv_spec(cq + 2, j) for j in range(n_kblk)]
    in_specs += [pl.BlockSpec((None, heads, tq, n_kblk * tq),
                              lambda b, i: (jnp.minimum(i, n_kblk - 1), 0, 0, 0))]
    return pl.pallas_call(
        functools.partial(_attn_kernel, n_kblk=n_kblk, heads=heads, scale=dh ** -0.5 * LOG2_E),
        grid=(batch, nq),
        in_specs=in_specs,
        out_specs=pl.BlockSpec((tq, width), lambda b, i: (b * nq + i, 0)),
        out_shape=jax.ShapeDtypeStruct((t, width), BF16),
        compiler_params=_params("parallel", "arbitrary"),
    )(*([proj] * (1 + 2 * n_kblk)), bias)


def _outproj_kernel(ya_ref, yb_ref, w_ref, h_ref, g_ref, o_ref):
    y = jnp.concatenate([ya_ref[...], yb_ref[...]], axis=1)
    y = jnp.dot(y, w_ref[...], preferred_element_type=F32)
    o_ref[...] = h_ref[...] + _rms(y, g_ref[...])


def outproj_residual(ya, yb, w, layer, h, g, *, tm=512):
    t, d = h.shape
    row = lambda a: pl.BlockSpec((tm, a.shape[1]), lambda i: (i, 0))
    return pl.pallas_call(
        _outproj_kernel, grid=(t // tm,),
        in_specs=[row(ya), row(yb), pl.BlockSpec((None,) + w.shape[1:], lambda i: (layer, 0, 0)), row(h),
                  pl.BlockSpec((1, d), lambda i: (0, 0))],
        out_specs=row(h),
        out_shape=jax.ShapeDtypeStruct((t, d), F32),
        compiler_params=_params("parallel"),
    )(ya, yb, w, h, g)


def _mlp_kernel(h_ref, g_in_ref, w1_ref, w2_ref, g_out_ref, o_ref, u_ref, *, parts):
    j = pl.program_id(1)
    last = pl.num_programs(1) - 1
    step = o_ref.shape[0] // parts
    row_parts = [slice(p * step, (p + 1) * step) for p in range(parts)]

    def ff_block(u):
        a = jnp.dot(u, w1_ref[...], preferred_element_type=F32)
        a = jnp.square(jnp.maximum(a, 0.0)).astype(BF16)
        return jnp.dot(a, w2_ref[...], preferred_element_type=F32)

    @pl.when(j == 0)
    def _():
        for rows in row_parts:
            u = _rms(h_ref[rows, :], g_in_ref[...]).astype(BF16)
            u_ref[rows, :] = u
            o_ref[rows, :] = ff_block(u)

    @pl.when((j > 0) & (j < last))
    def _():
        o_ref[...] += ff_block(u_ref[...])

    @pl.when(j == last)
    def _():
        for rows in row_parts:
            z = o_ref[rows, :] + ff_block(u_ref[rows, :])
            o_ref[rows, :] = h_ref[rows, :] + _rms(z, g_out_ref[...])


def mlp_residual(h, g_in, w1, w2, layer, g_out, *, tm=1024, tf=512, parts=2):
    t, d = h.shape
    ff = w1.shape[2]
    assert ff // tf >= 2, "first and last ff block are handled by different branches"
    vec = pl.BlockSpec((1, d), lambda i, j: (0, 0))
    return pl.pallas_call(
        functools.partial(_mlp_kernel, parts=parts), grid=(t // tm, ff // tf),
        in_specs=[pl.BlockSpec((tm, d), lambda i, j: (i, 0)), vec,
                  pl.BlockSpec((None, d, tf), lambda i, j: (layer, 0, j)),
                  pl.BlockSpec((None, tf, d), lambda i, j: (layer, j, 0)), vec],
        out_specs=pl.BlockSpec((tm, d), lambda i, j: (i, 0)),
        out_shape=jax.ShapeDtypeStruct((t, d), F32),
        scratch_shapes=[pltpu.VMEM((tm, d), BF16)],
        compiler_params=_params("parallel", "arbitrary"),
    )(h, g_in, w1, w2, g_out)


def _split3_bf16(x):
    hi = x.astype(BF16)
    r = x - hi.astype(F32)
    mid = r.astype(BF16)
    lo = (r - mid.astype(F32)).astype(BF16)
    return hi, mid, lo


def _recurrence_operands(q, k, v, log_a):
    r, dk = q.shape
    nch, nsb = r // CHUNK, r // SUB
    row = lax.broadcasted_iota(jnp.int32, (CHUNK, CHUNK), 0)
    col = lax.broadcasted_iota(jnp.int32, (CHUNK, CHUNK), 1)
    tri = (row >= col).astype(BF16)
    parts = _split3_bf16(log_a * LOG2_E)
    b = jnp.concatenate(
        [sum(jnp.dot(tri, p[c * CHUNK:(c + 1) * CHUNK], preferred_element_type=F32) for p in parts)
         for c in range(nch)], axis=0)

    b3 = b.reshape(nch, CHUNK, dk)
    q3 = q.reshape(nch, CHUNK, dk)
    b_last = b3[:, CHUNK - 1:CHUNK, :]
    qb = (q * jnp.exp2(b)).astype(BF16)
    kb = (k.reshape(nch, CHUNK, dk) * jnp.exp2(b_last - b3)).reshape(r, dk).astype(BF16)
    d_last = jnp.exp2(b_last)

    b4 = b.reshape(nsb, SUB, dk)
    q4 = q.reshape(nsb, SUB, dk)
    k4 = k.reshape(nsb, SUB, dk)

    kt3 = (k4 * jnp.exp2(b4[:, SUB - 1:SUB, :] - b4)).reshape(nch, CHUNK, dk).astype(BF16)
    sub_of_row = lax.broadcasted_iota(jnp.int32, (1, CHUNK, 1), 1) // SUB
    q_slots, k_slots = [], []
    for j in range(N_SUB - 1):
        lo = (j + 1) * SUB
        end_j = b3[:, lo - 1:lo, :]
        qt = (q3[:, lo:, :] * jnp.exp2(b3[:, lo:, :] - end_j)).astype(BF16)
        q_slots.append(jnp.concatenate([jnp.zeros((nch, lo, dk), BF16), qt], axis=1))
        k_slots.append(jnp.where(sub_of_row == j, kt3, jnp.zeros_like(kt3)))
    q_cat = jnp.concatenate(q_slots, axis=2)
    k_cat = jnp.concatenate(k_slots, axis=2)

    t_in = lax.broadcasted_iota(jnp.int32, (1, SUB, 1), 1)
    prods = []
    for s in range(SUB):
        arg = b4 - b4[:, s:s + 1, :]
        if s > 0:
            arg = jnp.where(t_in >= s, arg, -jnp.inf)
        prods.append((q4 * k4[:, s:s + 1, :] * jnp.exp2(arg)).reshape(r, dk).astype(BF16))
    slot = lax.broadcasted_iota(jnp.int32, (SUB * dk, LANES), 0) // dk
    to_lane = (slot == lax.broadcasted_iota(jnp.int32, (SUB * dk, LANES), 1)).astype(BF16)
    diag = jnp.dot(jnp.concatenate(prods, axis=1), to_lane, preferred_element_type=F32)
    diag4 = diag.reshape(nch, N_SUB, SUB, LANES)
    diag = jnp.stack(
        [diag4[:, 0]] + [pltpu.roll(diag4[:, j].reshape(nch * SUB, LANES), j * SUB, axis=1)
                         .reshape(nch, SUB, LANES) for j in range(1, N_SUB)], axis=1).reshape(r, LANES)

    return dict(q_cat=q_cat, k_cat=k_cat, diag=diag, qb=qb, kb=kb, d_last=d_last, v=v, vb=v.astype(BF16))


def _gated_recurrence_heads(heads, st_ref):
    ops = [_recurrence_operands(*hd) for hd in heads]
    nch = ops[0]["q_cat"].shape[0]
    chunk = lambda c: slice(c * CHUNK, (c + 1) * CHUNK)
    scores = [[_nt_dot(p["q_cat"][c], p["k_cat"][c]) + p["diag"][chunk(c), :CHUNK] for p in ops]
              for c in range(nch)]
    intra = [[jnp.dot(scores[c][i].astype(BF16), p["vb"][chunk(c)], preferred_element_type=F32)
              for i, p in enumerate(ops)] for c in range(nch)]
    upd = [[jnp.dot(p["v"][chunk(c)].T.astype(BF16), p["kb"][chunk(c)], preferred_element_type=F32)
            for p in ops] for c in range(nch)]
    sts = [st_ref[i] for i in range(len(ops))]
    outs = [[] for _ in ops]
    for c in range(nch):
        for i, p in enumerate(ops):
            outs[i].append(intra[c][i] + _nt_dot(p["qb"][chunk(c)], sts[i].astype(BF16)))
            sts[i] = sts[i] * p["d_last"][c] + upd[c][i]
    for i, st in enumerate(sts):
        st_ref[i] = st
    return [jnp.concatenate(o, axis=0) for o in outs]


def _hgrn_kernel(q_ref, f_ref, i_ref, g_ref, lb_ref, ng_ref, o_ref, st_ref, *, layer):
    @pl.when(pl.program_id(2) == 0)
    def _():
        st_ref[...] = jnp.zeros_like(st_ref)

    lb_raw = lb_ref[...]
    e = jnp.exp(lb_raw - jnp.max(lb_raw, axis=0, keepdims=True))
    soft = e / jnp.sum(e, axis=0, keepdims=True)
    lb_all = jnp.sum(soft[:layer + 1], axis=0, keepdims=True) - soft[0:1]

    d = LANES
    head_cols = [slice(hh * d, (hh + 1) * d) for hh in range(st_ref.shape[0])]
    heads = []
    for cols in head_cols:
        lb = lb_all[:, cols]
        q_raw = q_ref[:, cols]
        f = lb + (1.0 - lb) * _sigmoid(f_ref[:, cols])
        heads.append((q_raw * _sigmoid(q_raw), 1.0 - f, i_ref[:, cols], jnp.log(f)))
    for cols, o in zip(head_cols, _gated_recurrence_heads(heads, st_ref)):
        g_raw = g_ref[:, cols]
        o_ref[:, cols] = (_rms(o, ng_ref[:, cols]) * (g_raw * _sigmoid(g_raw))).astype(o_ref.dtype)


def hgrn2_mixer(proj, lb, norm_g, batch, seq, layer, *, rows=512, heads_per_step=4):
    t = proj.shape[0]
    d = LANES
    w = heads_per_step * d
    groups = HGRN_HEADS // heads_per_step
    nblk = seq // rows
    col = lambda base: pl.BlockSpec((rows, w), lambda b, h, i: (b * nblk + i, base + h))
    return pl.pallas_call(
        functools.partial(_hgrn_kernel, layer=layer),
        grid=(batch, groups, nblk),
        in_specs=[col(0), col(groups), col(2 * groups), col(3 * groups),
                  pl.BlockSpec((lb.shape[0], w), lambda b, h, i: (0, h)),
                  pl.BlockSpec((1, w), lambda b, h, i: (0, h))],
        out_specs=pl.BlockSpec((rows, w), lambda b, h, i: (b * nblk + i, h)),
        out_shape=jax.ShapeDtypeStruct((t, HGRN_HEADS * d), BF16),
        scratch_shapes=[pltpu.VMEM((heads_per_step, d, d), F32)],
        compiler_params=_params("parallel", "parallel", "arbitrary"),
    )(proj, proj, proj, proj, lb, norm_g)


def _gla_kernel(q_ref, k_ref, v_ref, r_ref, a_ref, wa_ref, ba_ref, ng_ref, o_ref, st_ref, *, q_scale):
    @pl.when(pl.program_id(2) == 0)
    def _():
        st_ref[...] = jnp.zeros_like(st_ref)

    x = jnp.dot(a_ref[...].astype(BF16), wa_ref[...], preferred_element_type=F32) + ba_ref[...]
    log_sig = jnp.minimum(x, 0.0) - jnp.log(1.0 + jnp.exp(-jnp.abs(x)))
    log_a = log_sig / GLA_GATE_NORMALIZER
    nh, dv, dk = st_ref.shape
    k_cols = [slice(hh * dk, (hh + 1) * dk) for hh in range(nh)]
    v_cols = [slice(hh * dv, (hh + 1) * dv) for hh in range(nh)]
    heads = [(q_ref[:, kc] * q_scale, k_ref[:, kc], v_ref[:, vc], log_a[:, kc]) for kc, vc in zip(k_cols, v_cols)]
    for vc, o in zip(v_cols, _gated_recurrence_heads(heads, st_ref)):
        r_raw = r_ref[:, vc]
        o_ref[:, vc] = (_rms(o, ng_ref[:, vc]) * (r_raw * _sigmoid(r_raw))).astype(o_ref.dtype)


def gla_mixer(proj, a_lr, wa2, ba, norm_g, batch, seq, col0, *, rows=512, heads_per_step=4):
    t = proj.shape[0]
    dk = LANES
    dv = 2 * dk
    wk, wv = heads_per_step * dk, heads_per_step * dv
    groups = GLA_HEADS // heads_per_step
    nblk = seq // rows
    cq, ck = col0 // wk, col0 // wk + groups
    cv = (col0 + 2 * GLA_HEADS * dk) // wv
    cr = cv + groups
    spec = lambda w, base: pl.BlockSpec((rows, w), lambda b, h, i: (b * nblk + i, base + h))
    return pl.pallas_call(
        functools.partial(_gla_kernel, q_scale=dk ** -0.5),
        grid=(batch, groups, nblk),
        in_specs=[spec(wk, cq), spec(wk, ck), spec(wv, cv), spec(wv, cr),
                  pl.BlockSpec((rows, a_lr.shape[1]), lambda b, h, i: (b * nblk + i, 0)),
                  pl.BlockSpec((wa2.shape[0], wk), lambda b, h, i: (0, h)),
                  pl.BlockSpec((1, wk), lambda b, h, i: (0, h)),
                  pl.BlockSpec((1, wv), lambda b, h, i: (0, h))],
        out_specs=pl.BlockSpec((rows, wv), lambda b, h, i: (b * nblk + i, h)),
        out_shape=jax.ShapeDtypeStruct((t, GLA_HEADS * dv), BF16),
        scratch_shapes=[pltpu.VMEM((heads_per_step, dv, dk), F32)],
        compiler_params=_params("parallel", "parallel", "arbitrary"),
    )(proj, proj, proj, proj, a_lr, wa2, ba, norm_g)


def kernel(x, norm_g, even_w_in, even_conv_w, even_rel_bias, even_w_out, odd_w_in, hgrn_lb, hgrn_norm_g,
           gla_wa2, gla_ba, gla_norm_g, odd_w_out, mlp_w1, mlp_w2):
    batch, seq, d = x.shape
    depth = norm_g.shape[0]
    half = d // 2
    even_w_in_b, even_w_out_b = even_w_in.astype(BF16), even_w_out.astype(BF16)
    n_odd_main = odd_w_in.shape[2] - GLA_GATE_RANK
    odd_w_in_b, odd_w_out_b = odd_w_in.astype(BF16), odd_w_out.astype(BF16)
    mlp_w1_b, mlp_w2_b = mlp_w1.astype(BF16), mlp_w2.astype(BF16)
    lane_pad = LANES - GLA_GATE_RANK
    h = x.reshape(batch * seq, d)
    for l in range(depth):
        g = norm_g[l][:, None, :]
        e = l // 2
        if l % 2 == 0:
            conv_in, qkv = rms_matmul(h, g[0], even_w_in_b, e, [(3 * half, F32, 2), (3 * half, BF16, 1)])
            ya = gated_short_conv(conv_in, even_conv_w[e], seq)
            yb = band_attention(qkv, even_rel_bias[e], batch, seq, 0)
            w_out = even_w_out_b
        else:
            w_gate = jnp.pad(odd_w_in[e][:, n_odd_main:], ((0, 0), (0, lane_pad))).astype(BF16)
            proj, a_lr = rms_matmul(h, g[0], odd_w_in_b, e, [(n_odd_main, F32, 4)], w_gate)
            ya = hgrn2_mixer(proj, hgrn_lb, hgrn_norm_g[e][None, :], batch, seq, l)
            wa2 = jnp.pad(gla_wa2[e], ((0, lane_pad), (0, 0))).astype(BF16)
            yb = gla_mixer(proj, a_lr, wa2, gla_ba[e][None, :], gla_norm_g[e][None, :], batch, seq,
                           4 * HGRN_HEADS * LANES)
            w_out = odd_w_out_b
        h = outproj_residual(ya, yb, w_out, e, h, g[1])
        h = mlp_residual(h, g[2], mlp_w1_b, mlp_w2_b, l, g[3])
    return h.reshape(batch, seq, d)
```

```python
import functools

import numpy as np
import jax
import jax.numpy as jnp
from jax import lax
from jax.experimental import pallas as pl
from jax.experimental.pallas import tpu as pltpu

F32 = jnp.float32
BF16 = jnp.bfloat16

EPS = 1e-6
LOG2_E = 1.4426950408889634
CHUNK = 64
SUB = 8
LANES = 128
N_SUB = CHUNK // SUB

CONV_WIDTH = 3
ATT_HEADS = 8
ATT_PAST_CHUNKS = 8
REL_CLIP = 256
HGRN_HEADS = 8
GLA_HEADS = 4
GLA_GATE_RANK = 16
GLA_GATE_NORMALIZER = 16.0

VMEM_LIMIT = 60 * 1024 * 1024


def _params(*sem):
    return pltpu.CompilerParams(dimension_semantics=sem, vmem_limit_bytes=VMEM_LIMIT)


def _rms(x, g):
    ms = jnp.mean(x * x, axis=-1, keepdims=True)
    return x * lax.rsqrt(ms + EPS) * g


def _sigmoid(x):
    return 1.0 / (1.0 + jnp.exp(-x))


def _nt_dot(a, b):
    return lax.dot_general(a, b, (((1,), (1,)), ((), ())), preferred_element_type=F32)


def _rms_matmul_kernel(h_ref, g_ref, w_ref, *refs, col_steps, has_extra):
    u_ref = refs[-1]
    wx_ref, ox_ref = (refs[0], refs[-2]) if has_extra else (None, None)
    o_refs = refs[1:-2] if has_extra else refs[:-1]
    j = pl.program_id(1)

    @pl.when(j == 0)
    def _():
        u = _rms(h_ref[...], g_ref[...]).astype(BF16)
        u_ref[...] = u
        if has_extra:
            ox_ref[...] = jnp.dot(u, wx_ref[...], preferred_element_type=F32)

    for jj, (out_idx, col0, tn) in enumerate(col_steps):
        @pl.when(j == jj)
        def _():
            o_ref = o_refs[out_idx]
            y = jnp.dot(u_ref[...], w_ref[:, col0:col0 + tn], preferred_element_type=F32)
            o_ref[...] = y.astype(o_ref.dtype)


def rms_matmul(h, g, w, layer, outs, w_extra=None, *, tm=512):
    t, d = h.shape
    resident = pl.Buffered(1)
    in_specs = [pl.BlockSpec((tm, d), lambda i, j: (i, 0)),
                pl.BlockSpec((1, d), lambda i, j: (0, 0)),
                pl.BlockSpec((None, d, w.shape[2]), lambda i, j: (layer, 0, 0), pipeline_mode=resident)]
    args = [h, g, w]
    if w_extra is not None:
        in_specs.append(pl.BlockSpec(w_extra.shape, lambda i, j: (0, 0), pipeline_mode=resident))
        args.append(w_extra)
    out_specs, out_shape, col_steps = [], [], []
    col = 0
    for k, (ncols, dtype, nblk) in enumerate(outs):
        tn = ncols // nblk
        j0 = len(col_steps)
        out_specs.append(pl.BlockSpec(
            (tm, tn), lambda i, j, j0=j0, nblk=nblk: (i, jnp.clip(j - j0, 0, nblk - 1))))
        out_shape.append(jax.ShapeDtypeStruct((t, ncols), dtype))
        col_steps += [(k, col + b * tn, tn) for b in range(nblk)]
        col += ncols
    if w_extra is not None:
        nx = w_extra.shape[1]
        out_specs.append(pl.BlockSpec((tm, nx), lambda i, j: (i, 0)))
        out_shape.append(jax.ShapeDtypeStruct((t, nx), F32))
    return pl.pallas_call(
        functools.partial(_rms_matmul_kernel, col_steps=tuple(col_steps), has_extra=w_extra is not None),
        grid=(t // tm, len(col_steps)),
        in_specs=in_specs, out_specs=out_specs, out_shape=out_shape,
        scratch_shapes=[pltpu.VMEM((tm, d), BF16)],
        compiler_params=_params("parallel", "arbitrary"),
    )(*args)


def _conv_kernel(b_ref, c_ref, hc_ref, cprev_ref, hprev_ref, w_ref, o_ref, u_ref, *, blocks_per_seq):
    tm = o_ref.shape[0]
    first = (pl.program_id(0) % blocks_per_seq) == 0
    prev = cprev_ref[...] * hprev_ref[...]
    u_ref[0:SUB, :] = jnp.where(first, 0.0, prev)
    u_ref[SUB:, :] = c_ref[...] * hc_ref[...]
    w = w_ref[...]
    y = w[0:1] * u_ref[pl.ds(SUB - 2, tm), :]
    for j in range(1, CONV_WIDTH):
        y = y + w[j:j + 1] * u_ref[pl.ds(SUB - 2 + j, tm), :]
    o_ref[...] = (b_ref[...] * y).astype(o_ref.dtype)


def gated_short_conv(proj, conv_w, seq, *, tm=512):
    t = proj.shape[0]
    c = conv_w.shape[1]
    rows = tm // SUB
    prev = lambda col: pl.BlockSpec((SUB, c), lambda i: (jnp.maximum(i * rows - 1, 0), col))
    cur = lambda col: pl.BlockSpec((tm, c), lambda i: (i, col))
    return pl.pallas_call(
        functools.partial(_conv_kernel, blocks_per_seq=seq // tm),
        grid=(t // tm,),
        in_specs=[cur(0), cur(1), cur(2), prev(1), prev(2),
                  pl.BlockSpec((CONV_WIDTH, c), lambda i: (0, 0))],
        out_specs=pl.BlockSpec((tm, c), lambda i: (i, 0)),
        out_shape=jax.ShapeDtypeStruct((t, c), BF16),
        scratch_shapes=[pltpu.VMEM((tm + SUB, c), F32)],
        compiler_params=_params("parallel"),
    )(proj, proj, proj, proj, proj, conv_w)


def _band_bias(rel_bias, tq, n_kblk):
    band = (ATT_PAST_CHUNKS + 1) * CHUNK
    heads = rel_bias.shape[0]
    rel_bias = rel_bias.astype(F32) * LOG2_E
    n_far = band - REL_CLIP
    ext = jnp.concatenate([jnp.broadcast_to(rel_bias[:, -1:], (heads, n_far)), rel_bias[:, ::-1][:, 1:]], axis=1)
    rows = jnp.stack([ext[:, CHUNK - 1 - qi:CHUNK - 1 - qi + band] for qi in range(CHUNK)], axis=1)
    tk = n_kblk * tq
    neg = lambda w: jnp.full((heads, CHUNK, w), -1e30, F32)
    blocks = [jnp.concatenate([neg(c * CHUNK), rows, neg(tk - band - c * CHUNK)], axis=2)
              for c in range(tq // CHUNK)]
    base = jnp.concatenate(blocks, axis=1)
    kblk = np.arange(tk)[None, :] // tq
    variants = [jnp.where(kblk + v >= n_kblk - 1, base, -1e30) for v in range(n_kblk)]
    return jnp.stack(variants, axis=0)


def _attn_kernel(q_ref, *refs, n_kblk, heads, scale):
    k_refs, v_refs = refs[:n_kblk], refs[n_kblk:2 * n_kblk]
    bias_ref, o_ref = refs[2 * n_kblk], refs[2 * n_kblk + 1]
    dh = q_ref.shape[1] // heads
    head_cols = [slice(h * dh, (h + 1) * dh) for h in range(heads)]
    scores = []
    for h, cols in enumerate(head_cols):
        q = (q_ref[:, cols].astype(F32) * scale).astype(BF16)
        k = jnp.concatenate([r[:, cols] for r in k_refs], axis=0).astype(BF16)
        scores.append(_nt_dot(q, k) + bias_ref[h])
    for cols, s in zip(head_cols, scores):
        v = jnp.concatenate([r[:, cols] for r in v_refs], axis=0).astype(BF16)
        m = jnp.max(s, axis=-1, keepdims=True)
        p = jnp.exp2(s - m)
        l = jnp.sum(p, axis=-1, keepdims=True)
        o = jnp.dot(p.astype(BF16), v, preferred_element_type=F32)
        o_ref[:, cols] = (o / l).astype(o_ref.dtype)


def band_attention(proj, rel_bias, batch, seq, col0, *, tq=256):
    t = proj.shape[0]
    heads = rel_bias.shape[0]
    dh = LANES
    width = heads * dh
    n_kblk = ATT_PAST_CHUNKS * CHUNK // tq + 1
    nq = seq // tq
    bias = _band_bias(rel_bias, tq, n_kblk)
    cq = col0 // width

    def kv_spec(cblk, j):
        return pl.BlockSpec(
            (tq, width), lambda b, i: (b * nq + jnp.maximum(i - (n_kblk - 1) + j, 0), cblk))

    in_specs = [pl.BlockSpec((tq, width), lambda b, i: (b * nq + i, cq))]
    in_specs += [kv_spec(cq + 1, j) for j in range(n_kblk)]
    in_specs += [kv_spec(cq + 2, j) for j in range(n_kblk)]
    in_specs += [pl.BlockSpec((None, heads, tq, n_kblk * tq),
                              lambda b, i: (jnp.minimum(i, n_kblk - 1), 0, 0, 0))]
    return pl.pallas_call(
        functools.partial(_attn_kernel, n_kblk=n_kblk, heads=heads, scale=dh ** -0.5 * LOG2_E),
        grid=(batch, nq),
        in_specs=in_specs,
        out_specs=pl.BlockSpec((tq, width), lambda b, i: (b * nq + i, 0)),
        out_shape=jax.ShapeDtypeStruct((t, width), BF16),
        compiler_params=_params("parallel", "arbitrary"),
    )(*([proj] * (1 + 2 * n_kblk)), bias)


def _outproj_kernel(ya_ref, yb_ref, w_ref, h_ref, g_ref, o_ref):
    y = jnp.concatenate([ya_ref[...], yb_ref[...]], axis=1)
    y = jnp.dot(y, w_ref[...], preferred_element_type=F32)
    o_ref[...] = h_ref[...] + _rms(y, g_ref[...])


def outproj_residual(ya, yb, w, layer, h, g, *, tm=512):
    t, d = h.shape
    row = lambda a: pl.BlockSpec((tm, a.shape[1]), lambda i: (i, 0))
    return pl.pallas_call(
        _outproj_kernel, grid=(t // tm,),
        in_specs=[row(ya), row(yb), pl.BlockSpec((None,) + w.shape[1:], lambda i: (layer, 0, 0)), row(h),
                  pl.BlockSpec((1, d), lambda i: (0, 0))],
        out_specs=row(h),
        out_shape=jax.ShapeDtypeStruct((t, d), F32),
        compiler_params=_params("parallel"),
    )(ya, yb, w, h, g)


def _mlp_kernel(h_ref, g_in_ref, w1_ref, w2_ref, g_out_ref, o_ref, u_ref, *, parts):
    j = pl.program_id(1)
    last = pl.num_programs(1) - 1
    step = o_ref.shape[0] // parts
    row_parts = [slice(p * step, (p + 1) * step) for p in range(parts)]

    def ff_block(u):
        a = jnp.dot(u, w1_ref[...].astype(BF16), preferred_element_type=F32)
        a = jnp.square(jnp.maximum(a, 0.0)).astype(BF16)
        return jnp.dot(a, w2_ref[...].astype(BF16), preferred_element_type=F32)

    @pl.when(j == 0)
    def _():
        for rows in row_parts:
            u = _rms(h_ref[rows, :], g_in_ref[...]).astype(BF16)
            u_ref[rows, :] = u
            o_ref[rows, :] = ff_block(u)

    @pl.when((j > 0) & (j < last))
    def _():
        o_ref[...] += ff_block(u_ref[...])

    @pl.when(j == last)
    def _():
        for rows in row_parts:
            z = o_ref[rows, :] + ff_block(u_ref[rows, :])
            o_ref[rows, :] = h_ref[rows, :] + _rms(z, g_out_ref[...])


def mlp_residual(h, g_in, w1, w2, layer, g_out, *, tm=1024, tf=512, parts=2):
    t, d = h.shape
    ff = w1.shape[2]
    assert ff // tf >= 2, "first and last ff block are handled by different branches"
    vec = pl.BlockSpec((1, d), lambda i, j: (0, 0))
    return pl.pallas_call(
        functools.partial(_mlp_kernel, parts=parts), grid=(t // tm, ff // tf),
        in_specs=[pl.BlockSpec((tm, d), lambda i, j: (i, 0)), vec,
                  pl.BlockSpec((None, d, tf), lambda i, j: (layer, 0, j)),
                  pl.BlockSpec((None, tf, d), lambda i, j: (layer, j, 0)), vec],
        out_specs=pl.BlockSpec((tm, d), lambda i, j: (i, 0)),
        out_shape=jax.ShapeDtypeStruct((t, d), F32),
        scratch_shapes=[pltpu.VMEM((tm, d), BF16)],
        compiler_params=_params("parallel", "arbitrary"),
    )(h, g_in, w1, w2, g_out)


def _split3_bf16(x):
    hi = x.astype(BF16)
    r = x - hi.astype(F32)
    mid = r.astype(BF16)
    lo = (r - mid.astype(F32)).astype(BF16)
    return hi, mid, lo


def _recurrence_operands(q, k, v, log_a):
    r, dk = q.shape
    nch, nsb = r // CHUNK, r // SUB
    row = lax.broadcasted_iota(jnp.int32, (CHUNK, CHUNK), 0)
    col = lax.broadcasted_iota(jnp.int32, (CHUNK, CHUNK), 1)
    tri = (row >= col).astype(BF16)
    parts = _split3_bf16(log_a * LOG2_E)
    b = jnp.concatenate(
        [sum(jnp.dot(tri, p[c * CHUNK:(c + 1) * CHUNK], preferred_element_type=F32) for p in parts)
         for c in range(nch)], axis=0)

    b3 = b.reshape(nch, CHUNK, dk)
    q3 = q.reshape(nch, CHUNK, dk)
    b_last = b3[:, CHUNK - 1:CHUNK, :]
    qb = (q * jnp.exp2(b)).astype(BF16)
    kb = (k.reshape(nch, CHUNK, dk) * jnp.exp2(b_last - b3)).reshape(r, dk).astype(BF16)
    d_last = jnp.exp2(b_last)

    b4 = b.reshape(nsb, SUB, dk)
    q4 = q.reshape(nsb, SUB, dk)
    k4 = k.reshape(nsb, SUB, dk)

    kt3 = (k4 * jnp.exp2(b4[:, SUB - 1:SUB, :] - b4)).reshape(nch, CHUNK, dk).astype(BF16)
    sub_of_row = lax.broadcasted_iota(jnp.int32, (1, CHUNK, 1), 1) // SUB
    q_slots, k_slots = [], []
    for j in range(N_SUB - 1):
        lo = (j + 1) * SUB
        end_j = b3[:, lo - 1:lo, :]
        qt = (q3[:, lo:, :] * jnp.exp2(b3[:, lo:, :] - end_j)).astype(BF16)
        q_slots.append(jnp.concatenate([jnp.zeros((nch, lo, dk), BF16), qt], axis=1))
        k_slots.append(jnp.where(sub_of_row == j, kt3, jnp.zeros_like(kt3)))
    q_cat = jnp.concatenate(q_slots, axis=2)
    k_cat = jnp.concatenate(k_slots, axis=2)

    t_in = lax.broadcasted_iota(jnp.int32, (1, SUB, 1), 1)
    prods = []
    for s in range(SUB):
        arg = b4 - b4[:, s:s + 1, :]
        if s > 0:
            arg = jnp.where(t_in >= s, arg, -jnp.inf)
        prods.append((q4 * k4[:, s:s + 1, :] * jnp.exp2(arg)).reshape(r, dk).astype(BF16))
    slot = lax.broadcasted_iota(jnp.int32, (SUB * dk, LANES), 0) // dk
    to_lane = (slot == lax.broadcasted_iota(jnp.int32, (SUB * dk, LANES), 1)).astype(BF16)
    diag = jnp.dot(jnp.concatenate(prods, axis=1), to_lane, preferred_element_type=F32)
    diag4 = diag.reshape(nch, N_SUB, SUB, LANES)
    diag = jnp.stack(
        [diag4[:, 0]] + [pltpu.roll(diag4[:, j].reshape(nch * SUB, LANES), j * SUB, axis=1)
                         .reshape(nch, SUB, LANES) for j in range(1, N_SUB)], axis=1).reshape(r, LANES)

    return dict(q_cat=q_cat, k_cat=k_cat, diag=diag, qb=qb, kb=kb, d_last=d_last, v=v, vb=v.astype(BF16))


def _gated_recurrence_heads(heads, st_ref):
    ops = [_recurrence_operands(*hd) for hd in heads]
    nch = ops[0]["q_cat"].shape[0]
    chunk = lambda c: slice(c * CHUNK, (c + 1) * CHUNK)
    scores = [[_nt_dot(p["q_cat"][c], p["k_cat"][c]) + p["diag"][chunk(c), :CHUNK] for p in ops]
              for c in range(nch)]
    intra = [[jnp.dot(scores[c][i].astype(BF16), p["vb"][chunk(c)], preferred_element_type=F32)
              for i, p in enumerate(ops)] for c in range(nch)]
    upd = [[jnp.dot(p["v"][chunk(c)].T.astype(BF16), p["kb"][chunk(c)], preferred_element_type=F32)
            for p in ops] for c in range(nch)]
    sts = [st_ref[i] for i in range(len(ops))]
    outs = [[] for _ in ops]
    for c in range(nch):
        for i, p in enumerate(ops):
            outs[i].append(intra[c][i] + _nt_dot(p["qb"][chunk(c)], sts[i].astype(BF16)))
            sts[i] = sts[i] * p["d_last"][c] + upd[c][i]
    for i, st in enumerate(sts):
        st_ref[i] = st
    return [jnp.concatenate(o, axis=0) for o in outs]


def _hgrn_kernel(q_ref, f_ref, i_ref, g_ref, lb_ref, ng_ref, o_ref, st_ref, *, layer):
    @pl.when(pl.program_id(2) == 0)
    def _():
        st_ref[...] = jnp.zeros_like(st_ref)

    lb_raw = lb_ref[...]
    e = jnp.exp(lb_raw - jnp.max(lb_raw, axis=0, keepdims=True))
    soft = e / jnp.sum(e, axis=0, keepdims=True)
    lb_all = jnp.sum(soft[:layer + 1], axis=0, keepdims=True) - soft[0:1]

    d = LANES
    head_cols = [slice(hh * d, (hh + 1) * d) for hh in range(st_ref.shape[0])]
    heads = []
    for cols in head_cols:
        lb = lb_all[:, cols]
        q_raw = q_ref[:, cols]
        f = lb + (1.0 - lb) * _sigmoid(f_ref[:, cols])
        heads.append((q_raw * _sigmoid(q_raw), 1.0 - f, i_ref[:, cols], jnp.log(f)))
    for cols, o in zip(head_cols, _gated_recurrence_heads(heads, st_ref)):
        g_raw = g_ref[:, cols]
        o_ref[:, cols] = (_rms(o, ng_ref[:, cols]) * (g_raw * _sigmoid(g_raw))).astype(o_ref.dtype)


def hgrn2_mixer(proj, lb, norm_g, batch, seq, layer, *, rows=512, heads_per_step=4):
    t = proj.shape[0]
    d = LANES
    w = heads_per_step * d
    groups = HGRN_HEADS // heads_per_step
    nblk = seq // rows
    col = lambda base: pl.BlockSpec((rows, w), lambda b, h, i: (b * nblk + i, base + h))
    return pl.pallas_call(
        functools.partial(_hgrn_kernel, layer=layer),
        grid=(batch, groups, nblk),
        in_specs=[col(0), col(groups), col(2 * groups), col(3 * groups),
                  pl.BlockSpec((lb.shape[0], w), lambda b, h, i: (0, h)),
                  pl.BlockSpec((1, w), lambda b, h, i: (0, h))],
        out_specs=pl.BlockSpec((rows, w), lambda b, h, i: (b * nblk + i, h)),
        out_shape=jax.ShapeDtypeStruct((t, HGRN_HEADS * d), BF16),
        scratch_shapes=[pltpu.VMEM((heads_per_step, d, d), F32)],
        compiler_params=_params("parallel", "parallel", "arbitrary"),
    )(proj, proj, proj, proj, lb, norm_g)


def _gla_kernel(q_ref, k_ref, v_ref, r_ref, a_ref, wa_ref, ba_ref, ng_ref, o_ref, st_ref, *, q_scale):
    @pl.when(pl.program_id(2) == 0)
    def _():
        st_ref[...] = jnp.zeros_like(st_ref)

    x = jnp.dot(a_ref[...].astype(BF16), wa_ref[...], preferred_element_type=F32) + ba_ref[...]
    log_sig = jnp.minimum(x, 0.0) - jnp.log(1.0 + jnp.exp(-jnp.abs(x)))
    log_a = log_sig / GLA_GATE_NORMALIZER
    nh, dv, dk = st_ref.shape
    k_cols = [slice(hh * dk, (hh + 1) * dk) for hh in range(nh)]
    v_cols = [slice(hh * dv, (hh + 1) * dv) for hh in range(nh)]
    heads = [(q_ref[:, kc] * q_scale, k_ref[:, kc], v_ref[:, vc], log_a[:, kc]) for kc, vc in zip(k_cols, v_cols)]
    for vc, o in zip(v_cols, _gated_recurrence_heads(heads, st_ref)):
        r_raw = r_ref[:, vc]
        o_ref[:, vc] = (_rms(o, ng_ref[:, vc]) * (r_raw * _sigmoid(r_raw))).astype(o_ref.dtype)


def gla_mixer(proj, a_lr, wa2, ba, norm_g, batch, seq, col0, *, rows=512, heads_per_step=4):
    t = proj.shape[0]
    dk = LANES
    dv = 2 * dk
    wk, wv = heads_per_step * dk, heads_per_step * dv
    groups = GLA_HEADS // heads_per_step
    nblk = seq // rows
    cq, ck = col0 // wk, col0 // wk + groups
    cv = (col0 + 2 * GLA_HEADS * dk) // wv
    cr = cv + groups
    spec = lambda w, base: pl.BlockSpec((rows, w), lambda b, h, i: (b * nblk + i, base + h))
    return pl.pallas_call(
        functools.partial(_gla_kernel, q_scale=dk ** -0.5),
        grid=(batch, groups, nblk),
        in_specs=[spec(wk, cq), spec(wk, ck), spec(wv, cv), spec(wv, cr),
                  pl.BlockSpec((rows, a_lr.shape[1]), lambda b, h, i: (b * nblk + i, 0)),
                  pl.BlockSpec((wa2.shape[0], wk), lambda b, h, i: (0, h)),
                  pl.BlockSpec((1, wk), lambda b, h, i: (0, h)),
                  pl.BlockSpec((1, wv), lambda b, h, i: (0, h))],
        out_specs=pl.BlockSpec((rows, wv), lambda b, h, i: (b * nblk + i, h)),
        out_shape=jax.ShapeDtypeStruct((t, GLA_HEADS * dv), BF16),
        scratch_shapes=[pltpu.VMEM((heads_per_step, dv, dk), F32)],
        compiler_params=_params("parallel", "parallel", "arbitrary"),
    )(proj, proj, proj, proj, a_lr, wa2, ba, norm_g)


def kernel(x, norm_g, even_w_in, even_conv_w, even_rel_bias, even_w_out, odd_w_in, hgrn_lb, hgrn_norm_g,
           gla_wa2, gla_ba, gla_norm_g, odd_w_out, mlp_w1, mlp_w2):
    batch, seq, d = x.shape
    depth = norm_g.shape[0]
    half = d // 2
    even_w_in_b, even_w_out_b = even_w_in.astype(BF16), even_w_out.astype(BF16)
    n_odd_main = odd_w_in.shape[2] - GLA_GATE_RANK
    odd_w_in_b, odd_w_out_b = odd_w_in.astype(BF16), odd_w_out.astype(BF16)
    lane_pad = LANES - GLA_GATE_RANK
    h = x.reshape(batch * seq, d)
    for l in range(depth):
        g = norm_g[l][:, None, :]
        e = l // 2
        if l % 2 == 0:
            conv_in, qkv = rms_matmul(h, g[0], even_w_in_b, e, [(3 * half, F32, 2), (3 * half, BF16, 1)])
            ya = gated_short_conv(conv_in, even_conv_w[e], seq)
            yb = band_attention(qkv, even_rel_bias[e], batch, seq, 0)
            w_out = even_w_out_b
        else:
            w_gate = jnp.pad(odd_w_in[e][:, n_odd_main:], ((0, 0), (0, lane_pad))).astype(BF16)
            proj, a_lr = rms_matmul(h, g[0], odd_w_in_b, e, [(n_odd_main, F32, 4)], w_gate)
            ya = hgrn2_mixer(proj, hgrn_lb, hgrn_norm_g[e][None, :], batch, seq, l)
            wa2 = jnp.pad(gla_wa2[e], ((0, lane_pad), (0, 0))).astype(BF16)
            yb = gla_mixer(proj, a_lr, wa2, gla_ba[e][None, :], gla_norm_g[e][None, :], batch, seq,
                           4 * HGRN_HEADS * LANES)
            w_out = odd_w_out_b
        h = outproj_residual(ya, yb, w_out, e, h, g[1])
        h = mlp_residual(h, g[2], mlp_w1, mlp_w2, l, g[3])
    return h.reshape(batch, seq, d)
```

```python
import functools

import numpy as np
import jax
import jax.numpy as jnp
from jax import lax
from jax.experimental import pallas as pl
from jax.experimental.pallas import tpu as pltpu

F32 = jnp.float32
BF16 = jnp.bfloat16

EPS = 1e-6
LOG2_E = 1.4426950408889634
CHUNK = 64
SUB = 8
LANES = 128
N_SUB = CHUNK // SUB

CONV_WIDTH = 3
ATT_HEADS = 8
ATT_PAST_CHUNKS = 8
REL_CLIP = 256
HGRN_HEADS = 8
GLA_HEADS = 4
GLA_GATE_RANK = 16
GLA_GATE_NORMALIZER = 16.0

VMEM_LIMIT = 60 * 1024 * 1024


def _params(*sem):
    return pltpu.CompilerParams(dimension_semantics=sem, vmem_limit_bytes=VMEM_LIMIT)


def _rms(x, g):
    ms = jnp.mean(x * x, axis=-1, keepdims=True)
    return x * lax.rsqrt(ms + EPS) * g


def _sigmoid(x):
    return 1.0 / (1.0 + jnp.exp(-x))


def _nt_dot(a, b):
    return lax.dot_general(a, b, (((1,), (1,)), ((), ())), preferred_element_type=F32)


def _rms_matmul_kernel(h_ref, g_ref, w_ref, *refs, col_steps, has_extra):
    u_ref = refs[-1]
    wx_ref, ox_ref = (refs[0], refs[-2]) if has_extra else (None, None)
    o_refs = refs[1:-2] if has_extra else refs[:-1]
    j = pl.program_id(1)

    @pl.when(j == 0)
    def _():
        u = _rms(h_ref[...], g_ref[...]).astype(BF16)
        u_ref[...] = u
        if has_extra:
            ox_ref[...] = jnp.dot(u, wx_ref[...], preferred_element_type=F32)

    for jj, (out_idx, col0, tn) in enumerate(col_steps):
        @pl.when(j == jj)
        def _():
            o_ref = o_refs[out_idx]
            y = jnp.dot(u_ref[...], w_ref[:, col0:col0 + tn], preferred_element_type=F32)
            o_ref[...] = y.astype(o_ref.dtype)


def rms_matmul(h, g, w, layer, outs, w_extra=None, *, tm=512):
    t, d = h.shape
    resident = pl.Buffered(1)
    in_specs = [pl.BlockSpec((tm, d), lambda i, j: (i, 0)),
                pl.BlockSpec((1, d), lambda i, j: (0, 0)),
                pl.BlockSpec((None, d, w.shape[2]), lambda i, j: (layer, 0, 0), pipeline_mode=resident)]
    args = [h, g, w]
    if w_extra is not None:
        in_specs.append(pl.BlockSpec(w_extra.shape, lambda i, j: (0, 0), pipeline_mode=resident))
        args.append(w_extra)
    out_specs, out_shape, col_steps = [], [], []
    col = 0
    for k, (ncols, dtype, nblk) in enumerate(outs):
        tn = ncols // nblk
        j0 = len(col_steps)
        out_specs.append(pl.BlockSpec(
            (tm, tn), lambda i, j, j0=j0, nblk=nblk: (i, jnp.clip(j - j0, 0, nblk - 1))))
        out_shape.append(jax.ShapeDtypeStruct((t, ncols), dtype))
        col_steps += [(k, col + b * tn, tn) for b in range(nblk)]
        col += ncols
    if w_extra is not None:
        nx = w_extra.shape[1]
        out_specs.append(pl.BlockSpec((tm, nx), lambda i, j: (i, 0)))
        out_shape.append(jax.ShapeDtypeStruct((t, nx), F32))
    return pl.pallas_call(
        functools.partial(_rms_matmul_kernel, col_steps=tuple(col_steps), has_extra=w_extra is not None),
        grid=(t // tm, len(col_steps)),
        in_specs=in_specs, out_specs=out_specs, out_shape=out_shape,
        scratch_shapes=[pltpu.VMEM((tm, d), BF16)],
        compiler_params=_params("parallel", "arbitrary"),
    )(*args)


def _gated_conv_rows(b_ref, c_ref, hc_ref, cprev_ref, hprev_ref, w_ref, u_ref, first):
    tm = b_ref.shape[0]
    prev = cprev_ref[...] * hprev_ref[...]
    u_ref[0:SUB, :] = jnp.where(first, 0.0, prev)
    u_ref[SUB:, :] = c_ref[...] * hc_ref[...]
    w = w_ref[...]
    y = w[0:1] * u_ref[pl.ds(SUB - 2, tm), :]
    for j in range(1, CONV_WIDTH):
        y = y + w[j:j + 1] * u_ref[pl.ds(SUB - 2 + j, tm), :]
    return b_ref[...] * y


def _band_bias(rel_bias, tq, n_kblk):
    band = (ATT_PAST_CHUNKS + 1) * CHUNK
    heads = rel_bias.shape[0]
    rel_bias = rel_bias.astype(F32) * LOG2_E
    n_far = band - REL_CLIP
    ext = jnp.concatenate([jnp.broadcast_to(rel_bias[:, -1:], (heads, n_far)), rel_bias[:, ::-1][:, 1:]], axis=1)
    rows = jnp.stack([ext[:, CHUNK - 1 - qi:CHUNK - 1 - qi + band] for qi in range(CHUNK)], axis=1)
    tk = n_kblk * tq
    neg = lambda w: jnp.full((heads, CHUNK, w), -1e30, F32)
    blocks = [jnp.concatenate([neg(c * CHUNK), rows, neg(tk - band - c * CHUNK)], axis=2)
              for c in range(tq // CHUNK)]
    base = jnp.concatenate(blocks, axis=1)
    kblk = np.arange(tk)[None, :] // tq
    variants = [jnp.where(kblk + v >= n_kblk - 1, base, -1e30) for v in range(n_kblk)]
    return jnp.stack(variants, axis=0)


def _band_scores(q_ref, k_refs, bias_ref, heads, scale):
    dh = q_ref.shape[1] // heads
    scores = []
    for h in range(heads):
        cols = slice(h * dh, (h + 1) * dh)
        q = (q_ref[:, cols].astype(F32) * scale).astype(BF16)
        k = jnp.concatenate([r[:, cols] for r in k_refs], axis=0).astype(BF16)
        scores.append((cols, _nt_dot(q, k) + bias_ref[h]))
    return scores


def _band_softmax_pv(s, v_refs, cols):
    v = jnp.concatenate([r[:, cols] for r in v_refs], axis=0).astype(BF16)
    m = jnp.max(s, axis=-1, keepdims=True)
    p = jnp.exp2(s - m)
    l = jnp.sum(p, axis=-1, keepdims=True)
    return jnp.dot(p.astype(BF16), v, preferred_element_type=F32) / l


def _conv_attention_outproj_kernel(q_ref, *refs, n_kblk, heads, scale, blocks_per_seq):
    k_refs, v_refs = refs[:n_kblk], refs[n_kblk:2 * n_kblk]
    (bias_ref, b_ref, c_ref, hc_ref, cprev_ref, hprev_ref, cw_ref, w_ref, h_ref, g_ref,
     o_ref, y_ref, u_ref) = refs[2 * n_kblk:]
    step = pl.program_id(0)

    @pl.when(step == 0)
    def _():
        y_ref[...] = jnp.zeros_like(y_ref)

    slot = step % 2
    y = jnp.dot(y_ref[1 - slot], w_ref[...], preferred_element_type=F32)
    o_ref[...] = h_ref[...] + _rms(y, g_ref[...])

    block = jnp.minimum(step, pl.num_programs(0) - 2)
    half = b_ref.shape[1]
    ya = _gated_conv_rows(b_ref, c_ref, hc_ref, cprev_ref, hprev_ref, cw_ref, u_ref, block % blocks_per_seq == 0)
    y_ref[slot, :, 0:half] = ya.astype(BF16)
    for cols, s in _band_scores(q_ref, k_refs, bias_ref, heads, scale):
        o = _band_softmax_pv(s, v_refs, cols)
        y_ref[slot, :, half + cols.start:half + cols.stop] = o.astype(BF16)


def conv_attention_outproj(conv_in, qkv, conv_w, rel_bias, w_out, layer, h, g, seq, *, tq=256):
    t, d = h.shape
    half = conv_w.shape[1]
    heads = rel_bias.shape[0]
    n_kblk = ATT_PAST_CHUNKS * CHUNK // tq + 1
    nq = seq // tq
    nblk = t // tq
    bias = _band_bias(rel_bias, tq, n_kblk)
    blk = lambda s: jnp.minimum(s, nblk - 1)
    prev_blk = lambda s: jnp.maximum(s - 1, 0)

    def kv_spec(col, j):
        def index(s):
            i = blk(s) % nq
            return (blk(s) - i + jnp.maximum(i - (n_kblk - 1) + j, 0), col)
        return pl.BlockSpec((tq, half), index)

    cur = lambda col: pl.BlockSpec((tq, half), lambda s: (blk(s), col))
    prev_rows = lambda col: pl.BlockSpec((SUB, half), lambda s: (jnp.maximum(blk(s) * (tq // SUB) - 1, 0), col))
    in_specs = [cur(0)] + [kv_spec(1, j) for j in range(n_kblk)] + [kv_spec(2, j) for j in range(n_kblk)]
    in_specs += [pl.BlockSpec((None, heads, tq, n_kblk * tq),
                              lambda s: (jnp.minimum(blk(s) % nq, n_kblk - 1), 0, 0, 0)),
                 cur(0), cur(1), cur(2), prev_rows(1), prev_rows(2),
                 pl.BlockSpec((CONV_WIDTH, half), lambda s: (0, 0)),
                 pl.BlockSpec((None, d, d), lambda s: (layer, 0, 0), pipeline_mode=pl.Buffered(1)),
                 pl.BlockSpec((tq, d), lambda s: (prev_blk(s), 0)),
                 pl.BlockSpec((1, d), lambda s: (0, 0))]
    return pl.pallas_call(
        functools.partial(_conv_attention_outproj_kernel, n_kblk=n_kblk, heads=heads,
                          scale=(half // heads) ** -0.5 * LOG2_E, blocks_per_seq=nq),
        grid=(nblk + 1,),
        in_specs=in_specs,
        out_specs=pl.BlockSpec((tq, d), lambda s: (prev_blk(s), 0)),
        out_shape=jax.ShapeDtypeStruct((t, d), F32),
        scratch_shapes=[pltpu.VMEM((2, tq, d), BF16), pltpu.VMEM((tq + SUB, half), F32)],
        compiler_params=_params("arbitrary"),
    )(*([qkv] * (1 + 2 * n_kblk)), bias, *([conv_in] * 5), conv_w, w_out, h, g)


def _outproj_kernel(ya_ref, yb_ref, w_ref, h_ref, g_ref, o_ref):
    y = jnp.concatenate([ya_ref[...], yb_ref[...]], axis=1)
    y = jnp.dot(y, w_ref[...], preferred_element_type=F32)
    o_ref[...] = h_ref[...] + _rms(y, g_ref[...])


def outproj_residual(ya, yb, w, layer, h, g, *, tm=512):
    t, d = h.shape
    row = lambda a: pl.BlockSpec((tm, a.shape[1]), lambda i: (i, 0))
    return pl.pallas_call(
        _outproj_kernel, grid=(t // tm,),
        in_specs=[row(ya), row(yb), pl.BlockSpec((None,) + w.shape[1:], lambda i: (layer, 0, 0)), row(h),
                  pl.BlockSpec((1, d), lambda i: (0, 0))],
        out_specs=row(h),
        out_shape=jax.ShapeDtypeStruct((t, d), F32),
        compiler_params=_params("parallel"),
    )(ya, yb, w, h, g)


def _mlp_kernel(h_ref, g_in_ref, w1_ref, w2_ref, g_out_ref, o_ref, u_ref, *, parts):
    j = pl.program_id(1)
    last = pl.num_programs(1) - 1
    step = o_ref.shape[0] // parts
    row_parts = [slice(p * step, (p + 1) * step) for p in range(parts)]

    def ff_block(u):
        a = jnp.dot(u, w1_ref[...].astype(BF16), preferred_element_type=F32)
        a = jnp.square(jnp.maximum(a, 0.0)).astype(BF16)
        return jnp.dot(a, w2_ref[...].astype(BF16), preferred_element_type=F32)

    @pl.when(j == 0)
    def _():
        for rows in row_parts:
            u = _rms(h_ref[rows, :], g_in_ref[...]).astype(BF16)
            u_ref[rows, :] = u
            o_ref[rows, :] = ff_block(u)

    @pl.when((j > 0) & (j < last))
    def _():
        o_ref[...] += ff_block(u_ref[...])

    @pl.when(j == last)
    def _():
        for rows in row_parts:
            z = o_ref[rows, :] + ff_block(u_ref[rows, :])
            o_ref[rows, :] = h_ref[rows, :] + _rms(z, g_out_ref[...])


def mlp_residual(h, g_in, w1, w2, layer, g_out, *, tm=1024, tf=512, parts=2):
    t, d = h.shape
    ff = w1.shape[2]
    assert ff // tf >= 2, "first and last ff block are handled by different branches"
    vec = pl.BlockSpec((1, d), lambda i, j: (0, 0))
    return pl.pallas_call(
        functools.partial(_mlp_kernel, parts=parts), grid=(t // tm, ff // tf),
        in_specs=[pl.BlockSpec((tm, d), lambda i, j: (i, 0)), vec,
                  pl.BlockSpec((None, d, tf), lambda i, j: (layer, 0, j)),
                  pl.BlockSpec((None, tf, d), lambda i, j: (layer, j, 0)), vec],
        out_specs=pl.BlockSpec((tm, d), lambda i, j: (i, 0)),
        out_shape=jax.ShapeDtypeStruct((t, d), F32),
        scratch_shapes=[pltpu.VMEM((tm, d), BF16)],
        compiler_params=_params("parallel", "arbitrary"),
    )(h, g_in, w1, w2, g_out)


def _split3_bf16(x):
    hi = x.astype(BF16)
    r = x - hi.astype(F32)
    mid = r.astype(BF16)
    lo = (r - mid.astype(F32)).astype(BF16)
    return hi, mid, lo


def _recurrence_operands(q, k, v, log_a):
    r, dk = q.shape
    nch, nsb = r // CHUNK, r // SUB
    row = lax.broadcasted_iota(jnp.int32, (CHUNK, CHUNK), 0)
    col = lax.broadcasted_iota(jnp.int32, (CHUNK, CHUNK), 1)
    tri = (row >= col).astype(BF16)
    parts = _split3_bf16(log_a * LOG2_E)
    b = jnp.concatenate(
        [sum(jnp.dot(tri, p[c * CHUNK:(c + 1) * CHUNK], preferred_element_type=F32) for p in parts)
         for c in range(nch)], axis=0)

    b3 = b.reshape(nch, CHUNK, dk)
    q3 = q.reshape(nch, CHUNK, dk)
    b_last = b3[:, CHUNK - 1:CHUNK, :]
    qb = (q * jnp.exp2(b)).astype(BF16)
    kb = (k.reshape(nch, CHUNK, dk) * jnp.exp2(b_last - b3)).reshape(r, dk).astype(BF16)
    d_last = jnp.exp2(b_last)

    b4 = b.reshape(nsb, SUB, dk)
    q4 = q.reshape(nsb, SUB, dk)
    k4 = k.reshape(nsb, SUB, dk)

    kt3 = (k4 * jnp.exp2(b4[:, SUB - 1:SUB, :] - b4)).reshape(nch, CHUNK, dk).astype(BF16)
    sub_of_row = lax.broadcasted_iota(jnp.int32, (1, CHUNK, 1), 1) // SUB
    q_slots, k_slots = [], []
    for j in range(N_SUB - 1):
        lo = (j + 1) * SUB
        end_j = b3[:, lo - 1:lo, :]
        qt = (q3[:, lo:, :] * jnp.exp2(b3[:, lo:, :] - end_j)).astype(BF16)
        q_slots.append(jnp.concatenate([jnp.zeros((nch, lo, dk), BF16), qt], axis=1))
        k_slots.append(jnp.where(sub_of_row == j, kt3, jnp.zeros_like(kt3)))
    q_cat = jnp.concatenate(q_slots, axis=2)
    k_cat = jnp.concatenate(k_slots, axis=2)

    t_in = lax.broadcasted_iota(jnp.int32, (1, SUB, 1), 1)
    prods = []
    for s in range(SUB):
        arg = b4 - b4[:, s:s + 1, :]
        if s > 0:
            arg = jnp.where(t_in >= s, arg, -jnp.inf)
        prods.append((q4 * k4[:, s:s + 1, :] * jnp.exp2(arg)).reshape(r, dk).astype(BF16))
    slot = lax.broadcasted_iota(jnp.int32, (SUB * dk, LANES), 0) // dk
    to_lane = (slot == lax.broadcasted_iota(jnp.int32, (SUB * dk, LANES), 1)).astype(BF16)
    diag = jnp.dot(jnp.concatenate(prods, axis=1), to_lane, preferred_element_type=F32)
    diag4 = diag.reshape(nch, N_SUB, SUB, LANES)
    diag = jnp.stack(
        [diag4[:, 0]] + [pltpu.roll(diag4[:, j].reshape(nch * SUB, LANES), j * SUB, axis=1)
                         .reshape(nch, SUB, LANES) for j in range(1, N_SUB)], axis=1).reshape(r, LANES)

    return dict(q_cat=q_cat, k_cat=k_cat, diag=diag, qb=qb, kb=kb, d_last=d_last, v=v, vb=v.astype(BF16))


def _gated_recurrence_heads(heads, st_ref):
    ops = [_recurrence_operands(*hd) for hd in heads]
    nch = ops[0]["q_cat"].shape[0]
    chunk = lambda c: slice(c * CHUNK, (c + 1) * CHUNK)
    scores = [[_nt_dot(p["q_cat"][c], p["k_cat"][c]) + p["diag"][chunk(c), :CHUNK] for p in ops]
              for c in range(nch)]
    intra = [[jnp.dot(scores[c][i].astype(BF16), p["vb"][chunk(c)], preferred_element_type=F32)
              for i, p in enumerate(ops)] for c in range(nch)]
    upd = [[jnp.dot(p["v"][chunk(c)].T.astype(BF16), p["kb"][chunk(c)], preferred_element_type=F32)
            for p in ops] for c in range(nch)]
    sts = [st_ref[i] for i in range(len(ops))]
    outs = [[] for _ in ops]
    for c in range(nch):
        for i, p in enumerate(ops):
            outs[i].append(intra[c][i] + _nt_dot(p["qb"][chunk(c)], sts[i].astype(BF16)))
            sts[i] = sts[i] * p["d_last"][c] + upd[c][i]
    for i, st in enumerate(sts):
        st_ref[i] = st
    return [jnp.concatenate(o, axis=0) for o in outs]


def _hgrn_kernel(q_ref, f_ref, i_ref, g_ref, lb_ref, ng_ref, o_ref, st_ref, *, layer):
    @pl.when(pl.program_id(2) == 0)
    def _():
        st_ref[...] = jnp.zeros_like(st_ref)

    lb_raw = lb_ref[...]
    e = jnp.exp(lb_raw - jnp.max(lb_raw, axis=0, keepdims=True))
    soft = e / jnp.sum(e, axis=0, keepdims=True)
    lb_all = jnp.sum(soft[:layer + 1], axis=0, keepdims=True) - soft[0:1]

    d = LANES
    head_cols = [slice(hh * d, (hh + 1) * d) for hh in range(st_ref.shape[0])]
    heads = []
    for cols in head_cols:
        lb = lb_all[:, cols]
        q_raw = q_ref[:, cols]
        f = lb + (1.0 - lb) * _sigmoid(f_ref[:, cols])
        heads.append((q_raw * _sigmoid(q_raw), 1.0 - f, i_ref[:, cols], jnp.log(f)))
    for cols, o in zip(head_cols, _gated_recurrence_heads(heads, st_ref)):
        g_raw = g_ref[:, cols]
        o_ref[:, cols] = (_rms(o, ng_ref[:, cols]) * (g_raw * _sigmoid(g_raw))).astype(o_ref.dtype)


def hgrn2_mixer(proj, lb, norm_g, batch, seq, layer, *, rows=512, heads_per_step=4):
    t = proj.shape[0]
    d = LANES
    w = heads_per_step * d
    groups = HGRN_HEADS // heads_per_step
    nblk = seq // rows
    col = lambda base: pl.BlockSpec((rows, w), lambda b, h, i: (b * nblk + i, base + h))
    return pl.pallas_call(
        functools.partial(_hgrn_kernel, layer=layer),
        grid=(batch, groups, nblk),
        in_specs=[col(0), col(groups), col(2 * groups), col(3 * groups),
                  pl.BlockSpec((lb.shape[0], w), lambda b, h, i: (0, h)),
                  pl.BlockSpec((1, w), lambda b, h, i: (0, h))],
        out_specs=pl.BlockSpec((rows, w), lambda b, h, i: (b * nblk + i, h)),
        out_shape=jax.ShapeDtypeStruct((t, HGRN_HEADS * d), BF16),
        scratch_shapes=[pltpu.VMEM((heads_per_step, d, d), F32)],
        compiler_params=_params("parallel", "parallel", "arbitrary"),
    )(proj, proj, proj, proj, lb, norm_g)


def _gla_kernel(q_ref, k_ref, v_ref, r_ref, a_ref, wa_ref, ba_ref, ng_ref, o_ref, st_ref, *, q_scale):
    @pl.when(pl.program_id(2) == 0)
    def _():
        st_ref[...] = jnp.zeros_like(st_ref)

    x = jnp.dot(a_ref[...].astype(BF16), wa_ref[...], preferred_element_type=F32) + ba_ref[...]
    log_sig = jnp.minimum(x, 0.0) - jnp.log(1.0 + jnp.exp(-jnp.abs(x)))
    log_a = log_sig / GLA_GATE_NORMALIZER
    nh, dv, dk = st_ref.shape
    k_cols = [slice(hh * dk, (hh + 1) * dk) for hh in range(nh)]
    v_cols = [slice(hh * dv, (hh + 1) * dv) for hh in range(nh)]
    heads = [(q_ref[:, kc] * q_scale, k_ref[:, kc], v_ref[:, vc], log_a[:, kc]) for kc, vc in zip(k_cols, v_cols)]
    for vc, o in zip(v_cols, _gated_recurrence_heads(heads, st_ref)):
        r_raw = r_ref[:, vc]
        o_ref[:, vc] = (_rms(o, ng_ref[:, vc]) * (r_raw * _sigmoid(r_raw))).astype(o_ref.dtype)


def gla_mixer(proj, a_lr, wa2, ba, norm_g, batch, seq, col0, *, rows=512, heads_per_step=4):
    t = proj.shape[0]
    dk = LANES
    dv = 2 * dk
    wk, wv = heads_per_step * dk, heads_per_step * dv
    groups = GLA_HEADS // heads_per_step
    nblk = seq // rows
    cq, ck = col0 // wk, col0 // wk + groups
    cv = (col0 + 2 * GLA_HEADS * dk) // wv
    cr = cv + groups
    spec = lambda w, base: pl.BlockSpec((rows, w), lambda b, h, i: (b * nblk + i, base + h))
    return pl.pallas_call(
        functools.partial(_gla_kernel, q_scale=dk ** -0.5),
        grid=(batch, groups, nblk),
        in_specs=[spec(wk, cq), spec(wk, ck), spec(wv, cv), spec(wv, cr),
                  pl.BlockSpec((rows, a_lr.shape[1]), lambda b, h, i: (b * nblk + i, 0)),
                  pl.BlockSpec((wa2.shape[0], wk), lambda b, h, i: (0, h)),
                  pl.BlockSpec((1, wk), lambda b, h, i: (0, h)),
                  pl.BlockSpec((1, wv), lambda b, h, i: (0, h))],
        out_specs=pl.BlockSpec((rows, wv), lambda b, h, i: (b * nblk + i, h)),
        out_shape=jax.ShapeDtypeStruct((t, GLA_HEADS * dv), BF16),
        scratch_shapes=[pltpu.VMEM((heads_per_step, dv, dk), F32)],
        compiler_params=_params("parallel", "parallel", "arbitrary"),
    )(proj, proj, proj, proj, a_lr, wa2, ba, norm_g)


def kernel(x, norm_g, even_w_in, even_conv_w, even_rel_bias, even_w_out, odd_w_in, hgrn_lb, hgrn_norm_g,
           gla_wa2, gla_ba, gla_norm_g, odd_w_out, mlp_w1, mlp_w2):
    batch, seq, d = x.shape
    depth = norm_g.shape[0]
    half = d // 2
    even_w_in_b, even_w_out_b = even_w_in.astype(BF16), even_w_out.astype(BF16)
    n_odd_main = odd_w_in.shape[2] - GLA_GATE_RANK
    odd_w_in_b, odd_w_out_b = odd_w_in.astype(BF16), odd_w_out.astype(BF16)
    lane_pad = LANES - GLA_GATE_RANK
    h = x.reshape(batch * seq, d)
    for l in range(depth):
        g = norm_g[l][:, None, :]
        e = l // 2
        if l % 2 == 0:
            conv_in, qkv = rms_matmul(h, g[0], even_w_in_b, e, [(3 * half, F32, 2), (3 * half, BF16, 1)])
            h = conv_attention_outproj(conv_in, qkv, even_conv_w[e], even_rel_bias[e], even_w_out_b, e, h, g[1], seq)
        else:
            w_gate = jnp.pad(odd_w_in[e][:, n_odd_main:], ((0, 0), (0, lane_pad))).astype(BF16)
            proj, a_lr = rms_matmul(h, g[0], odd_w_in_b, e, [(n_odd_main, F32, 4)], w_gate)
            ya = hgrn2_mixer(proj, hgrn_lb, hgrn_norm_g[e][None, :], batch, seq, l)
            wa2 = jnp.pad(gla_wa2[e], ((0, lane_pad), (0, 0))).astype(BF16)
            yb = gla_mixer(proj, a_lr, wa2, gla_ba[e][None, :], gla_norm_g[e][None, :], batch, seq,
                           4 * HGRN_HEADS * LANES)
            h = outproj_residual(ya, yb, odd_w_out_b, e, h, g[1])
        h = mlp_residual(h, g[2], mlp_w1, mlp_w2, l, g[3])
    return h.reshape(batch, seq, d)
```

```python
import functools

import numpy as np
import jax
import jax.numpy as jnp
from jax import lax
from jax.experimental import pallas as pl
from jax.experimental.pallas import tpu as pltpu

F32 = jnp.float32
BF16 = jnp.bfloat16

EPS = 1e-6
LOG2_E = 1.4426950408889634
CHUNK = 64
SUB = 8
LANES = 128
N_SUB = CHUNK // SUB

CONV_WIDTH = 3
ATT_HEADS = 8
ATT_PAST_CHUNKS = 8
REL_CLIP = 256
HGRN_HEADS = 8
GLA_HEADS = 4
GLA_GATE_RANK = 16
GLA_GATE_NORMALIZER = 16.0

VMEM_LIMIT = 60 * 1024 * 1024


def _params(*sem):
    return pltpu.CompilerParams(dimension_semantics=sem, vmem_limit_bytes=VMEM_LIMIT)


def _rms(x, g):
    ms = jnp.mean(x * x, axis=-1, keepdims=True)
    return x * lax.rsqrt(ms + EPS) * g


def _sigmoid(x):
    return 1.0 / (1.0 + jnp.exp(-x))


def _nt_dot(a, b):
    return lax.dot_general(a, b, (((1,), (1,)), ((), ())), preferred_element_type=F32)


def _rms_matmul_kernel(h_ref, g_ref, w_ref, *refs, col_steps, has_extra):
    u_ref = refs[-1]
    wx_ref, ox_ref = (refs[0], refs[-2]) if has_extra else (None, None)
    o_refs = refs[1:-2] if has_extra else refs[:-1]
    j = pl.program_id(1)

    @pl.when(j == 0)
    def _():
        u = _rms(h_ref[...], g_ref[...]).astype(BF16)
        u_ref[...] = u
        if has_extra:
            ox_ref[...] = jnp.dot(u, wx_ref[...], preferred_element_type=F32)

    for jj, (out_idx, col0, tn) in enumerate(col_steps):
        @pl.when(j == jj)
        def _():
            o_ref = o_refs[out_idx]
            y = jnp.dot(u_ref[...], w_ref[:, col0:col0 + tn], preferred_element_type=F32)
            o_ref[...] = y.astype(o_ref.dtype)


def rms_matmul(h, g, w, layer, outs, w_extra=None, *, tm=512):
    t, d = h.shape
    resident = pl.Buffered(1)
    in_specs = [pl.BlockSpec((tm, d), lambda i, j: (i, 0)),
                pl.BlockSpec((1, d), lambda i, j: (0, 0)),
                pl.BlockSpec((None, d, w.shape[2]), lambda i, j: (layer, 0, 0), pipeline_mode=resident)]
    args = [h, g, w]
    if w_extra is not None:
        in_specs.append(pl.BlockSpec(w_extra.shape, lambda i, j: (0, 0), pipeline_mode=resident))
        args.append(w_extra)
    out_specs, out_shape, col_steps = [], [], []
    col = 0
    for k, (ncols, dtype, nblk) in enumerate(outs):
        tn = ncols // nblk
        j0 = len(col_steps)
        out_specs.append(pl.BlockSpec(
            (tm, tn), lambda i, j, j0=j0, nblk=nblk: (i, jnp.clip(j - j0, 0, nblk - 1))))
        out_shape.append(jax.ShapeDtypeStruct((t, ncols), dtype))
        col_steps += [(k, col + b * tn, tn) for b in range(nblk)]
        col += ncols
    if w_extra is not None:
        nx = w_extra.shape[1]
        out_specs.append(pl.BlockSpec((tm, nx), lambda i, j: (i, 0)))
        out_shape.append(jax.ShapeDtypeStruct((t, nx), F32))
    return pl.pallas_call(
        functools.partial(_rms_matmul_kernel, col_steps=tuple(col_steps), has_extra=w_extra is not None),
        grid=(t // tm, len(col_steps)),
        in_specs=in_specs, out_specs=out_specs, out_shape=out_shape,
        scratch_shapes=[pltpu.VMEM((tm, d), BF16)],
        compiler_params=_params("parallel", "arbitrary"),
    )(*args)


def _gated_conv_rows(b_ref, c_ref, hc_ref, cprev_ref, hprev_ref, w_ref, u_ref, first):
    tm = b_ref.shape[0]
    prev = cprev_ref[...] * hprev_ref[...]
    u_ref[0:SUB, :] = jnp.where(first, 0.0, prev)
    u_ref[SUB:, :] = c_ref[...] * hc_ref[...]
    w = w_ref[...]
    y = w[0:1] * u_ref[pl.ds(SUB - 2, tm), :]
    for j in range(1, CONV_WIDTH):
        y = y + w[j:j + 1] * u_ref[pl.ds(SUB - 2 + j, tm), :]
    return b_ref[...] * y


def _band_bias(rel_bias, tq, n_kblk):
    band = (ATT_PAST_CHUNKS + 1) * CHUNK
    heads = rel_bias.shape[0]
    rel_bias = rel_bias.astype(F32) * LOG2_E
    n_far = band - REL_CLIP
    ext = jnp.concatenate([jnp.broadcast_to(rel_bias[:, -1:], (heads, n_far)), rel_bias[:, ::-1][:, 1:]], axis=1)
    rows = jnp.stack([ext[:, CHUNK - 1 - qi:CHUNK - 1 - qi + band] for qi in range(CHUNK)], axis=1)
    tk = n_kblk * tq
    neg = lambda w: jnp.full((heads, CHUNK, w), -1e30, F32)
    blocks = [jnp.concatenate([neg(c * CHUNK), rows, neg(tk - band - c * CHUNK)], axis=2)
              for c in range(tq // CHUNK)]
    base = jnp.concatenate(blocks, axis=1)
    kblk = np.arange(tk)[None, :] // tq
    variants = [jnp.where(kblk + v >= n_kblk - 1, base, -1e30) for v in range(n_kblk)]
    return jnp.stack(variants, axis=0)


def _band_scores(q_ref, k_refs, bias_ref, heads, scale):
    dh = q_ref.shape[1] // heads
    scores = []
    for h in range(heads):
        cols = slice(h * dh, (h + 1) * dh)
        q = (q_ref[:, cols].astype(F32) * scale).astype(BF16)
        k = jnp.concatenate([r[:, cols] for r in k_refs], axis=0).astype(BF16)
        scores.append((cols, _nt_dot(q, k) + bias_ref[h]))
    return scores


def _band_softmax_pv(s, v_refs, cols):
    v = jnp.concatenate([r[:, cols] for r in v_refs], axis=0).astype(BF16)
    m = jnp.max(s, axis=-1, keepdims=True)
    p = jnp.exp2(s - m)
    l = jnp.sum(p, axis=-1, keepdims=True)
    return jnp.dot(p.astype(BF16), v, preferred_element_type=F32) / l


def _conv_attention_outproj_kernel(q_ref, *refs, n_kblk, heads, scale, blocks_per_seq):
    k_refs, v_refs = refs[:n_kblk], refs[n_kblk:2 * n_kblk]
    (bias_ref, b_ref, c_ref, hc_ref, cprev_ref, hprev_ref, cw_ref, w_ref, h_ref, g_ref,
     o_ref, y_ref, u_ref) = refs[2 * n_kblk:]
    step = pl.program_id(0)

    @pl.when(step == 0)
    def _():
        y_ref[...] = jnp.zeros_like(y_ref)

    slot = step % 2
    y = jnp.dot(y_ref[1 - slot], w_ref[...], preferred_element_type=F32)
    o_ref[...] = h_ref[...] + _rms(y, g_ref[...])

    block = jnp.minimum(step, pl.num_programs(0) - 2)
    half = b_ref.shape[1]
    ya = _gated_conv_rows(b_ref, c_ref, hc_ref, cprev_ref, hprev_ref, cw_ref, u_ref, block % blocks_per_seq == 0)
    y_ref[slot, :, 0:half] = ya.astype(BF16)
    for cols, s in _band_scores(q_ref, k_refs, bias_ref, heads, scale):
        o = _band_softmax_pv(s, v_refs, cols)
        y_ref[slot, :, half + cols.start:half + cols.stop] = o.astype(BF16)


def conv_attention_outproj(conv_in, qkv, conv_w, rel_bias, w_out, layer, h, g, seq, *, tq=256):
    t, d = h.shape
    half = conv_w.shape[1]
    heads = rel_bias.shape[0]
    n_kblk = ATT_PAST_CHUNKS * CHUNK // tq + 1
    nq = seq // tq
    nblk = t // tq
    bias = _band_bias(rel_bias, tq, n_kblk)
    blk = lambda s: jnp.minimum(s, nblk - 1)
    prev_blk = lambda s: jnp.maximum(s - 1, 0)

    def kv_spec(col, j):
        def index(s):
            i = blk(s) % nq
            return (blk(s) - i + jnp.maximum(i - (n_kblk - 1) + j, 0), col)
        return pl.BlockSpec((tq, half), index)

    cur = lambda col: pl.BlockSpec((tq, half), lambda s: (blk(s), col))
    prev_rows = lambda col: pl.BlockSpec((SUB, half), lambda s: (jnp.maximum(blk(s) * (tq // SUB) - 1, 0), col))
    in_specs = [cur(0)] + [kv_spec(1, j) for j in range(n_kblk)] + [kv_spec(2, j) for j in range(n_kblk)]
    in_specs += [pl.BlockSpec((None, heads, tq, n_kblk * tq),
                              lambda s: (jnp.minimum(blk(s) % nq, n_kblk - 1), 0, 0, 0)),
                 cur(0), cur(1), cur(2), prev_rows(1), prev_rows(2),
                 pl.BlockSpec((CONV_WIDTH, half), lambda s: (0, 0)),
                 pl.BlockSpec((None, d, d), lambda s: (layer, 0, 0), pipeline_mode=pl.Buffered(1)),
                 pl.BlockSpec((tq, d), lambda s: (prev_blk(s), 0)),
                 pl.BlockSpec((1, d), lambda s: (0, 0))]
    return pl.pallas_call(
        functools.partial(_conv_attention_outproj_kernel, n_kblk=n_kblk, heads=heads,
                          scale=(half // heads) ** -0.5 * LOG2_E, blocks_per_seq=nq),
        grid=(nblk + 1,),
        in_specs=in_specs,
        out_specs=pl.BlockSpec((tq, d), lambda s: (prev_blk(s), 0)),
        out_shape=jax.ShapeDtypeStruct((t, d), F32),
        scratch_shapes=[pltpu.VMEM((2, tq, d), BF16), pltpu.VMEM((tq + SUB, half), F32)],
        compiler_params=_params("arbitrary"),
    )(*([qkv] * (1 + 2 * n_kblk)), bias, *([conv_in] * 5), conv_w, w_out, h, g)


def _outproj_kernel(ya_ref, yb_ref, w_ref, h_ref, g_ref, o_ref):
    y = jnp.concatenate([ya_ref[...], yb_ref[...]], axis=1)
    y = jnp.dot(y, w_ref[...], preferred_element_type=F32)
    o_ref[...] = h_ref[...] + _rms(y, g_ref[...])


def outproj_residual(ya, yb, w, layer, h, g, *, tm=512):
    t, d = h.shape
    row = lambda a: pl.BlockSpec((tm, a.shape[1]), lambda i: (i, 0))
    return pl.pallas_call(
        _outproj_kernel, grid=(t // tm,),
        in_specs=[row(ya), row(yb), pl.BlockSpec((None,) + w.shape[1:], lambda i: (layer, 0, 0)), row(h),
                  pl.BlockSpec((1, d), lambda i: (0, 0))],
        out_specs=row(h),
        out_shape=jax.ShapeDtypeStruct((t, d), F32),
        compiler_params=_params("parallel"),
    )(ya, yb, w, h, g)


def _mlp_kernel(h_ref, g_in_ref, w1_ref, w2_ref, g_out_ref, o_ref, u_ref, *, parts):
    j = pl.program_id(1)
    last = pl.num_programs(1) - 1
    step = o_ref.shape[0] // parts
    row_parts = [slice(p * step, (p + 1) * step) for p in range(parts)]

    def ff_block(u):
        a = jnp.dot(u, w1_ref[...].astype(BF16), preferred_element_type=F32)
        a = jnp.square(jnp.maximum(a, 0.0)).astype(BF16)
        return jnp.dot(a, w2_ref[...].astype(BF16), preferred_element_type=F32)

    @pl.when(j == 0)
    def _():
        for rows in row_parts:
            u = _rms(h_ref[rows, :], g_in_ref[...]).astype(BF16)
            u_ref[rows, :] = u
            o_ref[rows, :] = ff_block(u)

    @pl.when((j > 0) & (j < last))
    def _():
        o_ref[...] += ff_block(u_ref[...])

    @pl.when(j == last)
    def _():
        for rows in row_parts:
            z = o_ref[rows, :] + ff_block(u_ref[rows, :])
            o_ref[rows, :] = h_ref[rows, :] + _rms(z, g_out_ref[...])


def mlp_residual(h, g_in, w1, w2, layer, g_out, *, tm=1024, tf=512, parts=2):
    t, d = h.shape
    ff = w1.shape[2]
    assert ff // tf >= 2, "first and last ff block are handled by different branches"
    vec = pl.BlockSpec((1, d), lambda i, j: (0, 0))
    return pl.pallas_call(
        functools.partial(_mlp_kernel, parts=parts), grid=(t // tm, ff // tf),
        in_specs=[pl.BlockSpec((tm, d), lambda i, j: (i, 0)), vec,
                  pl.BlockSpec((None, d, tf), lambda i, j: (layer, 0, j)),
                  pl.BlockSpec((None, tf, d), lambda i, j: (layer, j, 0)), vec],
        out_specs=pl.BlockSpec((tm, d), lambda i, j: (i, 0)),
        out_shape=jax.ShapeDtypeStruct((t, d), F32),
        scratch_shapes=[pltpu.VMEM((tm, d), BF16)],
        compiler_params=_params("parallel", "arbitrary"),
    )(h, g_in, w1, w2, g_out)


def _split3_bf16(x):
    hi = x.astype(BF16)
    r = x - hi.astype(F32)
    mid = r.astype(BF16)
    lo = (r - mid.astype(F32)).astype(BF16)
    return hi, mid, lo


def _recurrence_operands(q, k, v, log_a):
    r, dk = q.shape
    nch, nsb = r // CHUNK, r // SUB
    row = lax.broadcasted_iota(jnp.int32, (CHUNK, CHUNK), 0)
    col = lax.broadcasted_iota(jnp.int32, (CHUNK, CHUNK), 1)
    tri = (row >= col).astype(BF16)
    log2_a = log_a * LOG2_E
    parts = _split3_bf16(log2_a)
    b = jnp.concatenate(
        [sum(jnp.dot(tri, p[c * CHUNK:(c + 1) * CHUNK], preferred_element_type=F32) for p in parts)
         for c in range(nch)], axis=0)

    b3 = b.reshape(nch, CHUNK, dk)
    q3 = q.reshape(nch, CHUNK, dk)
    b_last = b3[:, CHUNK - 1:CHUNK, :]
    qb = (q * jnp.exp2(b)).astype(BF16)
    kb = (k.reshape(nch, CHUNK, dk) * jnp.exp2(b_last - b3)).reshape(r, dk).astype(BF16)
    d_last = jnp.exp2(b_last)

    b4 = b.reshape(nsb, SUB, dk)
    q4 = q.reshape(nsb, SUB, dk)
    k4 = k.reshape(nsb, SUB, dk)

    kt3 = (k4 * jnp.exp2(b4[:, SUB - 1:SUB, :] - b4)).reshape(nch, CHUNK, dk).astype(BF16)
    sub_of_row = lax.broadcasted_iota(jnp.int32, (1, CHUNK, 1), 1) // SUB
    q_slots, k_slots = [], []
    for j in range(N_SUB - 1):
        lo = (j + 1) * SUB
        end_j = b3[:, lo - 1:lo, :]
        qt = (q3[:, lo:, :] * jnp.exp2(b3[:, lo:, :] - end_j)).astype(BF16)
        q_slots.append(jnp.concatenate([jnp.zeros((nch, lo, dk), BF16), qt], axis=1))
        k_slots.append(jnp.where(sub_of_row == j, kt3, jnp.zeros_like(kt3)))
    q_cat = jnp.concatenate(q_slots, axis=2)
    k_cat = jnp.concatenate(k_slots, axis=2)

    c4 = jnp.exp2(log2_a.reshape(nsb, SUB, dk))
    prods = [(q4 * k4).reshape(r, dk).astype(BF16)]
    decay = None
    for delta in range(1, SUB):
        c_rot = c4 if delta == 1 else pltpu.roll(c4, delta - 1, axis=1)
        decay = c_rot if decay is None else decay * c_rot
        prods.append((q4 * pltpu.roll(k4, delta, axis=1) * decay).reshape(r, dk).astype(BF16))
    slot = lax.broadcasted_iota(jnp.int32, (SUB * dk, LANES), 0) // dk
    lane_w = lax.broadcasted_iota(jnp.int32, (SUB * dk, LANES), 1)
    to_lane = (lane_w == (LANES - slot) % LANES).astype(BF16)
    diag = jnp.dot(jnp.concatenate(prods, axis=1), to_lane, preferred_element_type=F32)
    lane = lax.broadcasted_iota(jnp.int32, (r, LANES), 1)
    t_sub = lax.broadcasted_iota(jnp.int32, (r, LANES), 0) % SUB
    diag = jnp.where((lane == 0) | (lane + t_sub >= LANES), diag, 0.0)
    diag = jnp.concatenate(
        [pltpu.roll(diag[c * CHUNK:(c + 1) * CHUNK], 0, axis=1, stride=1, stride_axis=0) for c in range(nch)], axis=0)

    return dict(q_cat=q_cat, k_cat=k_cat, diag=diag, qb=qb, kb=kb, d_last=d_last, v=v, vb=v.astype(BF16))


def _gated_recurrence_heads(heads, st_ref):
    ops = [_recurrence_operands(*hd) for hd in heads]
    nch = ops[0]["q_cat"].shape[0]
    chunk = lambda c: slice(c * CHUNK, (c + 1) * CHUNK)
    scores = [[_nt_dot(p["q_cat"][c], p["k_cat"][c]) + p["diag"][chunk(c), :CHUNK] for p in ops]
              for c in range(nch)]
    intra = [[jnp.dot(scores[c][i].astype(BF16), p["vb"][chunk(c)], preferred_element_type=F32)
              for i, p in enumerate(ops)] for c in range(nch)]
    upd = [[jnp.dot(p["v"][chunk(c)].T.astype(BF16), p["kb"][chunk(c)], preferred_element_type=F32)
            for p in ops] for c in range(nch)]
    sts = [st_ref[i] for i in range(len(ops))]
    outs = [[] for _ in ops]
    for c in range(nch):
        for i, p in enumerate(ops):
            outs[i].append(intra[c][i] + _nt_dot(p["qb"][chunk(c)], sts[i].astype(BF16)))
            sts[i] = sts[i] * p["d_last"][c] + upd[c][i]
    for i, st in enumerate(sts):
        st_ref[i] = st
    return [jnp.concatenate(o, axis=0) for o in outs]


def _hgrn_kernel(q_ref, f_ref, i_ref, g_ref, lb_ref, ng_ref, o_ref, st_ref, *, layer):
    @pl.when(pl.program_id(2) == 0)
    def _():
        st_ref[...] = jnp.zeros_like(st_ref)

    lb_raw = lb_ref[...]
    e = jnp.exp(lb_raw - jnp.max(lb_raw, axis=0, keepdims=True))
    soft = e / jnp.sum(e, axis=0, keepdims=True)
    lb_all = jnp.sum(soft[:layer + 1], axis=0, keepdims=True) - soft[0:1]

    d = LANES
    head_cols = [slice(hh * d, (hh + 1) * d) for hh in range(st_ref.shape[0])]
    heads = []
    for cols in head_cols:
        lb = lb_all[:, cols]
        q_raw = q_ref[:, cols]
        f = lb + (1.0 - lb) * _sigmoid(f_ref[:, cols])
        heads.append((q_raw * _sigmoid(q_raw), 1.0 - f, i_ref[:, cols], jnp.log(f)))
    for cols, o in zip(head_cols, _gated_recurrence_heads(heads, st_ref)):
        g_raw = g_ref[:, cols]
        o_ref[:, cols] = (_rms(o, ng_ref[:, cols]) * (g_raw * _sigmoid(g_raw))).astype(o_ref.dtype)


def hgrn2_mixer(proj, lb, norm_g, batch, seq, layer, *, rows=512, heads_per_step=4):
    t = proj.shape[0]
    d = LANES
    w = heads_per_step * d
    groups = HGRN_HEADS // heads_per_step
    nblk = seq // rows
    col = lambda base: pl.BlockSpec((rows, w), lambda b, h, i: (b * nblk + i, base + h))
    return pl.pallas_call(
        functools.partial(_hgrn_kernel, layer=layer),
        grid=(batch, groups, nblk),
        in_specs=[col(0), col(groups), col(2 * groups), col(3 * groups),
                  pl.BlockSpec((lb.shape[0], w), lambda b, h, i: (0, h)),
                  pl.BlockSpec((1, w), lambda b, h, i: (0, h))],
        out_specs=pl.BlockSpec((rows, w), lambda b, h, i: (b * nblk + i, h)),
        out_shape=jax.ShapeDtypeStruct((t, HGRN_HEADS * d), BF16),
        scratch_shapes=[pltpu.VMEM((heads_per_step, d, d), F32)],
        compiler_params=_params("parallel", "parallel", "arbitrary"),
    )(proj, proj, proj, proj, lb, norm_g)


def _gla_kernel(q_ref, k_ref, v_ref, r_ref, a_ref, wa_ref, ba_ref, ng_ref, o_ref, st_ref, *, q_scale):
    @pl.when(pl.program_id(2) == 0)
    def _():
        st_ref[...] = jnp.zeros_like(st_ref)

    x = jnp.dot(a_ref[...].astype(BF16), wa_ref[...], preferred_element_type=F32) + ba_ref[...]
    log_sig = jnp.minimum(x, 0.0) - jnp.log(1.0 + jnp.exp(-jnp.abs(x)))
    log_a = log_sig / GLA_GATE_NORMALIZER
    nh, dv, dk = st_ref.shape
    k_cols = [slice(hh * dk, (hh + 1) * dk) for hh in range(nh)]
    v_cols = [slice(hh * dv, (hh + 1) * dv) for hh in range(nh)]
    heads = [(q_ref[:, kc] * q_scale, k_ref[:, kc], v_ref[:, vc], log_a[:, kc]) for kc, vc in zip(k_cols, v_cols)]
    for vc, o in zip(v_cols, _gated_recurrence_heads(heads, st_ref)):
        r_raw = r_ref[:, vc]
        o_ref[:, vc] = (_rms(o, ng_ref[:, vc]) * (r_raw * _sigmoid(r_raw))).astype(o_ref.dtype)


def gla_mixer(proj, a_lr, wa2, ba, norm_g, batch, seq, col0, *, rows=512, heads_per_step=4):
    t = proj.shape[0]
    dk = LANES
    dv = 2 * dk
    wk, wv = heads_per_step * dk, heads_per_step * dv
    groups = GLA_HEADS // heads_per_step
    nblk = seq // rows
    cq, ck = col0 // wk, col0 // wk + groups
    cv = (col0 + 2 * GLA_HEADS * dk) // wv
    cr = cv + groups
    spec = lambda w, base: pl.BlockSpec((rows, w), lambda b, h, i: (b * nblk + i, base + h))
    return pl.pallas_call(
        functools.partial(_gla_kernel, q_scale=dk ** -0.5),
        grid=(batch, groups, nblk),
        in_specs=[spec(wk, cq), spec(wk, ck), spec(wv, cv), spec(wv, cr),
                  pl.BlockSpec((rows, a_lr.shape[1]), lambda b, h, i: (b * nblk + i, 0)),
                  pl.BlockSpec((wa2.shape[0], wk), lambda b, h, i: (0, h)),
                  pl.BlockSpec((1, wk), lambda b, h, i: (0, h)),
                  pl.BlockSpec((1, wv), lambda b, h, i: (0, h))],
        out_specs=pl.BlockSpec((rows, wv), lambda b, h, i: (b * nblk + i, h)),
        out_shape=jax.ShapeDtypeStruct((t, GLA_HEADS * dv), BF16),
        scratch_shapes=[pltpu.VMEM((heads_per_step, dv, dk), F32)],
        compiler_params=_params("parallel", "parallel", "arbitrary"),
    )(proj, proj, proj, proj, a_lr, wa2, ba, norm_g)


def kernel(x, norm_g, even_w_in, even_conv_w, even_rel_bias, even_w_out, odd_w_in, hgrn_lb, hgrn_norm_g,
           gla_wa2, gla_ba, gla_norm_g, odd_w_out, mlp_w1, mlp_w2):
    batch, seq, d = x.shape
    depth = norm_g.shape[0]
    half = d // 2
    even_w_in_b, even_w_out_b = even_w_in.astype(BF16), even_w_out.astype(BF16)
    n_odd_main = odd_w_in.shape[2] - GLA_GATE_RANK
    odd_w_in_b, odd_w_out_b = odd_w_in.astype(BF16), odd_w_out.astype(BF16)
    lane_pad = LANES - GLA_GATE_RANK
    h = x.reshape(batch * seq, d)
    for l in range(depth):
        g = norm_g[l][:, None, :]
        e = l // 2
        if l % 2 == 0:
            conv_in, qkv = rms_matmul(h, g[0], even_w_in_b, e, [(3 * half, F32, 2), (3 * half, BF16, 1)])
            h = conv_attention_outproj(conv_in, qkv, even_conv_w[e], even_rel_bias[e], even_w_out_b, e, h, g[1], seq)
        else:
            w_gate = jnp.pad(odd_w_in[e][:, n_odd_main:], ((0, 0), (0, lane_pad))).astype(BF16)
            proj, a_lr = rms_matmul(h, g[0], odd_w_in_b, e, [(n_odd_main, F32, 4)], w_gate)
            ya = hgrn2_mixer(proj, hgrn_lb, hgrn_norm_g[e][None, :], batch, seq, l)
            wa2 = jnp.pad(gla_wa2[e], ((0, lane_pad), (0, 0))).astype(BF16)
            yb = gla_mixer(proj, a_lr, wa2, gla_ba[e][None, :], gla_norm_g[e][None, :], batch, seq,
                           4 * HGRN_HEADS * LANES)
            h = outproj_residual(ya, yb, odd_w_out_b, e, h, g[1])
        h = mlp_residual(h, g[2], mlp_w1, mlp_w2, l, g[3])
    return h.reshape(batch, seq, d)
```

```python
import functools

import numpy as np
import jax
import jax.numpy as jnp
from jax import lax
from jax.experimental import pallas as pl
from jax.experimental.pallas import tpu as pltpu

F32 = jnp.float32
BF16 = jnp.bfloat16

EPS = 1e-6
LOG2_E = 1.4426950408889634
CHUNK = 64
SUB = 8
LANES = 128
N_SUB = CHUNK // SUB

CONV_WIDTH = 3
ATT_HEADS = 8
ATT_PAST_CHUNKS = 8
REL_CLIP = 256
HGRN_HEADS = 8
GLA_HEADS = 4
GLA_GATE_RANK = 16
GLA_GATE_NORMALIZER = 16.0

VMEM_LIMIT = 60 * 1024 * 1024


def _params(*sem):
    return pltpu.CompilerParams(dimension_semantics=sem, vmem_limit_bytes=VMEM_LIMIT)


def _rms(x, g):
    ms = jnp.mean(x * x, axis=-1, keepdims=True)
    return x * lax.rsqrt(ms + EPS) * g


def _sigmoid(x):
    return 1.0 / (1.0 + jnp.exp(-x))


def _nt_dot(a, b):
    return lax.dot_general(a, b, (((1,), (1,)), ((), ())), preferred_element_type=F32)


def _rms_matmul_kernel(h_ref, g_ref, w_ref, *refs, col_steps, has_extra):
    u_ref = refs[-1]
    wx_ref, ox_ref = (refs[0], refs[-2]) if has_extra else (None, None)
    o_refs = refs[1:-2] if has_extra else refs[:-1]
    j = pl.program_id(1)

    @pl.when(j == 0)
    def _():
        u = _rms(h_ref[...], g_ref[...]).astype(BF16)
        u_ref[...] = u
        if has_extra:
            ox_ref[...] = jnp.dot(u, wx_ref[...], preferred_element_type=F32)

    for jj, (out_idx, col0, tn) in enumerate(col_steps):
        @pl.when(j == jj)
        def _():
            o_ref = o_refs[out_idx]
            y = jnp.dot(u_ref[...], w_ref[:, col0:col0 + tn], preferred_element_type=F32)
            o_ref[...] = y.astype(o_ref.dtype)


def rms_matmul(h, g, w, layer, outs, w_extra=None, *, tm=512):
    t, d = h.shape
    resident = pl.Buffered(1)
    in_specs = [pl.BlockSpec((tm, d), lambda i, j: (i, 0)),
                pl.BlockSpec((1, d), lambda i, j: (0, 0)),
                pl.BlockSpec((None, d, w.shape[2]), lambda i, j: (layer, 0, 0), pipeline_mode=resident)]
    args = [h, g, w]
    if w_extra is not None:
        in_specs.append(pl.BlockSpec(w_extra.shape, lambda i, j: (0, 0), pipeline_mode=resident))
        args.append(w_extra)
    out_specs, out_shape, col_steps = [], [], []
    col = 0
    for k, (ncols, dtype, nblk) in enumerate(outs):
        tn = ncols // nblk
        j0 = len(col_steps)
        out_specs.append(pl.BlockSpec(
            (tm, tn), lambda i, j, j0=j0, nblk=nblk: (i, jnp.clip(j - j0, 0, nblk - 1))))
        out_shape.append(jax.ShapeDtypeStruct((t, ncols), dtype))
        col_steps += [(k, col + b * tn, tn) for b in range(nblk)]
        col += ncols
    if w_extra is not None:
        nx = w_extra.shape[1]
        out_specs.append(pl.BlockSpec((tm, nx), lambda i, j: (i, 0)))
        out_shape.append(jax.ShapeDtypeStruct((t, nx), F32))
    return pl.pallas_call(
        functools.partial(_rms_matmul_kernel, col_steps=tuple(col_steps), has_extra=w_extra is not None),
        grid=(t // tm, len(col_steps)),
        in_specs=in_specs, out_specs=out_specs, out_shape=out_shape,
        scratch_shapes=[pltpu.VMEM((tm, d), BF16)],
        compiler_params=_params("parallel", "arbitrary"),
    )(*args)


def _gated_conv_rows(b_ref, c_ref, hc_ref, cprev_ref, hprev_ref, w_ref, u_ref, first):
    tm = b_ref.shape[0]
    prev = cprev_ref[...] * hprev_ref[...]
    u_ref[0:SUB, :] = jnp.where(first, 0.0, prev)
    u_ref[SUB:, :] = c_ref[...] * hc_ref[...]
    w = w_ref[...]
    y = w[0:1] * u_ref[pl.ds(SUB - 2, tm), :]
    for j in range(1, CONV_WIDTH):
        y = y + w[j:j + 1] * u_ref[pl.ds(SUB - 2 + j, tm), :]
    return b_ref[...] * y


def _band_bias(rel_bias, tq, n_kblk):
    band = (ATT_PAST_CHUNKS + 1) * CHUNK
    heads = rel_bias.shape[0]
    rel_bias = rel_bias.astype(F32) * LOG2_E
    n_far = band - REL_CLIP
    ext = jnp.concatenate([jnp.broadcast_to(rel_bias[:, -1:], (heads, n_far)), rel_bias[:, ::-1][:, 1:]], axis=1)
    rows = jnp.stack([ext[:, CHUNK - 1 - qi:CHUNK - 1 - qi + band] for qi in range(CHUNK)], axis=1)
    tk = n_kblk * tq
    neg = lambda w: jnp.full((heads, CHUNK, w), -1e30, F32)
    blocks = [jnp.concatenate([neg(c * CHUNK), rows, neg(tk - band - c * CHUNK)], axis=2)
              for c in range(tq // CHUNK)]
    base = jnp.concatenate(blocks, axis=1)
    kblk = np.arange(tk)[None, :] // tq
    variants = [jnp.where(kblk + v >= n_kblk - 1, base, -1e30) for v in range(n_kblk)]
    return jnp.stack(variants, axis=0)


def _band_scores(q_ref, k_refs, bias_ref, heads, scale):
    dh = q_ref.shape[1] // heads
    scores = []
    for h in range(heads):
        cols = slice(h * dh, (h + 1) * dh)
        q = (q_ref[:, cols].astype(F32) * scale).astype(BF16)
        k = jnp.concatenate([r[:, cols] for r in k_refs], axis=0).astype(BF16)
        scores.append((cols, _nt_dot(q, k) + bias_ref[h]))
    return scores


def _band_softmax_pv(s, v_refs, cols):
    v = jnp.concatenate([r[:, cols] for r in v_refs], axis=0).astype(BF16)
    dh = v.shape[1]
    v_ones = jnp.concatenate([v, jnp.ones_like(v)], axis=1)
    m = jnp.max(s, axis=-1, keepdims=True)
    p = jnp.exp2(s - m).astype(BF16)
    o_l = jnp.dot(p, v_ones, preferred_element_type=F32)
    return o_l[:, :dh] / o_l[:, dh:dh + 1]


def _conv_attention_outproj_kernel(q_ref, *refs, n_kblk, heads, scale, blocks_per_seq):
    k_refs, v_refs = refs[:n_kblk], refs[n_kblk:2 * n_kblk]
    (bias_ref, b_ref, c_ref, hc_ref, cprev_ref, hprev_ref, cw_ref, w_ref, h_ref, g_ref,
     o_ref, y_ref, u_ref) = refs[2 * n_kblk:]
    step = pl.program_id(0)

    @pl.when(step == 0)
    def _():
        y_ref[...] = jnp.zeros_like(y_ref)

    slot = step % 2
    y = jnp.dot(y_ref[1 - slot], w_ref[...], preferred_element_type=F32)
    o_ref[...] = h_ref[...] + _rms(y, g_ref[...])

    block = jnp.minimum(step, pl.num_programs(0) - 2)
    half = b_ref.shape[1]
    ya = _gated_conv_rows(b_ref, c_ref, hc_ref, cprev_ref, hprev_ref, cw_ref, u_ref, block % blocks_per_seq == 0)
    y_ref[slot, :, 0:half] = ya.astype(BF16)
    for cols, s in _band_scores(q_ref, k_refs, bias_ref, heads, scale):
        o = _band_softmax_pv(s, v_refs, cols)
        y_ref[slot, :, half + cols.start:half + cols.stop] = o.astype(BF16)


def conv_attention_outproj(conv_in, qkv, conv_w, rel_bias, w_out, layer, h, g, seq, *, tq=256):
    t, d = h.shape
    half = conv_w.shape[1]
    heads = rel_bias.shape[0]
    n_kblk = ATT_PAST_CHUNKS * CHUNK // tq + 1
    nq = seq // tq
    nblk = t // tq
    bias = _band_bias(rel_bias, tq, n_kblk)
    blk = lambda s: jnp.minimum(s, nblk - 1)
    prev_blk = lambda s: jnp.maximum(s - 1, 0)

    def kv_spec(col, j):
        def index(s):
            i = blk(s) % nq
            return (blk(s) - i + jnp.maximum(i - (n_kblk - 1) + j, 0), col)
        return pl.BlockSpec((tq, half), index)

    cur = lambda col: pl.BlockSpec((tq, half), lambda s: (blk(s), col))
    prev_rows = lambda col: pl.BlockSpec((SUB, half), lambda s: (jnp.maximum(blk(s) * (tq // SUB) - 1, 0), col))
    in_specs = [cur(0)] + [kv_spec(1, j) for j in range(n_kblk)] + [kv_spec(2, j) for j in range(n_kblk)]
    in_specs += [pl.BlockSpec((None, heads, tq, n_kblk * tq),
                              lambda s: (jnp.minimum(blk(s) % nq, n_kblk - 1), 0, 0, 0)),
                 cur(0), cur(1), cur(2), prev_rows(1), prev_rows(2),
                 pl.BlockSpec((CONV_WIDTH, half), lambda s: (0, 0)),
                 pl.BlockSpec((None, d, d), lambda s: (layer, 0, 0), pipeline_mode=pl.Buffered(1)),
                 pl.BlockSpec((tq, d), lambda s: (prev_blk(s), 0)),
                 pl.BlockSpec((1, d), lambda s: (0, 0))]
    return pl.pallas_call(
        functools.partial(_conv_attention_outproj_kernel, n_kblk=n_kblk, heads=heads,
                          scale=(half // heads) ** -0.5 * LOG2_E, blocks_per_seq=nq),
        grid=(nblk + 1,),
        in_specs=in_specs,
        out_specs=pl.BlockSpec((tq, d), lambda s: (prev_blk(s), 0)),
        out_shape=jax.ShapeDtypeStruct((t, d), F32),
        scratch_shapes=[pltpu.VMEM((2, tq, d), BF16), pltpu.VMEM((tq + SUB, half), F32)],
        compiler_params=_params("arbitrary"),
    )(*([qkv] * (1 + 2 * n_kblk)), bias, *([conv_in] * 5), conv_w, w_out, h, g)


def _outproj_kernel(ya_ref, yb_ref, w_ref, h_ref, g_ref, o_ref):
    y = jnp.concatenate([ya_ref[...], yb_ref[...]], axis=1)
    y = jnp.dot(y, w_ref[...], preferred_element_type=F32)
    o_ref[...] = h_ref[...] + _rms(y, g_ref[...])


def outproj_residual(ya, yb, w, layer, h, g, *, tm=512):
    t, d = h.shape
    row = lambda a: pl.BlockSpec((tm, a.shape[1]), lambda i: (i, 0))
    return pl.pallas_call(
        _outproj_kernel, grid=(t // tm,),
        in_specs=[row(ya), row(yb), pl.BlockSpec((None,) + w.shape[1:], lambda i: (layer, 0, 0)), row(h),
                  pl.BlockSpec((1, d), lambda i: (0, 0))],
        out_specs=row(h),
        out_shape=jax.ShapeDtypeStruct((t, d), F32),
        compiler_params=_params("parallel"),
    )(ya, yb, w, h, g)


def _mlp_kernel(h_ref, g_in_ref, w1_ref, w2_ref, g_out_ref, o_ref, u_ref, *, parts):
    j = pl.program_id(1)
    last = pl.num_programs(1) - 1
    step = o_ref.shape[0] // parts
    row_parts = [slice(p * step, (p + 1) * step) for p in range(parts)]

    def ff_block(u):
        a = jnp.dot(u, w1_ref[...].astype(BF16), preferred_element_type=F32)
        a = jnp.square(jnp.maximum(a, 0.0)).astype(BF16)
        return jnp.dot(a, w2_ref[...].astype(BF16), preferred_element_type=F32)

    @pl.when(j == 0)
    def _():
        for rows in row_parts:
            u = _rms(h_ref[rows, :], g_in_ref[...]).astype(BF16)
            u_ref[rows, :] = u
            o_ref[rows, :] = ff_block(u)

    @pl.when((j > 0) & (j < last))
    def _():
        o_ref[...] += ff_block(u_ref[...])

    @pl.when(j == last)
    def _():
        for rows in row_parts:
            z = o_ref[rows, :] + ff_block(u_ref[rows, :])
            o_ref[rows, :] = h_ref[rows, :] + _rms(z, g_out_ref[...])


def mlp_residual(h, g_in, w1, w2, layer, g_out, *, tm=1024, tf=512, parts=2):
    t, d = h.shape
    ff = w1.shape[2]
    assert ff // tf >= 2, "first and last ff block are handled by different branches"
    vec = pl.BlockSpec((1, d), lambda i, j: (0, 0))
    return pl.pallas_call(
        functools.partial(_mlp_kernel, parts=parts), grid=(t // tm, ff // tf),
        in_specs=[pl.BlockSpec((tm, d), lambda i, j: (i, 0)), vec,
                  pl.BlockSpec((None, d, tf), lambda i, j: (layer, 0, j)),
                  pl.BlockSpec((None, tf, d), lambda i, j: (layer, j, 0)), vec],
        out_specs=pl.BlockSpec((tm, d), lambda i, j: (i, 0)),
        out_shape=jax.ShapeDtypeStruct((t, d), F32),
        scratch_shapes=[pltpu.VMEM((tm, d), BF16)],
        compiler_params=_params("parallel", "arbitrary"),
    )(h, g_in, w1, w2, g_out)


def _split3_bf16(x):
    hi = x.astype(BF16)
    r = x - hi.astype(F32)
    mid = r.astype(BF16)
    lo = (r - mid.astype(F32)).astype(BF16)
    return hi, mid, lo


def _recurrence_operands(q, k, v, log_a):
    r, dk = q.shape
    nch, nsb = r // CHUNK, r // SUB
    row = lax.broadcasted_iota(jnp.int32, (CHUNK, CHUNK), 0)
    col = lax.broadcasted_iota(jnp.int32, (CHUNK, CHUNK), 1)
    tri = (row >= col).astype(BF16)
    log2_a = log_a * LOG2_E
    parts = _split3_bf16(log2_a)
    b = jnp.concatenate(
        [sum(jnp.dot(tri, p[c * CHUNK:(c + 1) * CHUNK], preferred_element_type=F32) for p in parts)
         for c in range(nch)], axis=0)

    b3 = b.reshape(nch, CHUNK, dk)
    q3 = q.reshape(nch, CHUNK, dk)
    b_last = b3[:, CHUNK - 1:CHUNK, :]
    qb = (q * jnp.exp2(b)).astype(BF16)
    kb = (k.reshape(nch, CHUNK, dk) * jnp.exp2(b_last - b3)).reshape(r, dk).astype(BF16)
    d_last = jnp.exp2(b_last)

    b4 = b.reshape(nsb, SUB, dk)
    q4 = q.reshape(nsb, SUB, dk)
    k4 = k.reshape(nsb, SUB, dk)

    kt3 = (k4 * jnp.exp2(b4[:, SUB - 1:SUB, :] - b4)).reshape(nch, CHUNK, dk).astype(BF16)
    sub_of_row = lax.broadcasted_iota(jnp.int32, (1, CHUNK, 1), 1) // SUB
    q_slots, k_slots = [], []
    for j in range(N_SUB - 1):
        lo = (j + 1) * SUB
        end_j = b3[:, lo - 1:lo, :]
        qt = (q3[:, lo:, :] * jnp.exp2(b3[:, lo:, :] - end_j)).astype(BF16)
        q_slots.append(jnp.concatenate([jnp.zeros((nch, lo, dk), BF16), qt], axis=1))
        k_slots.append(jnp.where(sub_of_row == j, kt3, jnp.zeros_like(kt3)))
    q_cat = jnp.concatenate(q_slots, axis=2)
    k_cat = jnp.concatenate(k_slots, axis=2)

    c4 = jnp.exp2(log2_a.reshape(nsb, SUB, dk))
    kd = k4
    prods = [(q4 * k4).reshape(r, dk).astype(BF16)]
    for _ in range(1, SUB):
        kd = c4 * pltpu.roll(kd, 1, axis=1)
        prods.append((q4 * kd).reshape(r, dk).astype(BF16))
    slot = lax.broadcasted_iota(jnp.int32, (SUB * dk, LANES), 0) // dk
    lane_w = lax.broadcasted_iota(jnp.int32, (SUB * dk, LANES), 1)
    to_lane = (lane_w == (LANES - slot) % LANES).astype(BF16)
    diag = jnp.dot(jnp.concatenate(prods, axis=1), to_lane, preferred_element_type=F32)
    lane = lax.broadcasted_iota(jnp.int32, (r, LANES), 1)
    t_sub = lax.broadcasted_iota(jnp.int32, (r, LANES), 0) % SUB
    diag = jnp.where((lane == 0) | (lane + t_sub >= LANES), diag, 0.0)
    diag = jnp.concatenate(
        [pltpu.roll(diag[c * CHUNK:(c + 1) * CHUNK], 0, axis=1, stride=1, stride_axis=0) for c in range(nch)], axis=0)

    return dict(q_cat=q_cat, k_cat=k_cat, diag=diag, qb=qb, kb=kb, d_last=d_last, v=v, vb=v.astype(BF16))


def _gated_recurrence_heads(heads, st_ref):
    ops = [_recurrence_operands(*hd) for hd in heads]
    nch = ops[0]["q_cat"].shape[0]
    chunk = lambda c: slice(c * CHUNK, (c + 1) * CHUNK)
    scores = [[_nt_dot(p["q_cat"][c], p["k_cat"][c]) + p["diag"][chunk(c), :CHUNK] for p in ops]
              for c in range(nch)]
    intra = [[jnp.dot(scores[c][i].astype(BF16), p["vb"][chunk(c)], preferred_element_type=F32)
              for i, p in enumerate(ops)] for c in range(nch)]
    upd = [[jnp.dot(p["v"][chunk(c)].T.astype(BF16), p["kb"][chunk(c)], preferred_element_type=F32)
            for p in ops] for c in range(nch)]
    sts = [st_ref[i] for i in range(len(ops))]
    outs = [[] for _ in ops]
    for c in range(nch):
        for i, p in enumerate(ops):
            outs[i].append(intra[c][i] + _nt_dot(p["qb"][chunk(c)], sts[i].astype(BF16)))
            sts[i] = sts[i] * p["d_last"][c] + upd[c][i]
    for i, st in enumerate(sts):
        st_ref[i] = st
    return [jnp.concatenate(o, axis=0) for o in outs]


def _hgrn_kernel(q_ref, f_ref, i_ref, g_ref, lb_ref, ng_ref, o_ref, st_ref, *, layer):
    @pl.when(pl.program_id(2) == 0)
    def _():
        st_ref[...] = jnp.zeros_like(st_ref)

    lb_raw = lb_ref[...]
    e = jnp.exp(lb_raw - jnp.max(lb_raw, axis=0, keepdims=True))
    soft = e / jnp.sum(e, axis=0, keepdims=True)
    lb_all = jnp.sum(soft[:layer + 1], axis=0, keepdims=True) - soft[0:1]

    d = LANES
    head_cols = [slice(hh * d, (hh + 1) * d) for hh in range(st_ref.shape[0])]
    heads = []
    for cols in head_cols:
        lb = lb_all[:, cols]
        q_raw = q_ref[:, cols]
        f = lb + (1.0 - lb) * _sigmoid(f_ref[:, cols])
        heads.append((q_raw * _sigmoid(q_raw), 1.0 - f, i_ref[:, cols], jnp.log(f)))
    for cols, o in zip(head_cols, _gated_recurrence_heads(heads, st_ref)):
        g_raw = g_ref[:, cols]
        o_ref[:, cols] = (_rms(o, ng_ref[:, cols]) * (g_raw * _sigmoid(g_raw))).astype(o_ref.dtype)


def hgrn2_mixer(proj, lb, norm_g, batch, seq, layer, *, rows=512, heads_per_step=4):
    t = proj.shape[0]
    d = LANES
    w = heads_per_step * d
    groups = HGRN_HEADS // heads_per_step
    nblk = seq // rows
    col = lambda base: pl.BlockSpec((rows, w), lambda b, h, i: (b * nblk + i, base + h))
    return pl.pallas_call(
        functools.partial(_hgrn_kernel, layer=layer),
        grid=(batch, groups, nblk),
        in_specs=[col(0), col(groups), col(2 * groups), col(3 * groups),
                  pl.BlockSpec((lb.shape[0], w), lambda b, h, i: (0, h)),
                  pl.BlockSpec((1, w), lambda b, h, i: (0, h))],
        out_specs=pl.BlockSpec((rows, w), lambda b, h, i: (b * nblk + i, h)),
        out_shape=jax.ShapeDtypeStruct((t, HGRN_HEADS * d), BF16),
        scratch_shapes=[pltpu.VMEM((heads_per_step, d, d), F32)],
        compiler_params=_params("parallel", "parallel", "arbitrary"),
    )(proj, proj, proj, proj, lb, norm_g)


def _gla_kernel(q_ref, k_ref, v_ref, r_ref, a_ref, wa_ref, ba_ref, ng_ref, o_ref, st_ref, *, q_scale):
    @pl.when(pl.program_id(2) == 0)
    def _():
        st_ref[...] = jnp.zeros_like(st_ref)

    x = jnp.dot(a_ref[...].astype(BF16), wa_ref[...], preferred_element_type=F32) + ba_ref[...]
    log_sig = jnp.minimum(x, 0.0) - jnp.log(1.0 + jnp.exp(-jnp.abs(x)))
    log_a = log_sig / GLA_GATE_NORMALIZER
    nh, dv, dk = st_ref.shape
    k_cols = [slice(hh * dk, (hh + 1) * dk) for hh in range(nh)]
    v_cols = [slice(hh * dv, (hh + 1) * dv) for hh in range(nh)]
    heads = [(q_ref[:, kc] * q_scale, k_ref[:, kc], v_ref[:, vc], log_a[:, kc]) for kc, vc in zip(k_cols, v_cols)]
    for vc, o in zip(v_cols, _gated_recurrence_heads(heads, st_ref)):
        r_raw = r_ref[:, vc]
        o_ref[:, vc] = (_rms(o, ng_ref[:, vc]) * (r_raw * _sigmoid(r_raw))).astype(o_ref.dtype)


def gla_mixer(proj, a_lr, wa2, ba, norm_g, batch, seq, col0, *, rows=512, heads_per_step=4):
    t = proj.shape[0]
    dk = LANES
    dv = 2 * dk
    wk, wv = heads_per_step * dk, heads_per_step * dv
    groups = GLA_HEADS // heads_per_step
    nblk = seq // rows
    cq, ck = col0 // wk, col0 // wk + groups
    cv = (col0 + 2 * GLA_HEADS * dk) // wv
    cr = cv + groups
    spec = lambda w, base: pl.BlockSpec((rows, w), lambda b, h, i: (b * nblk + i, base + h))
    return pl.pallas_call(
        functools.partial(_gla_kernel, q_scale=dk ** -0.5),
        grid=(batch, groups, nblk),
        in_specs=[spec(wk, cq), spec(wk, ck), spec(wv, cv), spec(wv, cr),
                  pl.BlockSpec((rows, a_lr.shape[1]), lambda b, h, i: (b * nblk + i, 0)),
                  pl.BlockSpec((wa2.shape[0], wk), lambda b, h, i: (0, h)),
                  pl.BlockSpec((1, wk), lambda b, h, i: (0, h)),
                  pl.BlockSpec((1, wv), lambda b, h, i: (0, h))],
        out_specs=pl.BlockSpec((rows, wv), lambda b, h, i: (b * nblk + i, h)),
        out_shape=jax.ShapeDtypeStruct((t, GLA_HEADS * dv), BF16),
        scratch_shapes=[pltpu.VMEM((heads_per_step, dv, dk), F32)],
        compiler_params=_params("parallel", "parallel", "arbitrary"),
    )(proj, proj, proj, proj, a_lr, wa2, ba, norm_g)


def kernel(x, norm_g, even_w_in, even_conv_w, even_rel_bias, even_w_out, odd_w_in, hgrn_lb, hgrn_norm_g,
           gla_wa2, gla_ba, gla_norm_g, odd_w_out, mlp_w1, mlp_w2):
    batch, seq, d = x.shape
    depth = norm_g.shape[0]
    half = d // 2
    even_w_in_b, even_w_out_b = even_w_in.astype(BF16), even_w_out.astype(BF16)
    n_odd_main = odd_w_in.shape[2] - GLA_GATE_RANK
    odd_w_in_b, odd_w_out_b = odd_w_in.astype(BF16), odd_w_out.astype(BF16)
    lane_pad = LANES - GLA_GATE_RANK
    h = x.reshape(batch * seq, d)
    for l in range(depth):
        g = norm_g[l][:, None, :]
        e = l // 2
        if l % 2 == 0:
            conv_in, qkv = rms_matmul(h, g[0], even_w_in_b, e, [(3 * half, F32, 2), (3 * half, BF16, 1)])
            h = conv_attention_outproj(conv_in, qkv, even_conv_w[e], even_rel_bias[e], even_w_out_b, e, h, g[1], seq)
        else:
            w_gate = jnp.pad(odd_w_in[e][:, n_odd_main:], ((0, 0), (0, lane_pad))).astype(BF16)
            proj, a_lr = rms_matmul(h, g[0], odd_w_in_b, e, [(n_odd_main, F32, 4)], w_gate)
            ya = hgrn2_mixer(proj, hgrn_lb, hgrn_norm_g[e][None, :], batch, seq, l)
            wa2 = jnp.pad(gla_wa2[e], ((0, lane_pad), (0, 0))).astype(BF16)
            yb = gla_mixer(proj, a_lr, wa2, gla_ba[e][None, :], gla_norm_g[e][None, :], batch, seq,
                           4 * HGRN_HEADS * LANES)
            h = outproj_residual(ya, yb, odd_w_out_b, e, h, g[1])
        h = mlp_residual(h, g[2], mlp_w1, mlp_w2, l, g[3])
    return h.reshape(batch, seq, d)
```

```python
import functools

import numpy as np
import jax
import jax.numpy as jnp
from jax import lax
from jax.experimental import pallas as pl
from jax.experimental.pallas import tpu as pltpu

F32 = jnp.float32
BF16 = jnp.bfloat16

EPS = 1e-6
LOG2_E = 1.4426950408889634
CHUNK = 64
SUB = 8
LANES = 128
N_SUB = CHUNK // SUB

CONV_WIDTH = 3
ATT_HEADS = 8
ATT_PAST_CHUNKS = 8
REL_CLIP = 256
HGRN_HEADS = 8
GLA_HEADS = 4
GLA_GATE_RANK = 16
GLA_GATE_NORMALIZER = 16.0

VMEM_LIMIT = 60 * 1024 * 1024


def _params(*sem):
    return pltpu.CompilerParams(dimension_semantics=sem, vmem_limit_bytes=VMEM_LIMIT)


def _rms(x, g):
    ms = jnp.mean(x * x, axis=-1, keepdims=True)
    return x * lax.rsqrt(ms + EPS) * g


def _sigmoid(x):
    return 1.0 / (1.0 + jnp.exp(-x))


def _nt_dot(a, b):
    return lax.dot_general(a, b, (((1,), (1,)), ((), ())), preferred_element_type=F32)


def _rms_matmul_kernel(h_ref, g_ref, w_ref, *refs, col_steps, has_extra):
    u_ref = refs[-1]
    wx_ref, ox_ref = (refs[0], refs[-2]) if has_extra else (None, None)
    o_refs = refs[1:-2] if has_extra else refs[:-1]
    j = pl.program_id(1)

    @pl.when(j == 0)
    def _():
        u = _rms(h_ref[...], g_ref[...]).astype(BF16)
        u_ref[...] = u
        if has_extra:
            ox_ref[...] = jnp.dot(u, wx_ref[...], preferred_element_type=F32)

    for jj, (out_idx, col0, tn) in enumerate(col_steps):
        @pl.when(j == jj)
        def _():
            o_ref = o_refs[out_idx]
            y = jnp.dot(u_ref[...], w_ref[:, col0:col0 + tn], preferred_element_type=F32)
            o_ref[...] = y.astype(o_ref.dtype)


def rms_matmul(h, g, w, layer, outs, w_extra=None, *, tm=512):
    t, d = h.shape
    resident = pl.Buffered(1)
    in_specs = [pl.BlockSpec((tm, d), lambda i, j: (i, 0)),
                pl.BlockSpec((1, d), lambda i, j: (0, 0)),
                pl.BlockSpec((None, d, w.shape[2]), lambda i, j: (layer, 0, 0), pipeline_mode=resident)]
    args = [h, g, w]
    if w_extra is not None:
        in_specs.append(pl.BlockSpec(w_extra.shape, lambda i, j: (0, 0), pipeline_mode=resident))
        args.append(w_extra)
    out_specs, out_shape, col_steps = [], [], []
    col = 0
    for k, (ncols, dtype, nblk) in enumerate(outs):
        tn = ncols // nblk
        j0 = len(col_steps)
        out_specs.append(pl.BlockSpec(
            (tm, tn), lambda i, j, j0=j0, nblk=nblk: (i, jnp.clip(j - j0, 0, nblk - 1))))
        out_shape.append(jax.ShapeDtypeStruct((t, ncols), dtype))
        col_steps += [(k, col + b * tn, tn) for b in range(nblk)]
        col += ncols
    if w_extra is not None:
        nx = w_extra.shape[1]
        out_specs.append(pl.BlockSpec((tm, nx), lambda i, j: (i, 0)))
        out_shape.append(jax.ShapeDtypeStruct((t, nx), F32))
    return pl.pallas_call(
        functools.partial(_rms_matmul_kernel, col_steps=tuple(col_steps), has_extra=w_extra is not None),
        grid=(t // tm, len(col_steps)),
        in_specs=in_specs, out_specs=out_specs, out_shape=out_shape,
        scratch_shapes=[pltpu.VMEM((tm, d), BF16)],
        compiler_params=_params("parallel", "arbitrary"),
    )(*args)


def _gated_conv_rows(b_ref, c_ref, hc_ref, cprev_ref, hprev_ref, w_ref, u_ref, first):
    tm = b_ref.shape[0]
    prev = cprev_ref[...] * hprev_ref[...]
    u_ref[0:SUB, :] = jnp.where(first, 0.0, prev)
    u_ref[SUB:, :] = c_ref[...] * hc_ref[...]
    w = w_ref[...]
    y = w[0:1] * u_ref[pl.ds(SUB - 2, tm), :]
    for j in range(1, CONV_WIDTH):
        y = y + w[j:j + 1] * u_ref[pl.ds(SUB - 2 + j, tm), :]
    return b_ref[...] * y


def _band_bias(rel_bias, tq, n_kblk):
    band = (ATT_PAST_CHUNKS + 1) * CHUNK
    heads = rel_bias.shape[0]
    rel_bias = rel_bias.astype(F32) * LOG2_E
    n_far = band - REL_CLIP
    ext = jnp.concatenate([jnp.broadcast_to(rel_bias[:, -1:], (heads, n_far)), rel_bias[:, ::-1][:, 1:]], axis=1)
    rows = jnp.stack([ext[:, CHUNK - 1 - qi:CHUNK - 1 - qi + band] for qi in range(CHUNK)], axis=1)
    tk = n_kblk * tq
    neg = lambda w: jnp.full((heads, CHUNK, w), -1e30, F32)
    blocks = [jnp.concatenate([neg(c * CHUNK), rows, neg(tk - band - c * CHUNK)], axis=2)
              for c in range(tq // CHUNK)]
    base = jnp.concatenate(blocks, axis=1)
    kblk = np.arange(tk)[None, :] // tq
    variants = [jnp.where(kblk + v >= n_kblk - 1, base, -1e30) for v in range(n_kblk)]
    return jnp.stack(variants, axis=0)


def _band_scores(q_ref, k_refs, bias_ref, heads, scale):
    dh = q_ref.shape[1] // heads
    scores = []
    for h in range(heads):
        cols = slice(h * dh, (h + 1) * dh)
        q = (q_ref[:, cols].astype(F32) * scale).astype(BF16)
        k = jnp.concatenate([r[:, cols] for r in k_refs], axis=0).astype(BF16)
        scores.append((cols, _nt_dot(q, k) + bias_ref[h]))
    return scores


def _band_softmax_pv(s, v_refs, cols):
    v = jnp.concatenate([r[:, cols] for r in v_refs], axis=0).astype(BF16)
    dh = v.shape[1]
    v_ones = jnp.concatenate([v, jnp.ones_like(v)], axis=1)
    m = jnp.max(s, axis=-1, keepdims=True)
    p = jnp.exp2(s - m).astype(BF16)
    o_l = jnp.dot(p, v_ones, preferred_element_type=F32)
    return o_l[:, :dh] / o_l[:, dh:dh + 1]


def _conv_attention_outproj_kernel(q_ref, *refs, n_kblk, heads, scale, blocks_per_seq):
    k_refs, v_refs = refs[:n_kblk], refs[n_kblk:2 * n_kblk]
    (bias_ref, b_ref, c_ref, hc_ref, cprev_ref, hprev_ref, cw_ref, w_ref, h_ref, g_ref,
     o_ref, y_ref, u_ref) = refs[2 * n_kblk:]
    step = pl.program_id(0)

    @pl.when(step == 0)
    def _():
        y_ref[...] = jnp.zeros_like(y_ref)

    slot = step % 2
    y = jnp.dot(y_ref[1 - slot], w_ref[...], preferred_element_type=F32)
    o_ref[...] = h_ref[...] + _rms(y, g_ref[...])

    block = jnp.minimum(step, pl.num_programs(0) - 2)
    half = b_ref.shape[1]
    ya = _gated_conv_rows(b_ref, c_ref, hc_ref, cprev_ref, hprev_ref, cw_ref, u_ref, block % blocks_per_seq == 0)
    y_ref[slot, :, 0:half] = ya.astype(BF16)
    for cols, s in _band_scores(q_ref, k_refs, bias_ref, heads, scale):
        o = _band_softmax_pv(s, v_refs, cols)
        y_ref[slot, :, half + cols.start:half + cols.stop] = o.astype(BF16)


def conv_attention_outproj(conv_in, qkv, conv_w, rel_bias, w_out, layer, h, g, seq, *, tq=256):
    t, d = h.shape
    half = conv_w.shape[1]
    heads = rel_bias.shape[0]
    n_kblk = ATT_PAST_CHUNKS * CHUNK // tq + 1
    nq = seq // tq
    nblk = t // tq
    bias = _band_bias(rel_bias, tq, n_kblk)
    blk = lambda s: jnp.minimum(s, nblk - 1)
    prev_blk = lambda s: jnp.maximum(s - 1, 0)

    def kv_spec(col, j):
        def index(s):
            i = blk(s) % nq
            return (blk(s) - i + jnp.maximum(i - (n_kblk - 1) + j, 0), col)
        return pl.BlockSpec((tq, half), index)

    cur = lambda col: pl.BlockSpec((tq, half), lambda s: (blk(s), col))
    prev_rows = lambda col: pl.BlockSpec((SUB, half), lambda s: (jnp.maximum(blk(s) * (tq // SUB) - 1, 0), col))
    in_specs = [cur(0)] + [kv_spec(1, j) for j in range(n_kblk)] + [kv_spec(2, j) for j in range(n_kblk)]
    in_specs += [pl.BlockSpec((None, heads, tq, n_kblk * tq),
                              lambda s: (jnp.minimum(blk(s) % nq, n_kblk - 1), 0, 0, 0)),
                 cur(0), cur(1), cur(2), prev_rows(1), prev_rows(2),
                 pl.BlockSpec((CONV_WIDTH, half), lambda s: (0, 0)),
                 pl.BlockSpec((None, d, d), lambda s: (layer, 0, 0), pipeline_mode=pl.Buffered(1)),
                 pl.BlockSpec((tq, d), lambda s: (prev_blk(s), 0)),
                 pl.BlockSpec((1, d), lambda s: (0, 0))]
    return pl.pallas_call(
        functools.partial(_conv_attention_outproj_kernel, n_kblk=n_kblk, heads=heads,
                          scale=(half // heads) ** -0.5 * LOG2_E, blocks_per_seq=nq),
        grid=(nblk + 1,),
        in_specs=in_specs,
        out_specs=pl.BlockSpec((tq, d), lambda s: (prev_blk(s), 0)),
        out_shape=jax.ShapeDtypeStruct((t, d), F32),
        scratch_shapes=[pltpu.VMEM((2, tq, d), BF16), pltpu.VMEM((tq + SUB, half), F32)],
        compiler_params=_params("arbitrary"),
    )(*([qkv] * (1 + 2 * n_kblk)), bias, *([conv_in] * 5), conv_w, w_out, h, g)


def _outproj_kernel(ya_ref, yb_ref, w_ref, h_ref, g_ref, o_ref):
    y = jnp.concatenate([ya_ref[...], yb_ref[...]], axis=1)
    y = jnp.dot(y, w_ref[...], preferred_element_type=F32)
    o_ref[...] = h_ref[...] + _rms(y, g_ref[...])


def outproj_residual(ya, yb, w, layer, h, g, *, tm=512):
    t, d = h.shape
    row = lambda a: pl.BlockSpec((tm, a.shape[1]), lambda i: (i, 0))
    return pl.pallas_call(
        _outproj_kernel, grid=(t // tm,),
        in_specs=[row(ya), row(yb), pl.BlockSpec((None,) + w.shape[1:], lambda i: (layer, 0, 0)), row(h),
                  pl.BlockSpec((1, d), lambda i: (0, 0))],
        out_specs=row(h),
        out_shape=jax.ShapeDtypeStruct((t, d), F32),
        compiler_params=_params("parallel"),
    )(ya, yb, w, h, g)


def _mlp_kernel(h_ref, g_in_ref, w1_ref, w2_ref, g_out_ref, o_ref, u_ref, *, parts):
    j = pl.program_id(1)
    last = pl.num_programs(1) - 1
    step = o_ref.shape[0] // parts
    row_parts = [slice(p * step, (p + 1) * step) for p in range(parts)]

    def ff_block(u):
        a = jnp.dot(u, w1_ref[...].astype(BF16), preferred_element_type=F32)
        a = jnp.square(jnp.maximum(a, 0.0)).astype(BF16)
        return jnp.dot(a, w2_ref[...].astype(BF16), preferred_element_type=F32)

    @pl.when(j == 0)
    def _():
        for rows in row_parts:
            u = _rms(h_ref[rows, :], g_in_ref[...]).astype(BF16)
            u_ref[rows, :] = u
            o_ref[rows, :] = ff_block(u)

    @pl.when((j > 0) & (j < last))
    def _():
        o_ref[...] += ff_block(u_ref[...])

    @pl.when(j == last)
    def _():
        for rows in row_parts:
            z = o_ref[rows, :] + ff_block(u_ref[rows, :])
            o_ref[rows, :] = h_ref[rows, :] + _rms(z, g_out_ref[...])


def mlp_residual(h, g_in, w1, w2, layer, g_out, *, tm=1024, tf=512, parts=2):
    t, d = h.shape
    ff = w1.shape[2]
    assert ff // tf >= 2, "first and last ff block are handled by different branches"
    vec = pl.BlockSpec((1, d), lambda i, j: (0, 0))
    return pl.pallas_call(
        functools.partial(_mlp_kernel, parts=parts), grid=(t // tm, ff // tf),
        in_specs=[pl.BlockSpec((tm, d), lambda i, j: (i, 0)), vec,
                  pl.BlockSpec((None, d, tf), lambda i, j: (layer, 0, j)),
                  pl.BlockSpec((None, tf, d), lambda i, j: (layer, j, 0)), vec],
        out_specs=pl.BlockSpec((tm, d), lambda i, j: (i, 0)),
        out_shape=jax.ShapeDtypeStruct((t, d), F32),
        scratch_shapes=[pltpu.VMEM((tm, d), BF16)],
        compiler_params=_params("parallel", "arbitrary"),
    )(h, g_in, w1, w2, g_out)


def _split3_bf16(x):
    hi = x.astype(BF16)
    r = x - hi.astype(F32)
    mid = r.astype(BF16)
    lo = (r - mid.astype(F32)).astype(BF16)
    return hi, mid, lo


def _recurrence_operands(q, k, v, log2_a):
    r, dk = q.shape
    nch, nsb = r // CHUNK, r // SUB
    row = lax.broadcasted_iota(jnp.int32, (CHUNK, CHUNK), 0)
    col = lax.broadcasted_iota(jnp.int32, (CHUNK, CHUNK), 1)
    tri = (row >= col).astype(BF16)
    parts = _split3_bf16(log2_a)
    b = jnp.concatenate(
        [sum(jnp.dot(tri, p[c * CHUNK:(c + 1) * CHUNK], preferred_element_type=F32) for p in parts)
         for c in range(nch)], axis=0)

    b3 = b.reshape(nch, CHUNK, dk)
    q3 = q.reshape(nch, CHUNK, dk)
    b_last = b3[:, CHUNK - 1:CHUNK, :]
    qb = (q * jnp.exp2(b)).astype(BF16)
    kb = (k.reshape(nch, CHUNK, dk) * jnp.exp2(b_last - b3)).reshape(r, dk).astype(BF16)
    d_last = jnp.exp2(b_last)

    b4 = b.reshape(nsb, SUB, dk)
    q4 = q.reshape(nsb, SUB, dk)
    k4 = k.reshape(nsb, SUB, dk)

    kt3 = (k4 * jnp.exp2(b4[:, SUB - 1:SUB, :] - b4)).reshape(nch, CHUNK, dk).astype(BF16)
    sub_of_row = lax.broadcasted_iota(jnp.int32, (1, CHUNK, 1), 1) // SUB
    q_slots, k_slots = [], []
    for j in range(N_SUB - 1):
        lo = (j + 1) * SUB
        end_j = b3[:, lo - 1:lo, :]
        qt = (q3[:, lo:, :] * jnp.exp2(b3[:, lo:, :] - end_j)).astype(BF16)
        q_slots.append(jnp.concatenate([jnp.zeros((nch, lo, dk), BF16), qt], axis=1))
        k_slots.append(jnp.where(sub_of_row == j, kt3, jnp.zeros_like(kt3)))
    q_cat = jnp.concatenate(q_slots, axis=2)
    k_cat = jnp.concatenate(k_slots, axis=2)

    c4 = jnp.exp2(log2_a.reshape(nsb, SUB, dk))
    kd = k4
    prods = [(q4 * k4).reshape(r, dk).astype(BF16)]
    for _ in range(1, SUB):
        kd = c4 * pltpu.roll(kd, 1, axis=1)
        prods.append((q4 * kd).reshape(r, dk).astype(BF16))
    slot = lax.broadcasted_iota(jnp.int32, (SUB * dk, LANES), 0) // dk
    lane_w = lax.broadcasted_iota(jnp.int32, (SUB * dk, LANES), 1)
    to_lane = (lane_w == (LANES - slot) % LANES).astype(BF16)
    diag = jnp.dot(jnp.concatenate(prods, axis=1), to_lane, preferred_element_type=F32)
    lane = lax.broadcasted_iota(jnp.int32, (r, LANES), 1)
    t_sub = lax.broadcasted_iota(jnp.int32, (r, LANES), 0) % SUB
    diag = jnp.where((lane == 0) | (lane + t_sub >= LANES), diag, 0.0)
    diag = jnp.concatenate(
        [pltpu.roll(diag[c * CHUNK:(c + 1) * CHUNK], 0, axis=1, stride=1, stride_axis=0) for c in range(nch)], axis=0)

    return dict(q_cat=q_cat, k_cat=k_cat, diag=diag, qb=qb, kb=kb, d_last=d_last, v=v, vb=v.astype(BF16))


def _gated_recurrence_heads(heads, st_ref):
    ops = [_recurrence_operands(*hd) for hd in heads]
    nch = ops[0]["q_cat"].shape[0]
    chunk = lambda c: slice(c * CHUNK, (c + 1) * CHUNK)
    scores = [[_nt_dot(p["q_cat"][c], p["k_cat"][c]) + p["diag"][chunk(c), :CHUNK] for p in ops]
              for c in range(nch)]
    intra = [[jnp.dot(scores[c][i].astype(BF16), p["vb"][chunk(c)], preferred_element_type=F32)
              for i, p in enumerate(ops)] for c in range(nch)]
    upd = [[jnp.dot(p["v"][chunk(c)].T.astype(BF16), p["kb"][chunk(c)], preferred_element_type=F32)
            for p in ops] for c in range(nch)]
    sts = [st_ref[i] for i in range(len(ops))]
    outs = [[] for _ in ops]
    for c in range(nch):
        for i, p in enumerate(ops):
            outs[i].append(intra[c][i] + _nt_dot(p["qb"][chunk(c)], sts[i].astype(BF16)))
            sts[i] = sts[i] * p["d_last"][c] + upd[c][i]
    for i, st in enumerate(sts):
        st_ref[i] = st
    return [jnp.concatenate(o, axis=0) for o in outs]


def _hgrn_kernel(q_ref, f_ref, i_ref, g_ref, lb_ref, ng_ref, o_ref, st_ref, *, layer):
    @pl.when(pl.program_id(2) == 0)
    def _():
        st_ref[...] = jnp.zeros_like(st_ref)

    lb_raw = lb_ref[...]
    e = jnp.exp(lb_raw - jnp.max(lb_raw, axis=0, keepdims=True))
    soft = e / jnp.sum(e, axis=0, keepdims=True)
    lb_all = jnp.sum(soft[:layer + 1], axis=0, keepdims=True) - soft[0:1]

    d = LANES
    head_cols = [slice(hh * d, (hh + 1) * d) for hh in range(st_ref.shape[0])]
    heads = []
    for cols in head_cols:
        lb = lb_all[:, cols]
        q_raw = q_ref[:, cols]
        f = lb + (1.0 - lb) * _sigmoid(f_ref[:, cols])
        heads.append((q_raw * _sigmoid(q_raw), 1.0 - f, i_ref[:, cols], jnp.log2(f)))
    for cols, o in zip(head_cols, _gated_recurrence_heads(heads, st_ref)):
        g_raw = g_ref[:, cols]
        o_ref[:, cols] = (_rms(o, ng_ref[:, cols]) * (g_raw * _sigmoid(g_raw))).astype(o_ref.dtype)


def hgrn2_mixer(proj, lb, norm_g, batch, seq, layer, *, rows=512, heads_per_step=8):
    t = proj.shape[0]
    d = LANES
    w = heads_per_step * d
    groups = HGRN_HEADS // heads_per_step
    nblk = seq // rows
    col = lambda base: pl.BlockSpec((rows, w), lambda b, h, i: (b * nblk + i, base + h))
    return pl.pallas_call(
        functools.partial(_hgrn_kernel, layer=layer),
        grid=(batch, groups, nblk),
        in_specs=[col(0), col(groups), col(2 * groups), col(3 * groups),
                  pl.BlockSpec((lb.shape[0], w), lambda b, h, i: (0, h)),
                  pl.BlockSpec((1, w), lambda b, h, i: (0, h))],
        out_specs=pl.BlockSpec((rows, w), lambda b, h, i: (b * nblk + i, h)),
        out_shape=jax.ShapeDtypeStruct((t, HGRN_HEADS * d), BF16),
        scratch_shapes=[pltpu.VMEM((heads_per_step, d, d), F32)],
        compiler_params=_params("parallel", "parallel", "arbitrary"),
    )(proj, proj, proj, proj, lb, norm_g)


def _gla_kernel(q_ref, k_ref, v_ref, r_ref, a_ref, wa_ref, ba_ref, ng_ref, o_ref, st_ref, *, q_scale):
    @pl.when(pl.program_id(2) == 0)
    def _():
        st_ref[...] = jnp.zeros_like(st_ref)

    x = jnp.dot(a_ref[...].astype(BF16), wa_ref[...], preferred_element_type=F32) + ba_ref[...]
    x2 = x * LOG2_E
    log2_a = (jnp.minimum(x2, 0.0) - jnp.log2(1.0 + jnp.exp2(-jnp.abs(x2)))) / GLA_GATE_NORMALIZER
    nh, dv, dk = st_ref.shape
    k_cols = [slice(hh * dk, (hh + 1) * dk) for hh in range(nh)]
    v_cols = [slice(hh * dv, (hh + 1) * dv) for hh in range(nh)]
    heads = [(q_ref[:, kc] * q_scale, k_ref[:, kc], v_ref[:, vc], log2_a[:, kc]) for kc, vc in zip(k_cols, v_cols)]
    for vc, o in zip(v_cols, _gated_recurrence_heads(heads, st_ref)):
        r_raw = r_ref[:, vc]
        o_ref[:, vc] = (_rms(o, ng_ref[:, vc]) * (r_raw * _sigmoid(r_raw))).astype(o_ref.dtype)


def gla_mixer(proj, a_lr, wa2, ba, norm_g, batch, seq, col0, *, rows=512, heads_per_step=4):
    t = proj.shape[0]
    dk = LANES
    dv = 2 * dk
    wk, wv = heads_per_step * dk, heads_per_step * dv
    groups = GLA_HEADS // heads_per_step
    nblk = seq // rows
    cq, ck = col0 // wk, col0 // wk + groups
    cv = (col0 + 2 * GLA_HEADS * dk) // wv
    cr = cv + groups
    spec = lambda w, base: pl.BlockSpec((rows, w), lambda b, h, i: (b * nblk + i, base + h))
    return pl.pallas_call(
        functools.partial(_gla_kernel, q_scale=dk ** -0.5),
        grid=(batch, groups, nblk),
        in_specs=[spec(wk, cq), spec(wk, ck), spec(wv, cv), spec(wv, cr),
                  pl.BlockSpec((rows, a_lr.shape[1]), lambda b, h, i: (b * nblk + i, 0)),
                  pl.BlockSpec((wa2.shape[0], wk), lambda b, h, i: (0, h)),
                  pl.BlockSpec((1, wk), lambda b, h, i: (0, h)),
                  pl.BlockSpec((1, wv), lambda b, h, i: (0, h))],
        out_specs=pl.BlockSpec((rows, wv), lambda b, h, i: (b * nblk + i, h)),
        out_shape=jax.ShapeDtypeStruct((t, GLA_HEADS * dv), BF16),
        scratch_shapes=[pltpu.VMEM((heads_per_step, dv, dk), F32)],
        compiler_params=_params("parallel", "parallel", "arbitrary"),
    )(proj, proj, proj, proj, a_lr, wa2, ba, norm_g)


def kernel(x, norm_g, even_w_in, even_conv_w, even_rel_bias, even_w_out, odd_w_in, hgrn_lb, hgrn_norm_g,
           gla_wa2, gla_ba, gla_norm_g, odd_w_out, mlp_w1, mlp_w2):
    batch, seq, d = x.shape
    depth = norm_g.shape[0]
    half = d // 2
    even_w_in_b, even_w_out_b = even_w_in.astype(BF16), even_w_out.astype(BF16)
    n_odd_main = odd_w_in.shape[2] - GLA_GATE_RANK
    odd_w_in_b, odd_w_out_b = odd_w_in.astype(BF16), odd_w_out.astype(BF16)
    lane_pad = LANES - GLA_GATE_RANK
    h = x.reshape(batch * seq, d)
    for l in range(depth):
        g = norm_g[l][:, None, :]
        e = l // 2
        if l % 2 == 0:
            conv_in, qkv = rms_matmul(h, g[0], even_w_in_b, e, [(3 * half, F32, 2), (3 * half, BF16, 1)])
            h = conv_attention_outproj(conv_in, qkv, even_conv_w[e], even_rel_bias[e], even_w_out_b, e, h, g[1], seq)
        else:
            w_gate = jnp.pad(odd_w_in[e][:, n_odd_main:], ((0, 0), (0, lane_pad))).astype(BF16)
            proj, a_lr = rms_matmul(h, g[0], odd_w_in_b, e, [(n_odd_main, F32, 4)], w_gate)
            ya = hgrn2_mixer(proj, hgrn_lb, hgrn_norm_g[e][None, :], batch, seq, l)
            wa2 = jnp.pad(gla_wa2[e], ((0, lane_pad), (0, 0))).astype(BF16)
            yb = gla_mixer(proj, a_lr, wa2, gla_ba[e][None, :], gla_norm_g[e][None, :], batch, seq,
                           4 * HGRN_HEADS * LANES)
            h = outproj_residual(ya, yb, odd_w_out_b, e, h, g[1])
        h = mlp_residual(h, g[2], mlp_w1, mlp_w2, l, g[3])
    return h.reshape(batch, seq, d)
```

```python
import functools

import numpy as np
import jax
import jax.numpy as jnp
from jax import lax
from jax.experimental import pallas as pl
from jax.experimental.pallas import tpu as pltpu

F32 = jnp.float32
BF16 = jnp.bfloat16

EPS = 1e-6
LOG2_E = 1.4426950408889634
CHUNK = 64
SUB = 8
LANES = 128
N_SUB = CHUNK // SUB

CONV_WIDTH = 3
ATT_HEADS = 8
ATT_PAST_CHUNKS = 8
REL_CLIP = 256
HGRN_HEADS = 8
GLA_HEADS = 4
GLA_GATE_RANK = 16
GLA_GATE_NORMALIZER = 16.0

VMEM_LIMIT = 60 * 1024 * 1024


def _params(*sem):
    return pltpu.CompilerParams(dimension_semantics=sem, vmem_limit_bytes=VMEM_LIMIT)


def _rms(x, g):
    ms = jnp.mean(x * x, axis=-1, keepdims=True)
    return x * lax.rsqrt(ms + EPS) * g


def _sigmoid(x):
    return 1.0 / (1.0 + jnp.exp(-x))


def _nt_dot(a, b):
    return lax.dot_general(a, b, (((1,), (1,)), ((), ())), preferred_element_type=F32)


def _rms_matmul_kernel(h_ref, g_ref, w_ref, *refs, col_steps, has_extra):
    u_ref = refs[-1]
    wx_ref, ox_ref = (refs[0], refs[-2]) if has_extra else (None, None)
    o_refs = refs[1:-2] if has_extra else refs[:-1]
    j = pl.program_id(1)

    @pl.when(j == 0)
    def _():
        u = _rms(h_ref[...], g_ref[...]).astype(BF16)
        u_ref[...] = u
        if has_extra:
            ox_ref[...] = jnp.dot(u, wx_ref[...], preferred_element_type=F32)

    for jj, (out_idx, col0, tn) in enumerate(col_steps):
        @pl.when(j == jj)
        def _():
            o_ref = o_refs[out_idx]
            y = jnp.dot(u_ref[...], w_ref[:, col0:col0 + tn], preferred_element_type=F32)
            o_ref[...] = y.astype(o_ref.dtype)


def rms_matmul(h, g, w, layer, outs, w_extra=None, *, tm=512):
    t, d = h.shape
    resident = pl.Buffered(1)
    in_specs = [pl.BlockSpec((tm, d), lambda i, j: (i, 0)),
                pl.BlockSpec((1, d), lambda i, j: (0, 0)),
                pl.BlockSpec((None, d, w.shape[2]), lambda i, j: (layer, 0, 0), pipeline_mode=resident)]
    args = [h, g, w]
    if w_extra is not None:
        in_specs.append(pl.BlockSpec(w_extra.shape, lambda i, j: (0, 0), pipeline_mode=resident))
        args.append(w_extra)
    out_specs, out_shape, col_steps = [], [], []
    col = 0
    for k, (ncols, dtype, nblk) in enumerate(outs):
        tn = ncols // nblk
        j0 = len(col_steps)
        out_specs.append(pl.BlockSpec(
            (tm, tn), lambda i, j, j0=j0, nblk=nblk: (i, jnp.clip(j - j0, 0, nblk - 1))))
        out_shape.append(jax.ShapeDtypeStruct((t, ncols), dtype))
        col_steps += [(k, col + b * tn, tn) for b in range(nblk)]
        col += ncols
    if w_extra is not None:
        nx = w_extra.shape[1]
        out_specs.append(pl.BlockSpec((tm, nx), lambda i, j: (i, 0)))
        out_shape.append(jax.ShapeDtypeStruct((t, nx), F32))
    return pl.pallas_call(
        functools.partial(_rms_matmul_kernel, col_steps=tuple(col_steps), has_extra=w_extra is not None),
        grid=(t // tm, len(col_steps)),
        in_specs=in_specs, out_specs=out_specs, out_shape=out_shape,
        scratch_shapes=[pltpu.VMEM((tm, d), BF16)],
        compiler_params=_params("parallel", "arbitrary"),
    )(*args)


def _gated_conv_rows(b_ref, c_ref, hc_ref, cprev_ref, hprev_ref, w_ref, u_ref, first):
    tm = b_ref.shape[0]
    prev = cprev_ref[...] * hprev_ref[...]
    u_ref[0:SUB, :] = jnp.where(first, 0.0, prev)
    u_ref[SUB:, :] = c_ref[...] * hc_ref[...]
    w = w_ref[...]
    y = w[0:1] * u_ref[pl.ds(SUB - 2, tm), :]
    for j in range(1, CONV_WIDTH):
        y = y + w[j:j + 1] * u_ref[pl.ds(SUB - 2 + j, tm), :]
    return b_ref[...] * y


def _band_bias(rel_bias, tq, n_kblk):
    band = (ATT_PAST_CHUNKS + 1) * CHUNK
    heads = rel_bias.shape[0]
    rel_bias = rel_bias.astype(F32) * LOG2_E
    n_far = band - REL_CLIP
    ext = jnp.concatenate([jnp.broadcast_to(rel_bias[:, -1:], (heads, n_far)), rel_bias[:, ::-1][:, 1:]], axis=1)
    rows = jnp.stack([ext[:, CHUNK - 1 - qi:CHUNK - 1 - qi + band] for qi in range(CHUNK)], axis=1)
    tk = n_kblk * tq
    neg = lambda w: jnp.full((heads, CHUNK, w), -1e30, F32)
    blocks = [jnp.concatenate([neg(c * CHUNK), rows, neg(tk - band - c * CHUNK)], axis=2)
              for c in range(tq // CHUNK)]
    base = jnp.concatenate(blocks, axis=1)
    kblk = np.arange(tk)[None, :] // tq
    variants = [jnp.where(kblk + v >= n_kblk - 1, base, -1e30) for v in range(n_kblk)]
    return jnp.stack(variants, axis=0)


def _band_scores(q_ref, k_refs, bias_ref, heads, scale):
    dh = q_ref.shape[1] // heads
    scores = []
    for h in range(heads):
        cols = slice(h * dh, (h + 1) * dh)
        q = (q_ref[:, cols].astype(F32) * scale).astype(BF16)
        k = jnp.concatenate([r[:, cols] for r in k_refs], axis=0).astype(BF16)
        scores.append((cols, _nt_dot(q, k) + bias_ref[h]))
    return scores


def _band_softmax_pv(s, v_refs, cols):
    v = jnp.concatenate([r[:, cols] for r in v_refs], axis=0).astype(BF16)
    dh = v.shape[1]
    v_ones = jnp.concatenate([v, jnp.ones_like(v)], axis=1)
    m = jnp.max(s, axis=-1, keepdims=True)
    p = jnp.exp2(s - m).astype(BF16)
    o_l = jnp.dot(p, v_ones, preferred_element_type=F32)
    return o_l[:, :dh] / o_l[:, dh:dh + 1]


def _conv_attention_outproj_kernel(q_ref, *refs, n_kblk, heads, scale, blocks_per_seq):
    k_refs, v_refs = refs[:n_kblk], refs[n_kblk:2 * n_kblk]
    (bias_ref, b_ref, c_ref, hc_ref, cprev_ref, hprev_ref, cw_ref, w_ref, h_ref, g_ref,
     o_ref, y_ref, u_ref) = refs[2 * n_kblk:]
    step = pl.program_id(0)

    @pl.when(step == 0)
    def _():
        y_ref[...] = jnp.zeros_like(y_ref)

    slot = step % 2
    y = jnp.dot(y_ref[1 - slot], w_ref[...], preferred_element_type=F32)
    o_ref[...] = h_ref[...] + _rms(y, g_ref[...])

    block = jnp.minimum(step, pl.num_programs(0) - 2)
    half = b_ref.shape[1]
    ya = _gated_conv_rows(b_ref, c_ref, hc_ref, cprev_ref, hprev_ref, cw_ref, u_ref, block % blocks_per_seq == 0)
    y_ref[slot, :, 0:half] = ya.astype(BF16)
    for cols, s in _band_scores(q_ref, k_refs, bias_ref, heads, scale):
        o = _band_softmax_pv(s, v_refs, cols)
        y_ref[slot, :, half + cols.start:half + cols.stop] = o.astype(BF16)


def conv_attention_outproj(conv_in, qkv, conv_w, rel_bias, w_out, layer, h, g, seq, *, tq=256):
    t, d = h.shape
    half = conv_w.shape[1]
    heads = rel_bias.shape[0]
    n_kblk = ATT_PAST_CHUNKS * CHUNK // tq + 1
    nq = seq // tq
    nblk = t // tq
    bias = _band_bias(rel_bias, tq, n_kblk)
    blk = lambda s: jnp.minimum(s, nblk - 1)
    prev_blk = lambda s: jnp.maximum(s - 1, 0)

    def kv_spec(col, j):
        def index(s):
            i = blk(s) % nq
            return (blk(s) - i + jnp.maximum(i - (n_kblk - 1) + j, 0), col)
        return pl.BlockSpec((tq, half), index)

    cur = lambda col: pl.BlockSpec((tq, half), lambda s: (blk(s), col))
    prev_rows = lambda col: pl.BlockSpec((SUB, half), lambda s: (jnp.maximum(blk(s) * (tq // SUB) - 1, 0), col))
    in_specs = [cur(0)] + [kv_spec(1, j) for j in range(n_kblk)] + [kv_spec(2, j) for j in range(n_kblk)]
    in_specs += [pl.BlockSpec((None, heads, tq, n_kblk * tq),
                              lambda s: (jnp.minimum(blk(s) % nq, n_kblk - 1), 0, 0, 0)),
                 cur(0), cur(1), cur(2), prev_rows(1), prev_rows(2),
                 pl.BlockSpec((CONV_WIDTH, half), lambda s: (0, 0)),
                 pl.BlockSpec((None, d, d), lambda s: (layer, 0, 0), pipeline_mode=pl.Buffered(1)),
                 pl.BlockSpec((tq, d), lambda s: (prev_blk(s), 0)),
                 pl.BlockSpec((1, d), lambda s: (0, 0))]
    return pl.pallas_call(
        functools.partial(_conv_attention_outproj_kernel, n_kblk=n_kblk, heads=heads,
                          scale=(half // heads) ** -0.5 * LOG2_E, blocks_per_seq=nq),
        grid=(nblk + 1,),
        in_specs=in_specs,
        out_specs=pl.BlockSpec((tq, d), lambda s: (prev_blk(s), 0)),
        out_shape=jax.ShapeDtypeStruct((t, d), F32),
        scratch_shapes=[pltpu.VMEM((2, tq, d), BF16), pltpu.VMEM((tq + SUB, half), F32)],
        compiler_params=_params("arbitrary"),
    )(*([qkv] * (1 + 2 * n_kblk)), bias, *([conv_in] * 5), conv_w, w_out, h, g)


def _outproj_kernel(ya_ref, yb_ref, w_ref, h_ref, g_ref, o_ref):
    y = jnp.concatenate([ya_ref[...], yb_ref[...]], axis=1)
    y = jnp.dot(y, w_ref[...], preferred_element_type=F32)
    o_ref[...] = h_ref[...] + _rms(y, g_ref[...])


def outproj_residual(ya, yb, w, layer, h, g, *, tm=512):
    t, d = h.shape
    row = lambda a: pl.BlockSpec((tm, a.shape[1]), lambda i: (i, 0))
    return pl.pallas_call(
        _outproj_kernel, grid=(t // tm,),
        in_specs=[row(ya), row(yb), pl.BlockSpec((None,) + w.shape[1:], lambda i: (layer, 0, 0)), row(h),
                  pl.BlockSpec((1, d), lambda i: (0, 0))],
        out_specs=row(h),
        out_shape=jax.ShapeDtypeStruct((t, d), F32),
        compiler_params=_params("parallel"),
    )(ya, yb, w, h, g)


def _mlp_kernel(h_ref, g_in_ref, w1_ref, w2_ref, g_out_ref, o_ref, u_ref, *, parts):
    j = pl.program_id(1)
    last = pl.num_programs(1) - 1
    step = o_ref.shape[0] // parts
    row_parts = [slice(p * step, (p + 1) * step) for p in range(parts)]

    def ff_block(u):
        a = jnp.dot(u, w1_ref[...].astype(BF16), preferred_element_type=F32)
        a = jnp.square(jnp.maximum(a, 0.0)).astype(BF16)
        return jnp.dot(a, w2_ref[...].astype(BF16), preferred_element_type=F32)

    @pl.when(j == 0)
    def _():
        for rows in row_parts:
            u = _rms(h_ref[rows, :], g_in_ref[...]).astype(BF16)
            u_ref[rows, :] = u
            o_ref[rows, :] = ff_block(u)

    @pl.when((j > 0) & (j < last))
    def _():
        o_ref[...] += ff_block(u_ref[...])

    @pl.when(j == last)
    def _():
        for rows in row_parts:
            z = o_ref[rows, :] + ff_block(u_ref[rows, :])
            o_ref[rows, :] = h_ref[rows, :] + _rms(z, g_out_ref[...])


def mlp_residual(h, g_in, w1, w2, layer, g_out, *, tm=1024, tf=512, parts=2):
    t, d = h.shape
    ff = w1.shape[2]
    assert ff // tf >= 2, "first and last ff block are handled by different branches"
    vec = pl.BlockSpec((1, d), lambda i, j: (0, 0))
    return pl.pallas_call(
        functools.partial(_mlp_kernel, parts=parts), grid=(t // tm, ff // tf),
        in_specs=[pl.BlockSpec((tm, d), lambda i, j: (i, 0)), vec,
                  pl.BlockSpec((None, d, tf), lambda i, j: (layer, 0, j)),
                  pl.BlockSpec((None, tf, d), lambda i, j: (layer, j, 0)), vec],
        out_specs=pl.BlockSpec((tm, d), lambda i, j: (i, 0)),
        out_shape=jax.ShapeDtypeStruct((t, d), F32),
        scratch_shapes=[pltpu.VMEM((tm, d), BF16)],
        compiler_params=_params("parallel", "arbitrary"),
    )(h, g_in, w1, w2, g_out)


def _split3_bf16(x):
    hi = x.astype(BF16)
    r = x - hi.astype(F32)
    mid = r.astype(BF16)
    lo = (r - mid.astype(F32)).astype(BF16)
    return hi, mid, lo


def _recurrence_operands(q, k, v, log2_a):
    r, dk = q.shape
    nch, nsb = r // CHUNK, r // SUB
    row = lax.broadcasted_iota(jnp.int32, (CHUNK, CHUNK), 0)
    col = lax.broadcasted_iota(jnp.int32, (CHUNK, CHUNK), 1)
    tri = (row >= col).astype(BF16)
    parts = _split3_bf16(log2_a)
    b = jnp.concatenate(
        [sum(jnp.dot(tri, p[c * CHUNK:(c + 1) * CHUNK], preferred_element_type=F32) for p in parts)
         for c in range(nch)], axis=0)

    b3 = b.reshape(nch, CHUNK, dk)
    q3 = q.reshape(nch, CHUNK, dk)
    b_last = b3[:, CHUNK - 1:CHUNK, :]
    qb = (q * jnp.exp2(b)).astype(BF16)
    kb = (k.reshape(nch, CHUNK, dk) * jnp.exp2(b_last - b3)).reshape(r, dk).astype(BF16)
    d_last = jnp.exp2(b_last)

    b4 = b.reshape(nsb, SUB, dk)
    q4 = q.reshape(nsb, SUB, dk)
    k4 = k.reshape(nsb, SUB, dk)

    kt3 = (k4 * jnp.exp2(b4[:, SUB - 1:SUB, :] - b4)).reshape(nch, CHUNK, dk).astype(BF16)
    sub_of_row = lax.broadcasted_iota(jnp.int32, (1, CHUNK, 1), 1) // SUB
    q_slots, k_slots = [], []
    for j in range(N_SUB - 1):
        lo = (j + 1) * SUB
        end_j = b3[:, lo - 1:lo, :]
        qt = (q3[:, lo:, :] * jnp.exp2(b3[:, lo:, :] - end_j)).astype(BF16)
        q_slots.append(jnp.concatenate([jnp.zeros((nch, lo, dk), BF16), qt], axis=1))
        k_slots.append(jnp.where(sub_of_row == j, kt3, jnp.zeros_like(kt3)))
    q_cat = jnp.concatenate(q_slots, axis=2)
    k_cat = jnp.concatenate(k_slots, axis=2)

    c4 = jnp.exp2(log2_a.reshape(nsb, SUB, dk))
    kd = k4
    prods = [(q4 * k4).reshape(r, dk).astype(BF16)]
    for _ in range(1, SUB):
        kd = c4 * pltpu.roll(kd, 1, axis=1)
        prods.append((q4 * kd).reshape(r, dk).astype(BF16))
    slot = lax.broadcasted_iota(jnp.int32, (SUB * dk, LANES), 0) // dk
    lane_w = lax.broadcasted_iota(jnp.int32, (SUB * dk, LANES), 1)
    to_lane = (lane_w == (LANES - slot) % LANES).astype(BF16)
    diag = jnp.dot(jnp.concatenate(prods, axis=1), to_lane, preferred_element_type=F32)
    lane = lax.broadcasted_iota(jnp.int32, (r, LANES), 1)
    t_sub = lax.broadcasted_iota(jnp.int32, (r, LANES), 0) % SUB
    diag = jnp.where((lane == 0) | (lane + t_sub >= LANES), diag, 0.0)
    diag = jnp.concatenate(
        [pltpu.roll(diag[c * CHUNK:(c + 1) * CHUNK], 0, axis=1, stride=1, stride_axis=0) for c in range(nch)], axis=0)

    return dict(q_cat=q_cat, k_cat=k_cat, diag=diag, qb=qb, kb=kb, d_last=d_last, v=v, vb=v.astype(BF16))


def _gated_recurrence_heads(heads, st_ref):
    ops = [_recurrence_operands(*hd) for hd in heads]
    nch = ops[0]["q_cat"].shape[0]
    chunk = lambda c: slice(c * CHUNK, (c + 1) * CHUNK)
    scores = [[_nt_dot(p["q_cat"][c], p["k_cat"][c]) + p["diag"][chunk(c), :CHUNK] for p in ops]
              for c in range(nch)]
    intra = [[jnp.dot(scores[c][i].astype(BF16), p["vb"][chunk(c)], preferred_element_type=F32)
              for i, p in enumerate(ops)] for c in range(nch)]
    upd = [[jnp.dot(p["v"][chunk(c)].T.astype(BF16), p["kb"][chunk(c)], preferred_element_type=F32)
            for p in ops] for c in range(nch)]
    sts = [st_ref[i] for i in range(len(ops))]
    outs = [[] for _ in ops]
    for c in range(nch):
        for i, p in enumerate(ops):
            outs[i].append(intra[c][i] + _nt_dot(p["qb"][chunk(c)], sts[i].astype(BF16)))
            sts[i] = sts[i] * p["d_last"][c] + upd[c][i]
    for i, st in enumerate(sts):
        st_ref[i] = st
    return [jnp.concatenate(o, axis=0) for o in outs]


def _hgrn_kernel(q_ref, f_ref, i_ref, g_ref, lb_ref, ng_ref, o_ref, st_ref, *, layer):
    @pl.when(pl.program_id(2) == 0)
    def _():
        st_ref[...] = jnp.zeros_like(st_ref)

    lb_raw = lb_ref[...]
    e = jnp.exp(lb_raw - jnp.max(lb_raw, axis=0, keepdims=True))
    soft = e / jnp.sum(e, axis=0, keepdims=True)
    lb_all = jnp.sum(soft[:layer + 1], axis=0, keepdims=True) - soft[0:1]

    d = LANES
    head_cols = [slice(hh * d, (hh + 1) * d) for hh in range(st_ref.shape[0])]
    heads = []
    for cols in head_cols:
        lb = lb_all[:, cols]
        q_raw = q_ref[:, cols]
        f = lb + (1.0 - lb) * _sigmoid(f_ref[:, cols])
        heads.append((q_raw * _sigmoid(q_raw), 1.0 - f, i_ref[:, cols], jnp.log2(f)))
    for cols, o in zip(head_cols, _gated_recurrence_heads(heads, st_ref)):
        g_raw = g_ref[:, cols]
        o_ref[:, cols] = (_rms(o, ng_ref[:, cols]) * (g_raw * _sigmoid(g_raw))).astype(o_ref.dtype)


def hgrn2_mixer(proj, lb, norm_g, batch, seq, layer, *, rows=512, heads_per_step=8):
    t = proj.shape[0]
    d = LANES
    w = heads_per_step * d
    groups = HGRN_HEADS // heads_per_step
    nblk = seq // rows
    col = lambda base: pl.BlockSpec((rows, w), lambda b, h, i: (b * nblk + i, base + h))
    return pl.pallas_call(
        functools.partial(_hgrn_kernel, layer=layer),
        grid=(batch, groups, nblk),
        in_specs=[col(0), col(groups), col(2 * groups), col(3 * groups),
                  pl.BlockSpec((lb.shape[0], w), lambda b, h, i: (0, h)),
                  pl.BlockSpec((1, w), lambda b, h, i: (0, h))],
        out_specs=pl.BlockSpec((rows, w), lambda b, h, i: (b * nblk + i, h)),
        out_shape=jax.ShapeDtypeStruct((t, HGRN_HEADS * d), BF16),
        scratch_shapes=[pltpu.VMEM((heads_per_step, d, d), F32)],
        compiler_params=_params("parallel", "parallel", "arbitrary"),
    )(proj, proj, proj, proj, lb, norm_g)


def _gla_kernel(q_ref, k_ref, v_ref, r_ref, a_ref, wa_ref, ba_ref, ng_ref, o_ref, st_ref, *, q_scale):
    @pl.when(pl.program_id(2) == 0)
    def _():
        st_ref[...] = jnp.zeros_like(st_ref)

    x = jnp.dot(a_ref[...].astype(BF16), wa_ref[...], preferred_element_type=F32) + ba_ref[...]
    x2 = x * LOG2_E
    log2_a = (jnp.minimum(x2, 0.0) - jnp.log2(1.0 + jnp.exp2(-jnp.abs(x2)))) / GLA_GATE_NORMALIZER
    nh, dv, dk = st_ref.shape
    k_cols = [slice(hh * dk, (hh + 1) * dk) for hh in range(nh)]
    v_cols = [slice(hh * dv, (hh + 1) * dv) for hh in range(nh)]
    heads = [(q_ref[:, kc] * q_scale, k_ref[:, kc], v_ref[:, vc], log2_a[:, kc]) for kc, vc in zip(k_cols, v_cols)]
    for vc, o in zip(v_cols, _gated_recurrence_heads(heads, st_ref)):
        r_raw = r_ref[:, vc]
        o_ref[:, vc] = (_rms(o, ng_ref[:, vc]) * (r_raw * _sigmoid(r_raw))).astype(o_ref.dtype)


def gla_mixer(proj, a_lr, wa2, ba, norm_g, batch, seq, col0, *, rows=512, heads_per_step=4):
    t = proj.shape[0]
    dk = LANES
    dv = 2 * dk
    wk, wv = heads_per_step * dk, heads_per_step * dv
    groups = GLA_HEADS // heads_per_step
    nblk = seq // rows
    cq, ck = col0 // wk, col0 // wk + groups
    cv = (col0 + 2 * GLA_HEADS * dk) // wv
    cr = cv + groups
    spec = lambda w, base: pl.BlockSpec((rows, w), lambda b, h, i: (b * nblk + i, base + h))
    return pl.pallas_call(
        functools.partial(_gla_kernel, q_scale=dk ** -0.5),
        grid=(batch, groups, nblk),
        in_specs=[spec(wk, cq), spec(wk, ck), spec(wv, cv), spec(wv, cr),
                  pl.BlockSpec((rows, a_lr.shape[1]), lambda b, h, i: (b * nblk + i, 0)),
                  pl.BlockSpec((wa2.shape[0], wk), lambda b, h, i: (0, h)),
                  pl.BlockSpec((1, wk), lambda b, h, i: (0, h)),
                  pl.BlockSpec((1, wv), lambda b, h, i: (0, h))],
        out_specs=pl.BlockSpec((rows, wv), lambda b, h, i: (b * nblk + i, h)),
        out_shape=jax.ShapeDtypeStruct((t, GLA_HEADS * dv), BF16),
        scratch_shapes=[pltpu.VMEM((heads_per_step, dv, dk), F32)],
        compiler_params=_params("parallel", "parallel", "arbitrary"),
    )(proj, proj, proj, proj, a_lr, wa2, ba, norm_g)


def kernel(x, norm_g, even_w_in, even_conv_w, even_rel_bias, even_w_out, odd_w_in, hgrn_lb, hgrn_norm_g,
           gla_wa2, gla_ba, gla_norm_g, odd_w_out, mlp_w1, mlp_w2):
    batch, seq, d = x.shape
    depth = norm_g.shape[0]
    half = d // 2
    even_w_in_b, even_w_out_b = even_w_in.astype(BF16), even_w_out.astype(BF16)
    n_odd_main = odd_w_in.shape[2] - GLA_GATE_RANK
    odd_w_in_b, odd_w_out_b = odd_w_in.astype(BF16), odd_w_out.astype(BF16)
    lane_pad = LANES - GLA_GATE_RANK
    h = x.reshape(batch * seq, d)
    for l in range(depth):
        g = norm_g[l][:, None, :]
        e = l // 2
        if l % 2 == 0:
            conv_in, qkv = rms_matmul(h, g[0], even_w_in_b, e, [(3 * half, F32, 1), (3 * half, BF16, 1)])
            h = conv_attention_outproj(conv_in, qkv, even_conv_w[e], even_rel_bias[e], even_w_out_b, e, h, g[1], seq)
        else:
            w_gate = jnp.pad(odd_w_in[e][:, n_odd_main:], ((0, 0), (0, lane_pad))).astype(BF16)
            proj, a_lr = rms_matmul(h, g[0], odd_w_in_b, e, [(n_odd_main, F32, 2)], w_gate)
            ya = hgrn2_mixer(proj, hgrn_lb, hgrn_norm_g[e][None, :], batch, seq, l)
            wa2 = jnp.pad(gla_wa2[e], ((0, lane_pad), (0, 0))).astype(BF16)
            yb = gla_mixer(proj, a_lr, wa2, gla_ba[e][None, :], gla_norm_g[e][None, :], batch, seq,
                           4 * HGRN_HEADS * LANES)
            h = outproj_residual(ya, yb, odd_w_out_b, e, h, g[1])
        h = mlp_residual(h, g[2], mlp_w1, mlp_w2, l, g[3])
    return h.reshape(batch, seq, d)
```

```python
import functools

import numpy as np
import jax
import jax.numpy as jnp
from jax import lax
from jax.experimental import pallas as pl
from jax.experimental.pallas import tpu as pltpu

F32 = jnp.float32
BF16 = jnp.bfloat16

EPS = 1e-6
LOG2_E = 1.4426950408889634
CHUNK = 64
SUB = 8
LANES = 128
N_SUB = CHUNK // SUB

CONV_WIDTH = 3
ATT_PAST_CHUNKS = 8
REL_CLIP = 256
HGRN_HEADS = 8
GLA_HEADS = 4
GLA_GATE_RANK = 16
GLA_GATE_NORMALIZER = 16.0

VMEM_LIMIT = 60 * 1024 * 1024


def _params(*sem):
    return pltpu.CompilerParams(dimension_semantics=sem, vmem_limit_bytes=VMEM_LIMIT)


def _rms(x, g):
    ms = jnp.mean(x * x, axis=-1, keepdims=True)
    return x * lax.rsqrt(ms + EPS) * g


def _sigmoid(x):
    return 1.0 / (1.0 + jnp.exp(-x))


def _nt_dot(a, b):
    return lax.dot_general(a, b, (((1,), (1,)), ((), ())), preferred_element_type=F32)


def _rms_matmul_kernel(h_ref, g_ref, w_ref, *refs, col_steps, has_extra):
    u_ref = refs[-1]
    wx_ref, ox_ref = (refs[0], refs[-2]) if has_extra else (None, None)
    o_refs = refs[1:-2] if has_extra else refs[:-1]
    j = pl.program_id(1)

    @pl.when(j == 0)
    def _():
        u = _rms(h_ref[...], g_ref[...]).astype(BF16)
        u_ref[...] = u
        if has_extra:
            ox_ref[...] = jnp.dot(u, wx_ref[...], preferred_element_type=F32)

    for jj, (out_idx, col0, tn) in enumerate(col_steps):
        @pl.when(j == jj)
        def _():
            o_ref = o_refs[out_idx]
            y = jnp.dot(u_ref[...], w_ref[:, col0:col0 + tn], preferred_element_type=F32)
            o_ref[...] = y.astype(o_ref.dtype)


def rms_matmul(h, g, w, layer, outs, w_extra=None, *, tm=512):
    t, d = h.shape
    resident = pl.Buffered(1)
    in_specs = [pl.BlockSpec((tm, d), lambda i, j: (i, 0)),
                pl.BlockSpec((1, d), lambda i, j: (0, 0)),
                pl.BlockSpec((None, d, w.shape[2]), lambda i, j: (layer, 0, 0), pipeline_mode=resident)]
    args = [h, g, w]
    if w_extra is not None:
        in_specs.append(pl.BlockSpec(w_extra.shape, lambda i, j: (0, 0), pipeline_mode=resident))
        args.append(w_extra)
    out_specs, out_shape, col_steps = [], [], []
    col = 0
    for k, (ncols, dtype, nblk) in enumerate(outs):
        tn = ncols // nblk
        j0 = len(col_steps)
        out_specs.append(pl.BlockSpec(
            (tm, tn), lambda i, j, j0=j0, nblk=nblk: (i, jnp.clip(j - j0, 0, nblk - 1))))
        out_shape.append(jax.ShapeDtypeStruct((t, ncols), dtype))
        col_steps += [(k, col + b * tn, tn) for b in range(nblk)]
        col += ncols
    if w_extra is not None:
        nx = w_extra.shape[1]
        out_specs.append(pl.BlockSpec((tm, nx), lambda i, j: (i, 0)))
        out_shape.append(jax.ShapeDtypeStruct((t, nx), F32))
    return pl.pallas_call(
        functools.partial(_rms_matmul_kernel, col_steps=tuple(col_steps), has_extra=w_extra is not None),
        grid=(t // tm, len(col_steps)),
        in_specs=in_specs, out_specs=out_specs, out_shape=out_shape,
        scratch_shapes=[pltpu.VMEM((tm, d), BF16)],
        compiler_params=_params("parallel", "arbitrary"),
    )(*args)


def _gated_conv_rows(b_ref, c_ref, hc_ref, cprev_ref, hprev_ref, w_ref, u_ref, first):
    tm = b_ref.shape[0]
    prev = cprev_ref[...] * hprev_ref[...]
    u_ref[0:SUB, :] = jnp.where(first, 0.0, prev)
    u_ref[SUB:, :] = c_ref[...] * hc_ref[...]
    w = w_ref[...]
    y = w[0:1] * u_ref[pl.ds(SUB - 2, tm), :]
    for j in range(1, CONV_WIDTH):
        y = y + w[j:j + 1] * u_ref[pl.ds(SUB - 2 + j, tm), :]
    return b_ref[...] * y


def _band_bias(rel_bias, tq, n_kblk):
    band = (ATT_PAST_CHUNKS + 1) * CHUNK
    heads = rel_bias.shape[0]
    assert rel_bias.shape[1] == CHUNK + REL_CLIP, "table covers distances -(CHUNK - 1) .. REL_CLIP"
    rel_bias = rel_bias.astype(F32) * LOG2_E
    n_far = band - REL_CLIP
    ext = jnp.concatenate([jnp.broadcast_to(rel_bias[:, -1:], (heads, n_far)), rel_bias[:, ::-1][:, 1:]], axis=1)
    rows = jnp.stack([ext[:, CHUNK - 1 - qi:CHUNK - 1 - qi + band] for qi in range(CHUNK)], axis=1)
    tk = n_kblk * tq
    neg = lambda w: jnp.full((heads, CHUNK, w), -1e30, F32)
    blocks = [jnp.concatenate([neg(c * CHUNK), rows, neg(tk - band - c * CHUNK)], axis=2)
              for c in range(tq // CHUNK)]
    base = jnp.concatenate(blocks, axis=1)
    kblk = np.arange(tk)[None, :] // tq
    variants = [jnp.where(kblk + v >= n_kblk - 1, base, -1e30) for v in range(n_kblk)]
    return jnp.stack(variants, axis=0)


def _band_scores(q_ref, k_refs, bias_ref, heads, scale):
    dh = q_ref.shape[1] // heads
    scores = []
    for h in range(heads):
        cols = slice(h * dh, (h + 1) * dh)
        q = (q_ref[:, cols].astype(F32) * scale).astype(BF16)
        k = jnp.concatenate([r[:, cols] for r in k_refs], axis=0).astype(BF16)
        scores.append((cols, _nt_dot(q, k) + bias_ref[h]))
    return scores


def _band_softmax_pv(s, v_refs, cols):
    v = jnp.concatenate([r[:, cols] for r in v_refs], axis=0).astype(BF16)
    dh = v.shape[1]
    v_ones = jnp.concatenate([v, jnp.ones_like(v)], axis=1)
    m = jnp.max(s, axis=-1, keepdims=True)
    p = jnp.exp2(s - m).astype(BF16)
    o_l = jnp.dot(p, v_ones, preferred_element_type=F32)
    return o_l[:, :dh] / o_l[:, dh:dh + 1]


def _conv_attention_outproj_kernel(q_ref, *refs, n_kblk, heads, scale, blocks_per_seq):
    k_refs, v_refs = refs[:n_kblk], refs[n_kblk:2 * n_kblk]
    (bias_ref, b_ref, c_ref, hc_ref, cprev_ref, hprev_ref, cw_ref, w_ref, h_ref, g_ref,
     o_ref, y_ref, u_ref) = refs[2 * n_kblk:]
    step = pl.program_id(0)

    @pl.when(step == 0)
    def _():
        y_ref[...] = jnp.zeros_like(y_ref)

    slot = step % 2
    y = jnp.dot(y_ref[1 - slot], w_ref[...], preferred_element_type=F32)
    o_ref[...] = h_ref[...] + _rms(y, g_ref[...])

    block = jnp.minimum(step, pl.num_programs(0) - 2)
    half = b_ref.shape[1]
    ya = _gated_conv_rows(b_ref, c_ref, hc_ref, cprev_ref, hprev_ref, cw_ref, u_ref, block % blocks_per_seq == 0)
    y_ref[slot, :, 0:half] = ya.astype(BF16)
    for cols, s in _band_scores(q_ref, k_refs, bias_ref, heads, scale):
        o = _band_softmax_pv(s, v_refs, cols)
        y_ref[slot, :, half + cols.start:half + cols.stop] = o.astype(BF16)


def conv_attention_outproj(conv_in, qkv, conv_w, rel_bias, w_out, layer, h, g, seq, *, tq=256):
    t, d = h.shape
    half = conv_w.shape[1]
    heads = rel_bias.shape[0]
    n_kblk = ATT_PAST_CHUNKS * CHUNK // tq + 1
    nq = seq // tq
    nblk = t // tq
    bias = _band_bias(rel_bias, tq, n_kblk)
    blk = lambda s: jnp.minimum(s, nblk - 1)
    prev_blk = lambda s: jnp.maximum(s - 1, 0)

    def kv_spec(col, j):
        def index(s):
            i = blk(s) % nq
            return (blk(s) - i + jnp.maximum(i - (n_kblk - 1) + j, 0), col)
        return pl.BlockSpec((tq, half), index)

    cur = lambda col: pl.BlockSpec((tq, half), lambda s: (blk(s), col))
    prev_rows = lambda col: pl.BlockSpec((SUB, half), lambda s: (jnp.maximum(blk(s) * (tq // SUB) - 1, 0), col))
    in_specs = [cur(0)] + [kv_spec(1, j) for j in range(n_kblk)] + [kv_spec(2, j) for j in range(n_kblk)]
    in_specs += [pl.BlockSpec((None, heads, tq, n_kblk * tq),
                              lambda s: (jnp.minimum(blk(s) % nq, n_kblk - 1), 0, 0, 0)),
                 cur(0), cur(1), cur(2), prev_rows(1), prev_rows(2),
                 pl.BlockSpec((CONV_WIDTH, half), lambda s: (0, 0)),
                 pl.BlockSpec((None, d, d), lambda s: (layer, 0, 0), pipeline_mode=pl.Buffered(1)),
                 pl.BlockSpec((tq, d), lambda s: (prev_blk(s), 0)),
                 pl.BlockSpec((1, d), lambda s: (0, 0))]
    return pl.pallas_call(
        functools.partial(_conv_attention_outproj_kernel, n_kblk=n_kblk, heads=heads,
                          scale=(half // heads) ** -0.5 * LOG2_E, blocks_per_seq=nq),
        grid=(nblk + 1,),
        in_specs=in_specs,
        out_specs=pl.BlockSpec((tq, d), lambda s: (prev_blk(s), 0)),
        out_shape=jax.ShapeDtypeStruct((t, d), F32),
        scratch_shapes=[pltpu.VMEM((2, tq, d), BF16), pltpu.VMEM((tq + SUB, half), F32)],
        compiler_params=_params("arbitrary"),
    )(*([qkv] * (1 + 2 * n_kblk)), bias, *([conv_in] * 5), conv_w, w_out, h, g)


def _outproj_kernel(ya_ref, yb_ref, w_ref, h_ref, g_ref, o_ref):
    y = jnp.concatenate([ya_ref[...], yb_ref[...]], axis=1)
    y = jnp.dot(y, w_ref[...], preferred_element_type=F32)
    o_ref[...] = h_ref[...] + _rms(y, g_ref[...])


def outproj_residual(ya, yb, w, layer, h, g, *, tm=512):
    t, d = h.shape
    row = lambda a: pl.BlockSpec((tm, a.shape[1]), lambda i: (i, 0))
    return pl.pallas_call(
        _outproj_kernel, grid=(t // tm,),
        in_specs=[row(ya), row(yb), pl.BlockSpec((None,) + w.shape[1:], lambda i: (layer, 0, 0)), row(h),
                  pl.BlockSpec((1, d), lambda i: (0, 0))],
        out_specs=row(h),
        out_shape=jax.ShapeDtypeStruct((t, d), F32),
        compiler_params=_params("parallel"),
    )(ya, yb, w, h, g)


def _mlp_kernel(h_ref, g_in_ref, w1_ref, w2_ref, g_out_ref, o_ref, u_ref, *, parts):
    j = pl.program_id(1)
    last = pl.num_programs(1) - 1
    step = o_ref.shape[0] // parts
    row_parts = [slice(p * step, (p + 1) * step) for p in range(parts)]

    def ff_block(u):
        a = jnp.dot(u, w1_ref[...].astype(BF16), preferred_element_type=F32)
        a = jnp.square(jnp.maximum(a, 0.0)).astype(BF16)
        return jnp.dot(a, w2_ref[...].astype(BF16), preferred_element_type=F32)

    @pl.when(j == 0)
    def _():
        for rows in row_parts:
            u = _rms(h_ref[rows, :], g_in_ref[...]).astype(BF16)
            u_ref[rows, :] = u
            o_ref[rows, :] = ff_block(u)

    @pl.when((j > 0) & (j < last))
    def _():
        o_ref[...] += ff_block(u_ref[...])

    @pl.when(j == last)
    def _():
        for rows in row_parts:
            z = o_ref[rows, :] + ff_block(u_ref[rows, :])
            o_ref[rows, :] = h_ref[rows, :] + _rms(z, g_out_ref[...])


def mlp_residual(h, g_in, w1, w2, layer, g_out, *, tm=1024, tf=512, parts=2):
    t, d = h.shape
    ff = w1.shape[2]
    assert ff // tf >= 2, "first and last ff block are handled by different branches"
    vec = pl.BlockSpec((1, d), lambda i, j: (0, 0))
    return pl.pallas_call(
        functools.partial(_mlp_kernel, parts=parts), grid=(t // tm, ff // tf),
        in_specs=[pl.BlockSpec((tm, d), lambda i, j: (i, 0)), vec,
                  pl.BlockSpec((None, d, tf), lambda i, j: (layer, 0, j)),
                  pl.BlockSpec((None, tf, d), lambda i, j: (layer, j, 0)), vec],
        out_specs=pl.BlockSpec((tm, d), lambda i, j: (i, 0)),
        out_shape=jax.ShapeDtypeStruct((t, d), F32),
        scratch_shapes=[pltpu.VMEM((tm, d), BF16)],
        compiler_params=_params("parallel", "arbitrary"),
    )(h, g_in, w1, w2, g_out)


def _split3_bf16(x):
    hi = x.astype(BF16)
    r = x - hi.astype(F32)
    mid = r.astype(BF16)
    lo = (r - mid.astype(F32)).astype(BF16)
    return hi, mid, lo


def _recurrence_operands(q, k, v, log2_a):
    r, dk = q.shape
    nch, nsb = r // CHUNK, r // SUB
    row = lax.broadcasted_iota(jnp.int32, (CHUNK, CHUNK), 0)
    col = lax.broadcasted_iota(jnp.int32, (CHUNK, CHUNK), 1)
    tri = (row >= col).astype(BF16)
    parts = _split3_bf16(log2_a)
    b = jnp.concatenate(
        [sum(jnp.dot(tri, p[c * CHUNK:(c + 1) * CHUNK], preferred_element_type=F32) for p in parts)
         for c in range(nch)], axis=0)

    b3 = b.reshape(nch, CHUNK, dk)
    q3 = q.reshape(nch, CHUNK, dk)
    b_last = b3[:, CHUNK - 1:CHUNK, :]
    qb = (q * jnp.exp2(b)).astype(BF16)
    kb = (k.reshape(nch, CHUNK, dk) * jnp.exp2(b_last - b3)).reshape(r, dk).astype(BF16)
    d_last = jnp.exp2(b_last)

    b4 = b.reshape(nsb, SUB, dk)
    q4 = q.reshape(nsb, SUB, dk)
    k4 = k.reshape(nsb, SUB, dk)

    kt3 = (k4 * jnp.exp2(b4[:, SUB - 1:SUB, :] - b4)).reshape(nch, CHUNK, dk).astype(BF16)
    sub_of_row = lax.broadcasted_iota(jnp.int32, (1, CHUNK, 1), 1) // SUB
    q_slots, k_slots = [], []
    for j in range(N_SUB - 1):
        lo = (j + 1) * SUB
        end_j = b3[:, lo - 1:lo, :]
        qt = (q3[:, lo:, :] * jnp.exp2(b3[:, lo:, :] - end_j)).astype(BF16)
        q_slots.append(jnp.concatenate([jnp.zeros((nch, lo, dk), BF16), qt], axis=1))
        k_slots.append(jnp.where(sub_of_row == j, kt3, jnp.zeros_like(kt3)))
    q_cat = jnp.concatenate(q_slots, axis=2)
    k_cat = jnp.concatenate(k_slots, axis=2)

    c4 = jnp.exp2(log2_a.reshape(nsb, SUB, dk))
    kd = k4
    prods = [(q4 * k4).reshape(r, dk).astype(BF16)]
    for _ in range(1, SUB):
        kd = c4 * pltpu.roll(kd, 1, axis=1)
        prods.append((q4 * kd).reshape(r, dk).astype(BF16))
    slot = lax.broadcasted_iota(jnp.int32, (SUB * dk, LANES), 0) // dk
    lane_w = lax.broadcasted_iota(jnp.int32, (SUB * dk, LANES), 1)
    to_lane = (lane_w == (LANES - slot) % LANES).astype(BF16)
    diag = jnp.dot(jnp.concatenate(prods, axis=1), to_lane, preferred_element_type=F32)
    lane = lax.broadcasted_iota(jnp.int32, (r, LANES), 1)
    t_sub = lax.broadcasted_iota(jnp.int32, (r, LANES), 0) % SUB
    diag = jnp.where((lane == 0) | (lane + t_sub >= LANES), diag, 0.0)
    diag = jnp.concatenate(
        [pltpu.roll(diag[c * CHUNK:(c + 1) * CHUNK], 0, axis=1, stride=1, stride_axis=0) for c in range(nch)], axis=0)

    return dict(q_cat=q_cat, k_cat=k_cat, diag=diag, qb=qb, kb=kb, d_last=d_last, v=v, vb=v.astype(BF16))


def _gated_recurrence_heads(heads, st_ref):
    ops = [_recurrence_operands(*hd) for hd in heads]
    nch = ops[0]["q_cat"].shape[0]
    chunk = lambda c: slice(c * CHUNK, (c + 1) * CHUNK)
    scores = [[_nt_dot(p["q_cat"][c], p["k_cat"][c]) + p["diag"][chunk(c), :CHUNK] for p in ops]
              for c in range(nch)]
    intra = [[jnp.dot(scores[c][i].astype(BF16), p["vb"][chunk(c)], preferred_element_type=F32)
              for i, p in enumerate(ops)] for c in range(nch)]
    upd = [[jnp.dot(p["v"][chunk(c)].T.astype(BF16), p["kb"][chunk(c)], preferred_element_type=F32)
            for p in ops] for c in range(nch)]
    sts = [st_ref[i] for i in range(len(ops))]
    outs = [[] for _ in ops]
    for c in range(nch):
        for i, p in enumerate(ops):
            outs[i].append(intra[c][i] + _nt_dot(p["qb"][chunk(c)], sts[i].astype(BF16)))
            sts[i] = sts[i] * p["d_last"][c] + upd[c][i]
    for i, st in enumerate(sts):
        st_ref[i] = st
    return [jnp.concatenate(o, axis=0) for o in outs]


def _hgrn_kernel(q_ref, f_ref, i_ref, g_ref, lb_ref, ng_ref, o_ref, st_ref, *, layer):
    @pl.when(pl.program_id(2) == 0)
    def _():
        st_ref[...] = jnp.zeros_like(st_ref)

    lb_raw = lb_ref[...]
    e = jnp.exp(lb_raw - jnp.max(lb_raw, axis=0, keepdims=True))
    soft = e / jnp.sum(e, axis=0, keepdims=True)
    lb_all = jnp.sum(soft[:layer + 1], axis=0, keepdims=True) - soft[0:1]

    d = LANES
    head_cols = [slice(hh * d, (hh + 1) * d) for hh in range(st_ref.shape[0])]
    heads = []
    for cols in head_cols:
        lb = lb_all[:, cols]
        q_raw = q_ref[:, cols]
        f = lb + (1.0 - lb) * _sigmoid(f_ref[:, cols])
        heads.append((q_raw * _sigmoid(q_raw), 1.0 - f, i_ref[:, cols], jnp.log2(f)))
    for cols, o in zip(head_cols, _gated_recurrence_heads(heads, st_ref)):
        g_raw = g_ref[:, cols]
        o_ref[:, cols] = (_rms(o, ng_ref[:, cols]) * (g_raw * _sigmoid(g_raw))).astype(o_ref.dtype)


def hgrn2_mixer(proj, lb, norm_g, batch, seq, layer, *, rows=512, heads_per_step=8):
    t = proj.shape[0]
    d = LANES
    w = heads_per_step * d
    groups = HGRN_HEADS // heads_per_step
    nblk = seq // rows
    col = lambda base: pl.BlockSpec((rows, w), lambda b, h, i: (b * nblk + i, base + h))
    return pl.pallas_call(
        functools.partial(_hgrn_kernel, layer=layer),
        grid=(batch, groups, nblk),
        in_specs=[col(0), col(groups), col(2 * groups), col(3 * groups),
                  pl.BlockSpec((lb.shape[0], w), lambda b, h, i: (0, h)),
                  pl.BlockSpec((1, w), lambda b, h, i: (0, h))],
        out_specs=pl.BlockSpec((rows, w), lambda b, h, i: (b * nblk + i, h)),
        out_shape=jax.ShapeDtypeStruct((t, HGRN_HEADS * d), BF16),
        scratch_shapes=[pltpu.VMEM((heads_per_step, d, d), F32)],
        compiler_params=_params("parallel", "parallel", "arbitrary"),
    )(proj, proj, proj, proj, lb, norm_g)


def _gla_kernel(q_ref, k_ref, v_ref, r_ref, a_ref, wa_ref, ba_ref, ng_ref, o_ref, st_ref, *, q_scale):
    @pl.when(pl.program_id(2) == 0)
    def _():
        st_ref[...] = jnp.zeros_like(st_ref)

    x = jnp.dot(a_ref[...].astype(BF16), wa_ref[...], preferred_element_type=F32) + ba_ref[...]
    x2 = x * LOG2_E
    log2_a = (jnp.minimum(x2, 0.0) - jnp.log2(1.0 + jnp.exp2(-jnp.abs(x2)))) / GLA_GATE_NORMALIZER
    nh, dv, dk = st_ref.shape
    k_cols = [slice(hh * dk, (hh + 1) * dk) for hh in range(nh)]
    v_cols = [slice(hh * dv, (hh + 1) * dv) for hh in range(nh)]
    heads = [(q_ref[:, kc] * q_scale, k_ref[:, kc], v_ref[:, vc], log2_a[:, kc]) for kc, vc in zip(k_cols, v_cols)]
    for vc, o in zip(v_cols, _gated_recurrence_heads(heads, st_ref)):
        r_raw = r_ref[:, vc]
        o_ref[:, vc] = (_rms(o, ng_ref[:, vc]) * (r_raw * _sigmoid(r_raw))).astype(o_ref.dtype)


def gla_mixer(proj, a_lr, wa2, ba, norm_g, batch, seq, col0, *, rows=512, heads_per_step=4):
    t = proj.shape[0]
    dk = LANES
    dv = 2 * dk
    wk, wv = heads_per_step * dk, heads_per_step * dv
    groups = GLA_HEADS // heads_per_step
    nblk = seq // rows
    cq, ck = col0 // wk, col0 // wk + groups
    cv = (col0 + 2 * GLA_HEADS * dk) // wv
    cr = cv + groups
    spec = lambda w, base: pl.BlockSpec((rows, w), lambda b, h, i: (b * nblk + i, base + h))
    return pl.pallas_call(
        functools.partial(_gla_kernel, q_scale=dk ** -0.5),
        grid=(batch, groups, nblk),
        in_specs=[spec(wk, cq), spec(wk, ck), spec(wv, cv), spec(wv, cr),
                  pl.BlockSpec((rows, a_lr.shape[1]), lambda b, h, i: (b * nblk + i, 0)),
                  pl.BlockSpec((wa2.shape[0], wk), lambda b, h, i: (0, h)),
                  pl.BlockSpec((1, wk), lambda b, h, i: (0, h)),
                  pl.BlockSpec((1, wv), lambda b, h, i: (0, h))],
        out_specs=pl.BlockSpec((rows, wv), lambda b, h, i: (b * nblk + i, h)),
        out_shape=jax.ShapeDtypeStruct((t, GLA_HEADS * dv), BF16),
        scratch_shapes=[pltpu.VMEM((heads_per_step, dv, dk), F32)],
        compiler_params=_params("parallel", "parallel", "arbitrary"),
    )(proj, proj, proj, proj, a_lr, wa2, ba, norm_g)


def kernel(x, norm_g, even_w_in, even_conv_w, even_rel_bias, even_w_out, odd_w_in, hgrn_lb, hgrn_norm_g,
           gla_wa2, gla_ba, gla_norm_g, odd_w_out, mlp_w1, mlp_w2):
    batch, seq, d = x.shape
    depth = norm_g.shape[0]
    half = d // 2
    even_w_in_b, even_w_out_b = even_w_in.astype(BF16), even_w_out.astype(BF16)
    n_odd_main = odd_w_in.shape[2] - GLA_GATE_RANK
    odd_w_in_b, odd_w_out_b = odd_w_in.astype(BF16), odd_w_out.astype(BF16)
    lane_pad = LANES - GLA_GATE_RANK
    h = x.reshape(batch * seq, d)
    for l in range(depth):
        g = norm_g[l][:, None, :]
        e = l // 2
        if l % 2 == 0:
            conv_in, qkv = rms_matmul(h, g[0], even_w_in_b, e, [(3 * half, F32, 1), (3 * half, BF16, 1)])
            h = conv_attention_outproj(conv_in, qkv, even_conv_w[e], even_rel_bias[e], even_w_out_b, e, h, g[1], seq)
        else:
            w_gate = jnp.pad(odd_w_in[e][:, n_odd_main:], ((0, 0), (0, lane_pad))).astype(BF16)
            proj, a_lr = rms_matmul(h, g[0], odd_w_in_b, e, [(n_odd_main, F32, 2)], w_gate)
            ya = hgrn2_mixer(proj, hgrn_lb, hgrn_norm_g[e][None, :], batch, seq, l)
            wa2 = jnp.pad(gla_wa2[e], ((0, lane_pad), (0, 0))).astype(BF16)
            yb = gla_mixer(proj, a_lr, wa2, gla_ba[e][None, :], gla_norm_g[e][None, :], batch, seq,
                           4 * HGRN_HEADS * LANES)
            h = outproj_residual(ya, yb, odd_w_out_b, e, h, g[1])
        h = mlp_residual(h, g[2], mlp_w1, mlp_w2, l, g[3])
    return h.reshape(batch, seq, d)
```

```python
import functools

import numpy as np
import jax
import jax.numpy as jnp
from jax import lax
from jax.experimental import pallas as pl
from jax.experimental.pallas import tpu as pltpu

F32 = jnp.float32
BF16 = jnp.bfloat16

EPS = 1e-6
LOG2_E = 1.4426950408889634
CHUNK = 64
SUB = 8
LANES = 128
N_SUB = CHUNK // SUB

CONV_WIDTH = 3
ATT_PAST_CHUNKS = 8
REL_CLIP = 256
HGRN_HEADS = 8
GLA_HEADS = 4
GLA_GATE_RANK = 16
GLA_GATE_NORMALIZER = 16.0

VMEM_LIMIT = 60 * 1024 * 1024


def _params(*sem):
    return pltpu.CompilerParams(dimension_semantics=sem, vmem_limit_bytes=VMEM_LIMIT)


def _rms(x, g):
    ms = jnp.mean(x * x, axis=-1, keepdims=True)
    return x * lax.rsqrt(ms + EPS) * g


def _sigmoid(x):
    return 1.0 / (1.0 + jnp.exp(-x))


def _nt_dot(a, b):
    return lax.dot_general(a, b, (((1,), (1,)), ((), ())), preferred_element_type=F32)


def _rms_matmul_kernel(h_ref, g_ref, w_ref, *refs, col_steps, has_extra, w_transposed):
    u_ref = refs[-1]
    wx_ref, ox_ref = (refs[0], refs[-2]) if has_extra else (None, None)
    o_refs = refs[1:-2] if has_extra else refs[:-1]
    j = pl.program_id(1)

    @pl.when(j == 0)
    def _():
        u = _rms(h_ref[...], g_ref[...]).astype(BF16)
        u_ref[...] = u
        if has_extra:
            ox_ref[...] = jnp.dot(u, wx_ref[...], preferred_element_type=F32)

    for jj, (out_idx, col0, tn) in enumerate(col_steps):
        @pl.when(j == jj)
        def _():
            o_ref = o_refs[out_idx]
            if w_transposed:
                y = _nt_dot(u_ref[...], w_ref[col0:col0 + tn, :])
            else:
                y = jnp.dot(u_ref[...], w_ref[:, col0:col0 + tn], preferred_element_type=F32)
            o_ref[...] = y.astype(o_ref.dtype)


def rms_matmul(h, g, w, layer, outs, w_extra=None, *, w_transposed=False, tm=512):
    t, d = h.shape
    resident = pl.Buffered(1)
    in_specs = [pl.BlockSpec((tm, d), lambda i, j: (i, 0)),
                pl.BlockSpec((1, d), lambda i, j: (0, 0)),
                pl.BlockSpec((None,) + w.shape[1:], lambda i, j: (layer, 0, 0), pipeline_mode=resident)]
    args = [h, g, w]
    if w_extra is not None:
        in_specs.append(pl.BlockSpec(w_extra.shape, lambda i, j: (0, 0), pipeline_mode=resident))
        args.append(w_extra)
    out_specs, out_shape, col_steps = [], [], []
    col = 0
    for k, (ncols, dtype, nblk) in enumerate(outs):
        tn = ncols // nblk
        j0 = len(col_steps)
        out_specs.append(pl.BlockSpec(
            (tm, tn), lambda i, j, j0=j0, nblk=nblk: (i, jnp.clip(j - j0, 0, nblk - 1))))
        out_shape.append(jax.ShapeDtypeStruct((t, ncols), dtype))
        col_steps += [(k, col + b * tn, tn) for b in range(nblk)]
        col += ncols
    if w_extra is not None:
        nx = w_extra.shape[1]
        out_specs.append(pl.BlockSpec((tm, nx), lambda i, j: (i, 0)))
        out_shape.append(jax.ShapeDtypeStruct((t, nx), F32))
    return pl.pallas_call(
        functools.partial(_rms_matmul_kernel, col_steps=tuple(col_steps), has_extra=w_extra is not None,
                          w_transposed=w_transposed),
        grid=(t // tm, len(col_steps)),
        in_specs=in_specs, out_specs=out_specs, out_shape=out_shape,
        scratch_shapes=[pltpu.VMEM((tm, d), BF16)],
        compiler_params=_params("parallel", "arbitrary"),
    )(*args)


def _gated_conv_rows(b_ref, c_ref, hc_ref, cprev_ref, hprev_ref, w_ref, u_ref, first):
    tm = b_ref.shape[0]
    prev = cprev_ref[...] * hprev_ref[...]
    u_ref[0:SUB, :] = jnp.where(first, 0.0, prev)
    u_ref[SUB:, :] = c_ref[...] * hc_ref[...]
    w = w_ref[...]
    y = w[0:1] * u_ref[pl.ds(SUB - 2, tm), :]
    for j in range(1, CONV_WIDTH):
        y = y + w[j:j + 1] * u_ref[pl.ds(SUB - 2 + j, tm), :]
    return b_ref[...] * y


def _band_bias(rel_bias, tq, n_kblk):
    band = (ATT_PAST_CHUNKS + 1) * CHUNK
    heads = rel_bias.shape[0]
    assert rel_bias.shape[1] == CHUNK + REL_CLIP, "table covers distances -(CHUNK - 1) .. REL_CLIP"
    rel_bias = rel_bias.astype(F32) * LOG2_E
    n_far = band - REL_CLIP
    ext = jnp.concatenate([jnp.broadcast_to(rel_bias[:, -1:], (heads, n_far)), rel_bias[:, ::-1][:, 1:]], axis=1)
    rows = jnp.stack([ext[:, CHUNK - 1 - qi:CHUNK - 1 - qi + band] for qi in range(CHUNK)], axis=1)
    tk = n_kblk * tq
    neg = lambda w: jnp.full((heads, CHUNK, w), -1e30, F32)
    blocks = [jnp.concatenate([neg(c * CHUNK), rows, neg(tk - band - c * CHUNK)], axis=2)
              for c in range(tq // CHUNK)]
    base = jnp.concatenate(blocks, axis=1)
    kblk = np.arange(tk)[None, :] // tq
    variants = [jnp.where(kblk + v >= n_kblk - 1, base, -1e30) for v in range(n_kblk)]
    return jnp.stack(variants, axis=0)


def _band_scores(q_ref, k_refs, bias_ref, heads, scale):
    dh = q_ref.shape[1] // heads
    scores = []
    for h in range(heads):
        cols = slice(h * dh, (h + 1) * dh)
        q = (q_ref[:, cols].astype(F32) * scale).astype(BF16)
        k = jnp.concatenate([r[:, cols] for r in k_refs], axis=0).astype(BF16)
        scores.append((cols, _nt_dot(q, k) + bias_ref[h]))
    return scores


def _band_softmax_pv(s, v_refs, cols):
    v = jnp.concatenate([r[:, cols] for r in v_refs], axis=0).astype(BF16)
    dh = v.shape[1]
    v_ones = jnp.concatenate([v, jnp.ones_like(v)], axis=1)
    m = jnp.max(s, axis=-1, keepdims=True)
    p = jnp.exp2(s - m).astype(BF16)
    o_l = jnp.dot(p, v_ones, preferred_element_type=F32)
    return o_l[:, :dh] / o_l[:, dh:dh + 1]


def _conv_attention_outproj_kernel(q_ref, *refs, n_kblk, heads, scale, blocks_per_seq):
    k_refs, v_refs = refs[:n_kblk], refs[n_kblk:2 * n_kblk]
    (bias_ref, b_ref, c_ref, hc_ref, cprev_ref, hprev_ref, cw_ref, w_ref, h_ref, g_ref,
     o_ref, y_ref, u_ref) = refs[2 * n_kblk:]
    step = pl.program_id(0)

    @pl.when(step == 0)
    def _():
        y_ref[...] = jnp.zeros_like(y_ref)

    slot = step % 2
    y = jnp.dot(y_ref[1 - slot], w_ref[...], preferred_element_type=F32)
    o_ref[...] = h_ref[...] + _rms(y, g_ref[...])

    block = jnp.minimum(step, pl.num_programs(0) - 2)
    half = b_ref.shape[1]
    ya = _gated_conv_rows(b_ref, c_ref, hc_ref, cprev_ref, hprev_ref, cw_ref, u_ref, block % blocks_per_seq == 0)
    y_ref[slot, :, 0:half] = ya.astype(BF16)
    for cols, s in _band_scores(q_ref, k_refs, bias_ref, heads, scale):
        o = _band_softmax_pv(s, v_refs, cols)
        y_ref[slot, :, half + cols.start:half + cols.stop] = o.astype(BF16)


def conv_attention_outproj(conv_in, qkv, conv_w, rel_bias, w_out, layer, h, g, seq, *, tq=256):
    t, d = h.shape
    half = conv_w.shape[1]
    heads = rel_bias.shape[0]
    n_kblk = ATT_PAST_CHUNKS * CHUNK // tq + 1
    nq = seq // tq
    nblk = t // tq
    bias = _band_bias(rel_bias, tq, n_kblk)
    blk = lambda s: jnp.minimum(s, nblk - 1)
    prev_blk = lambda s: jnp.maximum(s - 1, 0)

    def kv_spec(col, j):
        def index(s):
            i = blk(s) % nq
            return (blk(s) - i + jnp.maximum(i - (n_kblk - 1) + j, 0), col)
        return pl.BlockSpec((tq, half), index)

    cur = lambda col: pl.BlockSpec((tq, half), lambda s: (blk(s), col))
    prev_rows = lambda col: pl.BlockSpec((SUB, half), lambda s: (jnp.maximum(blk(s) * (tq // SUB) - 1, 0), col))
    in_specs = [cur(0)] + [kv_spec(1, j) for j in range(n_kblk)] + [kv_spec(2, j) for j in range(n_kblk)]
    in_specs += [pl.BlockSpec((None, heads, tq, n_kblk * tq),
                              lambda s: (jnp.minimum(blk(s) % nq, n_kblk - 1), 0, 0, 0)),
                 cur(0), cur(1), cur(2), prev_rows(1), prev_rows(2),
                 pl.BlockSpec((CONV_WIDTH, half), lambda s: (0, 0)),
                 pl.BlockSpec((None, d, d), lambda s: (layer, 0, 0), pipeline_mode=pl.Buffered(1)),
                 pl.BlockSpec((tq, d), lambda s: (prev_blk(s), 0)),
                 pl.BlockSpec((1, d), lambda s: (0, 0))]
    return pl.pallas_call(
        functools.partial(_conv_attention_outproj_kernel, n_kblk=n_kblk, heads=heads,
                          scale=(half // heads) ** -0.5 * LOG2_E, blocks_per_seq=nq),
        grid=(nblk + 1,),
        in_specs=in_specs,
        out_specs=pl.BlockSpec((tq, d), lambda s: (prev_blk(s), 0)),
        out_shape=jax.ShapeDtypeStruct((t, d), F32),
        scratch_shapes=[pltpu.VMEM((2, tq, d), BF16), pltpu.VMEM((tq + SUB, half), F32)],
        compiler_params=_params("arbitrary"),
    )(*([qkv] * (1 + 2 * n_kblk)), bias, *([conv_in] * 5), conv_w, w_out, h, g)


def _outproj_kernel(ya_ref, yb_ref, w_ref, h_ref, g_ref, o_ref):
    y = jnp.concatenate([ya_ref[...], yb_ref[...]], axis=1)
    y = jnp.dot(y, w_ref[...], preferred_element_type=F32)
    o_ref[...] = h_ref[...] + _rms(y, g_ref[...])


def outproj_residual(ya, yb, w, layer, h, g, *, tm=512):
    t, d = h.shape
    row = lambda a: pl.BlockSpec((tm, a.shape[1]), lambda i: (i, 0))
    return pl.pallas_call(
        _outproj_kernel, grid=(t // tm,),
        in_specs=[row(ya), row(yb), pl.BlockSpec((None,) + w.shape[1:], lambda i: (layer, 0, 0)), row(h),
                  pl.BlockSpec((1, d), lambda i: (0, 0))],
        out_specs=row(h),
        out_shape=jax.ShapeDtypeStruct((t, d), F32),
        compiler_params=_params("parallel"),
    )(ya, yb, w, h, g)


def _mlp_kernel(h_ref, g_in_ref, w1_ref, w2_ref, g_out_ref, o_ref, u_ref, *, parts):
    j = pl.program_id(1)
    last = pl.num_programs(1) - 1
    step = o_ref.shape[0] // parts
    row_parts = [slice(p * step, (p + 1) * step) for p in range(parts)]

    def ff_block(u):
        a = jnp.dot(u, w1_ref[...].astype(BF16), preferred_element_type=F32)
        a = jnp.square(jnp.maximum(a, 0.0)).astype(BF16)
        return jnp.dot(a, w2_ref[...].astype(BF16), preferred_element_type=F32)

    @pl.when(j == 0)
    def _():
        for rows in row_parts:
            u = _rms(h_ref[rows, :], g_in_ref[...]).astype(BF16)
            u_ref[rows, :] = u
            o_ref[rows, :] = ff_block(u)

    @pl.when((j > 0) & (j < last))
    def _():
        o_ref[...] += ff_block(u_ref[...])

    @pl.when(j == last)
    def _():
        for rows in row_parts:
            z = o_ref[rows, :] + ff_block(u_ref[rows, :])
            o_ref[rows, :] = h_ref[rows, :] + _rms(z, g_out_ref[...])


def mlp_residual(h, g_in, w1, w2, layer, g_out, *, tm=1024, tf=512, parts=2):
    t, d = h.shape
    ff = w1.shape[2]
    assert ff // tf >= 2, "first and last ff block are handled by different branches"
    vec = pl.BlockSpec((1, d), lambda i, j: (0, 0))
    return pl.pallas_call(
        functools.partial(_mlp_kernel, parts=parts), grid=(t // tm, ff // tf),
        in_specs=[pl.BlockSpec((tm, d), lambda i, j: (i, 0)), vec,
                  pl.BlockSpec((None, d, tf), lambda i, j: (layer, 0, j)),
                  pl.BlockSpec((None, tf, d), lambda i, j: (layer, j, 0)), vec],
        out_specs=pl.BlockSpec((tm, d), lambda i, j: (i, 0)),
        out_shape=jax.ShapeDtypeStruct((t, d), F32),
        scratch_shapes=[pltpu.VMEM((tm, d), BF16)],
        compiler_params=_params("parallel", "arbitrary"),
    )(h, g_in, w1, w2, g_out)


def _split3_bf16(x):
    hi = x.astype(BF16)
    r = x - hi.astype(F32)
    mid = r.astype(BF16)
    lo = (r - mid.astype(F32)).astype(BF16)
    return hi, mid, lo


def _recurrence_operands(q, k, v, log2_a):
    r, dk = q.shape
    nch, nsb = r // CHUNK, r // SUB
    row = lax.broadcasted_iota(jnp.int32, (CHUNK, CHUNK), 0)
    col = lax.broadcasted_iota(jnp.int32, (CHUNK, CHUNK), 1)
    tri = (row >= col).astype(BF16)
    parts = _split3_bf16(log2_a)
    b = jnp.concatenate(
        [sum(jnp.dot(tri, p[c * CHUNK:(c + 1) * CHUNK], preferred_element_type=F32) for p in parts)
         for c in range(nch)], axis=0)

    b3 = b.reshape(nch, CHUNK, dk)
    q3 = q.reshape(nch, CHUNK, dk)
    b_last = b3[:, CHUNK - 1:CHUNK, :]
    qb = (q * jnp.exp2(b)).astype(BF16)
    kb = (k.reshape(nch, CHUNK, dk) * jnp.exp2(b_last - b3)).reshape(r, dk).astype(BF16)
    d_last = jnp.exp2(b_last)

    b4 = b.reshape(nsb, SUB, dk)
    q4 = q.reshape(nsb, SUB, dk)
    k4 = k.reshape(nsb, SUB, dk)

    kt3 = (k4 * jnp.exp2(b4[:, SUB - 1:SUB, :] - b4)).reshape(nch, CHUNK, dk).astype(BF16)
    sub_of_row = lax.broadcasted_iota(jnp.int32, (1, CHUNK, 1), 1) // SUB
    q_slots, k_slots = [], []
    for j in range(N_SUB - 1):
        lo = (j + 1) * SUB
        end_j = b3[:, lo - 1:lo, :]
        qt = (q3[:, lo:, :] * jnp.exp2(b3[:, lo:, :] - end_j)).astype(BF16)
        q_slots.append(jnp.concatenate([jnp.zeros((nch, lo, dk), BF16), qt], axis=1))
        k_slots.append(jnp.where(sub_of_row == j, kt3, jnp.zeros_like(kt3)))
    q_cat = jnp.concatenate(q_slots, axis=2)
    k_cat = jnp.concatenate(k_slots, axis=2)

    c4 = jnp.exp2(log2_a.reshape(nsb, SUB, dk))
    kd = k4
    prods = [(q4 * k4).reshape(r, dk).astype(BF16)]
    for _ in range(1, SUB):
        kd = c4 * pltpu.roll(kd, 1, axis=1)
        prods.append((q4 * kd).reshape(r, dk).astype(BF16))
    slot = lax.broadcasted_iota(jnp.int32, (SUB * dk, LANES), 0) // dk
    lane_w = lax.broadcasted_iota(jnp.int32, (SUB * dk, LANES), 1)
    to_lane = (lane_w == (LANES - slot) % LANES).astype(BF16)
    diag = jnp.dot(jnp.concatenate(prods, axis=1), to_lane, preferred_element_type=F32)
    lane = lax.broadcasted_iota(jnp.int32, (r, LANES), 1)
    t_sub = lax.broadcasted_iota(jnp.int32, (r, LANES), 0) % SUB
    diag = jnp.where((lane == 0) | (lane + t_sub >= LANES), diag, 0.0)
    diag = jnp.concatenate(
        [pltpu.roll(diag[c * CHUNK:(c + 1) * CHUNK], 0, axis=1, stride=1, stride_axis=0) for c in range(nch)], axis=0)

    return dict(q_cat=q_cat, k_cat=k_cat, diag=diag, qb=qb, kb=kb, d_last=d_last, v=v, vb=v.astype(BF16))


def _gated_recurrence_heads(heads, st_ref):
    ops = [_recurrence_operands(*hd) for hd in heads]
    nch = ops[0]["q_cat"].shape[0]
    chunk = lambda c: slice(c * CHUNK, (c + 1) * CHUNK)
    scores = [[_nt_dot(p["q_cat"][c], p["k_cat"][c]) + p["diag"][chunk(c), :CHUNK] for p in ops]
              for c in range(nch)]
    intra = [[jnp.dot(scores[c][i].astype(BF16), p["vb"][chunk(c)], preferred_element_type=F32)
              for i, p in enumerate(ops)] for c in range(nch)]
    upd = [[jnp.dot(p["v"][chunk(c)].T.astype(BF16), p["kb"][chunk(c)], preferred_element_type=F32)
            for p in ops] for c in range(nch)]
    sts = [st_ref[i] for i in range(len(ops))]
    outs = [[] for _ in ops]
    for c in range(nch):
        for i, p in enumerate(ops):
            outs[i].append(intra[c][i] + _nt_dot(p["qb"][chunk(c)], sts[i].astype(BF16)))
            sts[i] = sts[i] * p["d_last"][c] + upd[c][i]
    for i, st in enumerate(sts):
        st_ref[i] = st
    return [jnp.concatenate(o, axis=0) for o in outs]


def _hgrn_kernel(q_ref, f_ref, i_ref, g_ref, lb_ref, ng_ref, o_ref, st_ref, *, layer):
    @pl.when(pl.program_id(2) == 0)
    def _():
        st_ref[...] = jnp.zeros_like(st_ref)

    lb_raw = lb_ref[...]
    e = jnp.exp(lb_raw - jnp.max(lb_raw, axis=0, keepdims=True))
    soft = e / jnp.sum(e, axis=0, keepdims=True)
    lb_all = jnp.sum(soft[:layer + 1], axis=0, keepdims=True) - soft[0:1]

    d = LANES
    head_cols = [slice(hh * d, (hh + 1) * d) for hh in range(st_ref.shape[0])]
    heads = []
    for cols in head_cols:
        lb = lb_all[:, cols]
        q_raw = q_ref[:, cols]
        f = lb + (1.0 - lb) * _sigmoid(f_ref[:, cols])
        heads.append((q_raw * _sigmoid(q_raw), 1.0 - f, i_ref[:, cols], jnp.log2(f)))
    for cols, o in zip(head_cols, _gated_recurrence_heads(heads, st_ref)):
        g_raw = g_ref[:, cols]
        o_ref[:, cols] = (_rms(o, ng_ref[:, cols]) * (g_raw * _sigmoid(g_raw))).astype(o_ref.dtype)


def hgrn2_mixer(proj, lb, norm_g, batch, seq, layer, *, rows=512, heads_per_step=8):
    t = proj.shape[0]
    d = LANES
    w = heads_per_step * d
    groups = HGRN_HEADS // heads_per_step
    nblk = seq // rows
    col = lambda base: pl.BlockSpec((rows, w), lambda b, h, i: (b * nblk + i, base + h))
    return pl.pallas_call(
        functools.partial(_hgrn_kernel, layer=layer),
        grid=(batch, groups, nblk),
        in_specs=[col(0), col(groups), col(2 * groups), col(3 * groups),
                  pl.BlockSpec((lb.shape[0], w), lambda b, h, i: (0, h)),
                  pl.BlockSpec((1, w), lambda b, h, i: (0, h))],
        out_specs=pl.BlockSpec((rows, w), lambda b, h, i: (b * nblk + i, h)),
        out_shape=jax.ShapeDtypeStruct((t, HGRN_HEADS * d), BF16),
        scratch_shapes=[pltpu.VMEM((heads_per_step, d, d), F32)],
        compiler_params=_params("parallel", "parallel", "arbitrary"),
    )(proj, proj, proj, proj, lb, norm_g)


def _gla_kernel(q_ref, k_ref, v_ref, r_ref, a_ref, wa_ref, ba_ref, ng_ref, o_ref, st_ref, *, q_scale):
    @pl.when(pl.program_id(2) == 0)
    def _():
        st_ref[...] = jnp.zeros_like(st_ref)

    x = jnp.dot(a_ref[...].astype(BF16), wa_ref[...], preferred_element_type=F32) + ba_ref[...]
    x2 = x * LOG2_E
    log2_a = (jnp.minimum(x2, 0.0) - jnp.log2(1.0 + jnp.exp2(-jnp.abs(x2)))) / GLA_GATE_NORMALIZER
    nh, dv, dk = st_ref.shape
    k_cols = [slice(hh * dk, (hh + 1) * dk) for hh in range(nh)]
    v_cols = [slice(hh * dv, (hh + 1) * dv) for hh in range(nh)]
    heads = [(q_ref[:, kc] * q_scale, k_ref[:, kc], v_ref[:, vc], log2_a[:, kc]) for kc, vc in zip(k_cols, v_cols)]
    for vc, o in zip(v_cols, _gated_recurrence_heads(heads, st_ref)):
        r_raw = r_ref[:, vc]
        o_ref[:, vc] = (_rms(o, ng_ref[:, vc]) * (r_raw * _sigmoid(r_raw))).astype(o_ref.dtype)


def gla_mixer(proj, a_lr, wa2, ba, norm_g, batch, seq, col0, *, rows=512, heads_per_step=4):
    t = proj.shape[0]
    dk = LANES
    dv = 2 * dk
    wk, wv = heads_per_step * dk, heads_per_step * dv
    groups = GLA_HEADS // heads_per_step
    nblk = seq // rows
    cq, ck = col0 // wk, col0 // wk + groups
    cv = (col0 + 2 * GLA_HEADS * dk) // wv
    cr = cv + groups
    spec = lambda w, base: pl.BlockSpec((rows, w), lambda b, h, i: (b * nblk + i, base + h))
    return pl.pallas_call(
        functools.partial(_gla_kernel, q_scale=dk ** -0.5),
        grid=(batch, groups, nblk),
        in_specs=[spec(wk, cq), spec(wk, ck), spec(wv, cv), spec(wv, cr),
                  pl.BlockSpec((rows, a_lr.shape[1]), lambda b, h, i: (b * nblk + i, 0)),
                  pl.BlockSpec((wa2.shape[0], wk), lambda b, h, i: (0, h)),
                  pl.BlockSpec((1, wk), lambda b, h, i: (0, h)),
                  pl.BlockSpec((1, wv), lambda b, h, i: (0, h))],
        out_specs=pl.BlockSpec((rows, wv), lambda b, h, i: (b * nblk + i, h)),
        out_shape=jax.ShapeDtypeStruct((t, GLA_HEADS * dv), BF16),
        scratch_shapes=[pltpu.VMEM((heads_per_step, dv, dk), F32)],
        compiler_params=_params("parallel", "parallel", "arbitrary"),
    )(proj, proj, proj, proj, a_lr, wa2, ba, norm_g)


def kernel(x, norm_g, even_w_in, even_conv_w, even_rel_bias, even_w_out, odd_w_in, hgrn_lb, hgrn_norm_g,
           gla_wa2, gla_ba, gla_norm_g, odd_w_out, mlp_w1, mlp_w2):
    batch, seq, d = x.shape
    depth = norm_g.shape[0]
    half = d // 2
    even_w_in_b, even_w_out_b = even_w_in.astype(BF16), even_w_out.astype(BF16)
    n_odd_main = odd_w_in.shape[2] - GLA_GATE_RANK
    odd_w_in_b, odd_w_out_b = jnp.swapaxes(odd_w_in, 1, 2).astype(BF16), odd_w_out.astype(BF16)
    lane_pad = LANES - GLA_GATE_RANK
    h = x.reshape(batch * seq, d)
    for l in range(depth):
        g = norm_g[l][:, None, :]
        e = l // 2
        if l % 2 == 0:
            conv_in, qkv = rms_matmul(h, g[0], even_w_in_b, e, [(3 * half, F32, 1), (3 * half, BF16, 1)])
            h = conv_attention_outproj(conv_in, qkv, even_conv_w[e], even_rel_bias[e], even_w_out_b, e, h, g[1], seq)
        else:
            w_gate = jnp.pad(odd_w_in[e][:, n_odd_main:], ((0, 0), (0, lane_pad))).astype(BF16)
            proj, a_lr = rms_matmul(h, g[0], odd_w_in_b, e, [(n_odd_main, F32, 2)], w_gate, w_transposed=True)
            ya = hgrn2_mixer(proj, hgrn_lb, hgrn_norm_g[e][None, :], batch, seq, l)
            wa2 = jnp.pad(gla_wa2[e], ((0, lane_pad), (0, 0))).astype(BF16)
            yb = gla_mixer(proj, a_lr, wa2, gla_ba[e][None, :], gla_norm_g[e][None, :], batch, seq,
                           4 * HGRN_HEADS * LANES)
            h = outproj_residual(ya, yb, odd_w_out_b, e, h, g[1])
        h = mlp_residual(h, g[2], mlp_w1, mlp_w2, l, g[3])
    return h.reshape(batch, seq, d)
```

```python
import functools

import numpy as np
import jax
import jax.numpy as jnp
from jax import lax
from jax.experimental import pallas as pl
from jax.experimental.pallas import tpu as pltpu

F32 = jnp.float32
BF16 = jnp.bfloat16

EPS = 1e-6
LOG2_E = 1.4426950408889634
CHUNK = 64
SUB = 8
LANES = 128
N_SUB = CHUNK // SUB

CONV_WIDTH = 3
ATT_PAST_CHUNKS = 8
REL_CLIP = 256
HGRN_HEADS = 8
GLA_HEADS = 4
GLA_GATE_RANK = 16
GLA_GATE_NORMALIZER = 16.0

VMEM_LIMIT = 60 * 1024 * 1024


def _params(*sem):
    return pltpu.CompilerParams(dimension_semantics=sem, vmem_limit_bytes=VMEM_LIMIT)


def _rms(x, g):
    ms = jnp.mean(x * x, axis=-1, keepdims=True)
    return x * lax.rsqrt(ms + EPS) * g


def _sigmoid(x):
    return 1.0 / (1.0 + jnp.exp(-x))


def _nt_dot(a, b):
    return lax.dot_general(a, b, (((1,), (1,)), ((), ())), preferred_element_type=F32)


def _rms_matmul_kernel(h_ref, g_ref, w_ref, *refs, col_steps, has_extra, row_parts):
    if row_parts:
        step = h_ref.shape[0] // row_parts
        for p in range(row_parts):
            rows = slice(p * step, (p + 1) * step)
            u = _rms(h_ref[rows, :], g_ref[...]).astype(BF16)
            for out_idx, col0, tn in col_steps:
                y = jnp.dot(u, w_ref[:, col0:col0 + tn], preferred_element_type=F32)
                refs[out_idx][rows, :] = y.astype(refs[out_idx].dtype)
        return
    u_ref = refs[-1]
    wx_ref, ox_ref = (refs[0], refs[-2]) if has_extra else (None, None)
    o_refs = refs[1:-2] if has_extra else refs[:-1]
    j = pl.program_id(1)

    @pl.when(j == 0)
    def _():
        u = _rms(h_ref[...], g_ref[...]).astype(BF16)
        u_ref[...] = u
        if has_extra:
            ox_ref[...] = jnp.dot(u, wx_ref[...], preferred_element_type=F32)

    for jj, (out_idx, col0, tn) in enumerate(col_steps):
        @pl.when(j == jj)
        def _():
            o_ref = o_refs[out_idx]
            y = jnp.dot(u_ref[...], w_ref[:, col0:col0 + tn], preferred_element_type=F32)
            o_ref[...] = y.astype(o_ref.dtype)


def rms_matmul(h, g, w, layer, outs, w_extra=None, *, tm=512):
    t, d = h.shape
    resident = pl.Buffered(1)
    in_specs = [pl.BlockSpec((tm, d), lambda i, j: (i, 0)),
                pl.BlockSpec((1, d), lambda i, j: (0, 0)),
                pl.BlockSpec((None, d, w.shape[2]), lambda i, j: (layer, 0, 0), pipeline_mode=resident)]
    args = [h, g, w]
    if w_extra is not None:
        in_specs.append(pl.BlockSpec(w_extra.shape, lambda i, j: (0, 0), pipeline_mode=resident))
        args.append(w_extra)
    out_specs, out_shape, col_steps = [], [], []
    col = 0
    for k, (ncols, dtype, nblk) in enumerate(outs):
        tn = ncols // nblk
        j0 = len(col_steps)
        out_specs.append(pl.BlockSpec(
            (tm, tn), lambda i, j, j0=j0, nblk=nblk: (i, jnp.clip(j - j0, 0, nblk - 1))))
        out_shape.append(jax.ShapeDtypeStruct((t, ncols), dtype))
        col_steps += [(k, col + b * tn, tn) for b in range(nblk)]
        col += ncols
    if w_extra is not None:
        nx = w_extra.shape[1]
        out_specs.append(pl.BlockSpec((tm, nx), lambda i, j: (i, 0)))
        out_shape.append(jax.ShapeDtypeStruct((t, nx), F32))
    one_step = w_extra is None and all(nblk == 1 for _, _, nblk in outs)
    return pl.pallas_call(
        functools.partial(_rms_matmul_kernel, col_steps=tuple(col_steps), has_extra=w_extra is not None,
                          row_parts=2 if one_step else 0),
        grid=(t // tm, 1 if one_step else len(col_steps)),
        in_specs=in_specs, out_specs=out_specs, out_shape=out_shape,
        scratch_shapes=[] if one_step else [pltpu.VMEM((tm, d), BF16)],
        compiler_params=_params("parallel", "arbitrary"),
    )(*args)


def _gated_conv_rows(b_ref, c_ref, hc_ref, cprev_ref, hprev_ref, w_ref, u_ref, first):
    tm = b_ref.shape[0]
    prev = cprev_ref[...] * hprev_ref[...]
    u_ref[0:SUB, :] = jnp.where(first, 0.0, prev)
    u_ref[SUB:, :] = c_ref[...] * hc_ref[...]
    w = w_ref[...]
    y = w[0:1] * u_ref[pl.ds(SUB - 2, tm), :]
    for j in range(1, CONV_WIDTH):
        y = y + w[j:j + 1] * u_ref[pl.ds(SUB - 2 + j, tm), :]
    return b_ref[...] * y


def _band_bias(rel_bias, tq, n_kblk):
    band = (ATT_PAST_CHUNKS + 1) * CHUNK
    heads = rel_bias.shape[0]
    assert rel_bias.shape[1] == CHUNK + REL_CLIP, "table covers distances -(CHUNK - 1) .. REL_CLIP"
    rel_bias = rel_bias.astype(F32) * LOG2_E
    n_far = band - REL_CLIP
    ext = jnp.concatenate([jnp.broadcast_to(rel_bias[:, -1:], (heads, n_far)), rel_bias[:, ::-1][:, 1:]], axis=1)
    rows = jnp.stack([ext[:, CHUNK - 1 - qi:CHUNK - 1 - qi + band] for qi in range(CHUNK)], axis=1)
    tk = n_kblk * tq
    neg = lambda w: jnp.full((heads, CHUNK, w), -1e30, F32)
    blocks = [jnp.concatenate([neg(c * CHUNK), rows, neg(tk - band - c * CHUNK)], axis=2)
              for c in range(tq // CHUNK)]
    base = jnp.concatenate(blocks, axis=1)
    kblk = np.arange(tk)[None, :] // tq
    variants = [jnp.where(kblk + v >= n_kblk - 1, base, -1e30) for v in range(n_kblk)]
    return jnp.stack(variants, axis=0)


def _band_scores(q_ref, k_refs, bias_ref, heads, scale):
    dh = q_ref.shape[1] // heads
    scores = []
    for h in range(heads):
        cols = slice(h * dh, (h + 1) * dh)
        q = (q_ref[:, cols].astype(F32) * scale).astype(BF16)
        k = jnp.concatenate([r[:, cols] for r in k_refs], axis=0).astype(BF16)
        scores.append((cols, _nt_dot(q, k) + bias_ref[h]))
    return scores


def _band_softmax_pv(s, v_refs, cols):
    v = jnp.concatenate([r[:, cols] for r in v_refs], axis=0).astype(BF16)
    dh = v.shape[1]
    v_ones = jnp.concatenate([v, jnp.ones_like(v)], axis=1)
    m = jnp.max(s, axis=-1, keepdims=True)
    p = jnp.exp2(s - m).astype(BF16)
    o_l = jnp.dot(p, v_ones, preferred_element_type=F32)
    return o_l[:, :dh] / o_l[:, dh:dh + 1]


def _conv_attention_outproj_kernel(q_ref, *refs, n_kblk, heads, scale, blocks_per_seq):
    k_refs, v_refs = refs[:n_kblk], refs[n_kblk:2 * n_kblk]
    (bias_ref, b_ref, c_ref, hc_ref, cprev_ref, hprev_ref, cw_ref, w_ref, h_ref, g_ref,
     o_ref, y_ref, u_ref) = refs[2 * n_kblk:]
    step = pl.program_id(0)

    @pl.when(step == 0)
    def _():
        y_ref[...] = jnp.zeros_like(y_ref)

    slot = step % 2
    y = jnp.dot(y_ref[1 - slot], w_ref[...], preferred_element_type=F32)
    o_ref[...] = h_ref[...] + _rms(y, g_ref[...])

    block = jnp.minimum(step, pl.num_programs(0) - 2)
    half = b_ref.shape[1]
    ya = _gated_conv_rows(b_ref, c_ref, hc_ref, cprev_ref, hprev_ref, cw_ref, u_ref, block % blocks_per_seq == 0)
    y_ref[slot, :, 0:half] = ya.astype(BF16)
    for cols, s in _band_scores(q_ref, k_refs, bias_ref, heads, scale):
        o = _band_softmax_pv(s, v_refs, cols)
        y_ref[slot, :, half + cols.start:half + cols.stop] = o.astype(BF16)


def conv_attention_outproj(conv_in, qkv, conv_w, rel_bias, w_out, layer, h, g, seq, *, tq=256):
    t, d = h.shape
    half = conv_w.shape[1]
    heads = rel_bias.shape[0]
    n_kblk = ATT_PAST_CHUNKS * CHUNK // tq + 1
    nq = seq // tq
    nblk = t // tq
    bias = _band_bias(rel_bias, tq, n_kblk)
    blk = lambda s: jnp.minimum(s, nblk - 1)
    prev_blk = lambda s: jnp.maximum(s - 1, 0)

    def kv_spec(col, j):
        def index(s):
            i = blk(s) % nq
            return (blk(s) - i + jnp.maximum(i - (n_kblk - 1) + j, 0), col)
        return pl.BlockSpec((tq, half), index)

    cur = lambda col: pl.BlockSpec((tq, half), lambda s: (blk(s), col))
    prev_rows = lambda col: pl.BlockSpec((SUB, half), lambda s: (jnp.maximum(blk(s) * (tq // SUB) - 1, 0), col))
    in_specs = [cur(0)] + [kv_spec(1, j) for j in range(n_kblk)] + [kv_spec(2, j) for j in range(n_kblk)]
    in_specs += [pl.BlockSpec((None, heads, tq, n_kblk * tq),
                              lambda s: (jnp.minimum(blk(s) % nq, n_kblk - 1), 0, 0, 0)),
                 cur(0), cur(1), cur(2), prev_rows(1), prev_rows(2),
                 pl.BlockSpec((CONV_WIDTH, half), lambda s: (0, 0)),
                 pl.BlockSpec((None, d, d), lambda s: (layer, 0, 0), pipeline_mode=pl.Buffered(1)),
                 pl.BlockSpec((tq, d), lambda s: (prev_blk(s), 0)),
                 pl.BlockSpec((1, d), lambda s: (0, 0))]
    return pl.pallas_call(
        functools.partial(_conv_attention_outproj_kernel, n_kblk=n_kblk, heads=heads,
                          scale=(half // heads) ** -0.5 * LOG2_E, blocks_per_seq=nq),
        grid=(nblk + 1,),
        in_specs=in_specs,
        out_specs=pl.BlockSpec((tq, d), lambda s: (prev_blk(s), 0)),
        out_shape=jax.ShapeDtypeStruct((t, d), F32),
        scratch_shapes=[pltpu.VMEM((2, tq, d), BF16), pltpu.VMEM((tq + SUB, half), F32)],
        compiler_params=_params("arbitrary"),
    )(*([qkv] * (1 + 2 * n_kblk)), bias, *([conv_in] * 5), conv_w, w_out, h, g)


def _outproj_kernel(ya_ref, yb_ref, w_ref, h_ref, g_ref, o_ref):
    y = jnp.concatenate([ya_ref[...], yb_ref[...]], axis=1)
    y = jnp.dot(y, w_ref[...], preferred_element_type=F32)
    o_ref[...] = h_ref[...] + _rms(y, g_ref[...])


def outproj_residual(ya, yb, w, layer, h, g, *, tm=512):
    t, d = h.shape
    row = lambda a: pl.BlockSpec((tm, a.shape[1]), lambda i: (i, 0))
    return pl.pallas_call(
        _outproj_kernel, grid=(t // tm,),
        in_specs=[row(ya), row(yb), pl.BlockSpec((None,) + w.shape[1:], lambda i: (layer, 0, 0)), row(h),
                  pl.BlockSpec((1, d), lambda i: (0, 0))],
        out_specs=row(h),
        out_shape=jax.ShapeDtypeStruct((t, d), F32),
        compiler_params=_params("parallel"),
    )(ya, yb, w, h, g)


def _mlp_kernel(h_ref, g_in_ref, w1_ref, w2_ref, g_out_ref, o_ref, u_ref, *, parts):
    j = pl.program_id(1)
    last = pl.num_programs(1) - 1
    step = o_ref.shape[0] // parts
    row_parts = [slice(p * step, (p + 1) * step) for p in range(parts)]

    def ff_block(u):
        a = jnp.dot(u, w1_ref[...].astype(BF16), preferred_element_type=F32)
        a = jnp.square(jnp.maximum(a, 0.0)).astype(BF16)
        return jnp.dot(a, w2_ref[...].astype(BF16), preferred_element_type=F32)

    @pl.when(j == 0)
    def _():
        for rows in row_parts:
            u = _rms(h_ref[rows, :], g_in_ref[...]).astype(BF16)
            u_ref[rows, :] = u
            o_ref[rows, :] = ff_block(u)

    @pl.when((j > 0) & (j < last))
    def _():
        o_ref[...] += ff_block(u_ref[...])

    @pl.when(j == last)
    def _():
        for rows in row_parts:
            z = o_ref[rows, :] + ff_block(u_ref[rows, :])
            o_ref[rows, :] = h_ref[rows, :] + _rms(z, g_out_ref[...])


def mlp_residual(h, g_in, w1, w2, layer, g_out, *, tm=1024, tf=512, parts=2):
    t, d = h.shape
    ff = w1.shape[2]
    assert ff // tf >= 2, "first and last ff block are handled by different branches"
    vec = pl.BlockSpec((1, d), lambda i, j: (0, 0))
    return pl.pallas_call(
        functools.partial(_mlp_kernel, parts=parts), grid=(t // tm, ff // tf),
        in_specs=[pl.BlockSpec((tm, d), lambda i, j: (i, 0)), vec,
                  pl.BlockSpec((None, d, tf), lambda i, j: (layer, 0, j)),
                  pl.BlockSpec((None, tf, d), lambda i, j: (layer, j, 0)), vec],
        out_specs=pl.BlockSpec((tm, d), lambda i, j: (i, 0)),
        out_shape=jax.ShapeDtypeStruct((t, d), F32),
        scratch_shapes=[pltpu.VMEM((tm, d), BF16)],
        compiler_params=_params("parallel", "arbitrary"),
    )(h, g_in, w1, w2, g_out)


def _split3_bf16(x):
    hi = x.astype(BF16)
    r = x - hi.astype(F32)
    mid = r.astype(BF16)
    lo = (r - mid.astype(F32)).astype(BF16)
    return hi, mid, lo


def _recurrence_operands(q, k, v, log2_a):
    r, dk = q.shape
    nch, nsb = r // CHUNK, r // SUB
    row = lax.broadcasted_iota(jnp.int32, (CHUNK, CHUNK), 0)
    col = lax.broadcasted_iota(jnp.int32, (CHUNK, CHUNK), 1)
    tri = (row >= col).astype(BF16)
    parts = _split3_bf16(log2_a)
    b = jnp.concatenate(
        [sum(jnp.dot(tri, p[c * CHUNK:(c + 1) * CHUNK], preferred_element_type=F32) for p in parts)
         for c in range(nch)], axis=0)

    b3 = b.reshape(nch, CHUNK, dk)
    q3 = q.reshape(nch, CHUNK, dk)
    b_last = b3[:, CHUNK - 1:CHUNK, :]
    qb = (q * jnp.exp2(b)).astype(BF16)
    kb = (k.reshape(nch, CHUNK, dk) * jnp.exp2(b_last - b3)).reshape(r, dk).astype(BF16)
    d_last = jnp.exp2(b_last)

    b4 = b.reshape(nsb, SUB, dk)
    q4 = q.reshape(nsb, SUB, dk)
    k4 = k.reshape(nsb, SUB, dk)

    kt3 = (k4 * jnp.exp2(b4[:, SUB - 1:SUB, :] - b4)).reshape(nch, CHUNK, dk).astype(BF16)
    sub_of_row = lax.broadcasted_iota(jnp.int32, (1, CHUNK, 1), 1) // SUB
    q_slots, k_slots = [], []
    for j in range(N_SUB - 1):
        lo = (j + 1) * SUB
        end_j = b3[:, lo - 1:lo, :]
        qt = (q3[:, lo:, :] * jnp.exp2(b3[:, lo:, :] - end_j)).astype(BF16)
        q_slots.append(jnp.concatenate([jnp.zeros((nch, lo, dk), BF16), qt], axis=1))
        k_slots.append(jnp.where(sub_of_row == j, kt3, jnp.zeros_like(kt3)))
    q_cat = jnp.concatenate(q_slots, axis=2)
    k_cat = jnp.concatenate(k_slots, axis=2)

    c4 = jnp.exp2(log2_a.reshape(nsb, SUB, dk))
    kd = k4
    prods = [(q4 * k4).reshape(r, dk).astype(BF16)]
    for _ in range(1, SUB):
        kd = c4 * pltpu.roll(kd, 1, axis=1)
        prods.append((q4 * kd).reshape(r, dk).astype(BF16))
    slot = lax.broadcasted_iota(jnp.int32, (SUB * dk, LANES), 0) // dk
    lane_w = lax.broadcasted_iota(jnp.int32, (SUB * dk, LANES), 1)
    to_lane = (lane_w == (LANES - slot) % LANES).astype(BF16)
    diag = jnp.dot(jnp.concatenate(prods, axis=1), to_lane, preferred_element_type=F32)
    lane = lax.broadcasted_iota(jnp.int32, (r, LANES), 1)
    t_sub = lax.broadcasted_iota(jnp.int32, (r, LANES), 0) % SUB
    diag = jnp.where((lane == 0) | (lane + t_sub >= LANES), diag, 0.0)
    diag = jnp.concatenate(
        [pltpu.roll(diag[c * CHUNK:(c + 1) * CHUNK], 0, axis=1, stride=1, stride_axis=0) for c in range(nch)], axis=0)

    return dict(q_cat=q_cat, k_cat=k_cat, diag=diag, qb=qb, kb=kb, d_last=d_last, v=v, vb=v.astype(BF16))


def _gated_recurrence_heads(heads, st_ref):
    ops = [_recurrence_operands(*hd) for hd in heads]
    nch = ops[0]["q_cat"].shape[0]
    chunk = lambda c: slice(c * CHUNK, (c + 1) * CHUNK)
    scores = [[_nt_dot(p["q_cat"][c], p["k_cat"][c]) + p["diag"][chunk(c), :CHUNK] for p in ops]
              for c in range(nch)]
    intra = [[jnp.dot(scores[c][i].astype(BF16), p["vb"][chunk(c)], preferred_element_type=F32)
              for i, p in enumerate(ops)] for c in range(nch)]
    upd = [[jnp.dot(p["v"][chunk(c)].T.astype(BF16), p["kb"][chunk(c)], preferred_element_type=F32)
            for p in ops] for c in range(nch)]
    sts = [st_ref[i] for i in range(len(ops))]
    outs = [[] for _ in ops]
    for c in range(nch):
        for i, p in enumerate(ops):
            outs[i].append(intra[c][i] + _nt_dot(p["qb"][chunk(c)], sts[i].astype(BF16)))
            sts[i] = sts[i] * p["d_last"][c] + upd[c][i]
    for i, st in enumerate(sts):
        st_ref[i] = st
    return [jnp.concatenate(o, axis=0) for o in outs]


def _hgrn_kernel(q_ref, f_ref, i_ref, g_ref, lb_ref, ng_ref, o_ref, st_ref, *, layer):
    @pl.when(pl.program_id(2) == 0)
    def _():
        st_ref[...] = jnp.zeros_like(st_ref)

    lb_raw = lb_ref[...]
    e = jnp.exp(lb_raw - jnp.max(lb_raw, axis=0, keepdims=True))
    soft = e / jnp.sum(e, axis=0, keepdims=True)
    lb_all = jnp.sum(soft[:layer + 1], axis=0, keepdims=True) - soft[0:1]

    d = LANES
    head_cols = [slice(hh * d, (hh + 1) * d) for hh in range(st_ref.shape[0])]
    heads = []
    for cols in head_cols:
        lb = lb_all[:, cols]
        q_raw = q_ref[:, cols]
        f = lb + (1.0 - lb) * _sigmoid(f_ref[:, cols])
        heads.append((q_raw * _sigmoid(q_raw), 1.0 - f, i_ref[:, cols], jnp.log2(f)))
    for cols, o in zip(head_cols, _gated_recurrence_heads(heads, st_ref)):
        g_raw = g_ref[:, cols]
        o_ref[:, cols] = (_rms(o, ng_ref[:, cols]) * (g_raw * _sigmoid(g_raw))).astype(o_ref.dtype)


def hgrn2_mixer(proj, lb, norm_g, batch, seq, layer, *, rows=512, heads_per_step=8):
    t = proj.shape[0]
    d = LANES
    w = heads_per_step * d
    groups = HGRN_HEADS // heads_per_step
    nblk = seq // rows
    col = lambda base: pl.BlockSpec((rows, w), lambda b, h, i: (b * nblk + i, base + h))
    return pl.pallas_call(
        functools.partial(_hgrn_kernel, layer=layer),
        grid=(batch, groups, nblk),
        in_specs=[col(0), col(groups), col(2 * groups), col(3 * groups),
                  pl.BlockSpec((lb.shape[0], w), lambda b, h, i: (0, h)),
                  pl.BlockSpec((1, w), lambda b, h, i: (0, h))],
        out_specs=pl.BlockSpec((rows, w), lambda b, h, i: (b * nblk + i, h)),
        out_shape=jax.ShapeDtypeStruct((t, HGRN_HEADS * d), BF16),
        scratch_shapes=[pltpu.VMEM((heads_per_step, d, d), F32)],
        compiler_params=_params("parallel", "parallel", "arbitrary"),
    )(proj, proj, proj, proj, lb, norm_g)


def _gla_kernel(q_ref, k_ref, v_ref, r_ref, a_ref, wa_ref, ba_ref, ng_ref, o_ref, st_ref, *, q_scale):
    @pl.when(pl.program_id(2) == 0)
    def _():
        st_ref[...] = jnp.zeros_like(st_ref)

    x = jnp.dot(a_ref[...].astype(BF16), wa_ref[...], preferred_element_type=F32) + ba_ref[...]
    x2 = x * LOG2_E
    log2_a = (jnp.minimum(x2, 0.0) - jnp.log2(1.0 + jnp.exp2(-jnp.abs(x2)))) / GLA_GATE_NORMALIZER
    nh, dv, dk = st_ref.shape
    k_cols = [slice(hh * dk, (hh + 1) * dk) for hh in range(nh)]
    v_cols = [slice(hh * dv, (hh + 1) * dv) for hh in range(nh)]
    heads = [(q_ref[:, kc] * q_scale, k_ref[:, kc], v_ref[:, vc], log2_a[:, kc]) for kc, vc in zip(k_cols, v_cols)]
    for vc, o in zip(v_cols, _gated_recurrence_heads(heads, st_ref)):
        r_raw = r_ref[:, vc]
        o_ref[:, vc] = (_rms(o, ng_ref[:, vc]) * (r_raw * _sigmoid(r_raw))).astype(o_ref.dtype)


def gla_mixer(proj, a_lr, wa2, ba, norm_g, batch, seq, col0, *, rows=512, heads_per_step=4):
    t = proj.shape[0]
    dk = LANES
    dv = 2 * dk
    wk, wv = heads_per_step * dk, heads_per_step * dv
    groups = GLA_HEADS // heads_per_step
    nblk = seq // rows
    cq, ck = col0 // wk, col0 // wk + groups
    cv = (col0 + 2 * GLA_HEADS * dk) // wv
    cr = cv + groups
    spec = lambda w, base: pl.BlockSpec((rows, w), lambda b, h, i: (b * nblk + i, base + h))
    return pl.pallas_call(
        functools.partial(_gla_kernel, q_scale=dk ** -0.5),
        grid=(batch, groups, nblk),
        in_specs=[spec(wk, cq), spec(wk, ck), spec(wv, cv), spec(wv, cr),
                  pl.BlockSpec((rows, a_lr.shape[1]), lambda b, h, i: (b * nblk + i, 0)),
                  pl.BlockSpec((wa2.shape[0], wk), lambda b, h, i: (0, h)),
                  pl.BlockSpec((1, wk), lambda b, h, i: (0, h)),
                  pl.BlockSpec((1, wv), lambda b, h, i: (0, h))],
        out_specs=pl.BlockSpec((rows, wv), lambda b, h, i: (b * nblk + i, h)),
        out_shape=jax.ShapeDtypeStruct((t, GLA_HEADS * dv), BF16),
        scratch_shapes=[pltpu.VMEM((heads_per_step, dv, dk), F32)],
        compiler_params=_params("parallel", "parallel", "arbitrary"),
    )(proj, proj, proj, proj, a_lr, wa2, ba, norm_g)


def kernel(x, norm_g, even_w_in, even_conv_w, even_rel_bias, even_w_out, odd_w_in, hgrn_lb, hgrn_norm_g,
           gla_wa2, gla_ba, gla_norm_g, odd_w_out, mlp_w1, mlp_w2):
    batch, seq, d = x.shape
    depth = norm_g.shape[0]
    half = d // 2
    even_w_in_b, even_w_out_b = even_w_in.astype(BF16), even_w_out.astype(BF16)
    n_odd_main = odd_w_in.shape[2] - GLA_GATE_RANK
    odd_w_in_b, odd_w_out_b = odd_w_in.astype(BF16), odd_w_out.astype(BF16)
    lane_pad = LANES - GLA_GATE_RANK
    h = x.reshape(batch * seq, d)
    for l in range(depth):
        g = norm_g[l][:, None, :]
        e = l // 2
        if l % 2 == 0:
            conv_in, qkv = rms_matmul(h, g[0], even_w_in_b, e, [(3 * half, F32, 1), (3 * half, BF16, 1)])
            h = conv_attention_outproj(conv_in, qkv, even_conv_w[e], even_rel_bias[e], even_w_out_b, e, h, g[1], seq)
        else:
            w_gate = jnp.pad(odd_w_in[e][:, n_odd_main:], ((0, 0), (0, lane_pad))).astype(BF16)
            proj, a_lr = rms_matmul(h, g[0], odd_w_in_b, e, [(n_odd_main, F32, 2)], w_gate)
            ya = hgrn2_mixer(proj, hgrn_lb, hgrn_norm_g[e][None, :], batch, seq, l)
            wa2 = jnp.pad(gla_wa2[e], ((0, lane_pad), (0, 0))).astype(BF16)
            yb = gla_mixer(proj, a_lr, wa2, gla_ba[e][None, :], gla_norm_g[e][None, :], batch, seq,
                           4 * HGRN_HEADS * LANES)
            h = outproj_residual(ya, yb, odd_w_out_b, e, h, g[1])
        h = mlp_residual(h, g[2], mlp_w1, mlp_w2, l, g[3])
    return h.reshape(batch, seq, d)
```

```python
import functools

import numpy as np
import jax
import jax.numpy as jnp
from jax import lax
from jax.experimental import pallas as pl
from jax.experimental.pallas import tpu as pltpu

F32 = jnp.float32
BF16 = jnp.bfloat16

EPS = 1e-6
LOG2_E = 1.4426950408889634
CHUNK = 64
SUB = 8
LANES = 128
N_SUB = CHUNK // SUB

CONV_WIDTH = 3
ATT_PAST_CHUNKS = 8
REL_CLIP = 256
HGRN_HEADS = 8
GLA_HEADS = 4
GLA_GATE_RANK = 16
GLA_GATE_NORMALIZER = 16.0

VMEM_LIMIT = 60 * 1024 * 1024


def _params(*sem):
    return pltpu.CompilerParams(dimension_semantics=sem, vmem_limit_bytes=VMEM_LIMIT)


def _rms(x, g):
    ms = jnp.mean(x * x, axis=-1, keepdims=True)
    return x * lax.rsqrt(ms + EPS) * g


def _sigmoid(x):
    return 1.0 / (1.0 + jnp.exp(-x))


def _nt_dot(a, b):
    return lax.dot_general(a, b, (((1,), (1,)), ((), ())), preferred_element_type=F32)


def _rms_matmul_kernel(h_ref, g_ref, w_ref, *refs, col_steps, has_extra, row_parts):
    if row_parts:
        step = h_ref.shape[0] // row_parts
        for p in range(row_parts):
            rows = slice(p * step, (p + 1) * step)
            u = _rms(h_ref[rows, :], g_ref[...]).astype(BF16)
            for out_idx, col0, tn in col_steps:
                y = jnp.dot(u, w_ref[:, col0:col0 + tn], preferred_element_type=F32)
                refs[out_idx][rows, :] = y.astype(refs[out_idx].dtype)
        return
    u_ref = refs[-1]
    wx_ref, ox_ref = (refs[0], refs[-2]) if has_extra else (None, None)
    o_refs = refs[1:-2] if has_extra else refs[:-1]
    j = pl.program_id(1)

    @pl.when(j == 0)
    def _():
        out_idx, col0, tn = col_steps[0]
        half = h_ref.shape[0] // 2
        for rows in (slice(0, half), slice(half, 2 * half)):
            u = _rms(h_ref[rows, :], g_ref[...]).astype(BF16)
            u_ref[rows, :] = u
            if has_extra:
                ox_ref[rows, :] = jnp.dot(u, wx_ref[...], preferred_element_type=F32)
            y = jnp.dot(u, w_ref[:, col0:col0 + tn], preferred_element_type=F32)
            o_refs[out_idx][rows, :] = y.astype(o_refs[out_idx].dtype)

    for jj, (out_idx, col0, tn) in enumerate(col_steps[1:], start=1):
        @pl.when(j == jj)
        def _():
            o_ref = o_refs[out_idx]
            y = jnp.dot(u_ref[...], w_ref[:, col0:col0 + tn], preferred_element_type=F32)
            o_ref[...] = y.astype(o_ref.dtype)


def rms_matmul(h, g, w, layer, outs, w_extra=None, *, tm=512):
    t, d = h.shape
    resident = pl.Buffered(1)
    in_specs = [pl.BlockSpec((tm, d), lambda i, j: (i, 0)),
                pl.BlockSpec((1, d), lambda i, j: (0, 0)),
                pl.BlockSpec((None, d, w.shape[2]), lambda i, j: (layer, 0, 0), pipeline_mode=resident)]
    args = [h, g, w]
    if w_extra is not None:
        in_specs.append(pl.BlockSpec(w_extra.shape, lambda i, j: (0, 0), pipeline_mode=resident))
        args.append(w_extra)
    out_specs, out_shape, col_steps = [], [], []
    col = 0
    for k, (ncols, dtype, nblk) in enumerate(outs):
        tn = ncols // nblk
        j0 = len(col_steps)
        out_specs.append(pl.BlockSpec(
            (tm, tn), lambda i, j, j0=j0, nblk=nblk: (i, jnp.clip(j - j0, 0, nblk - 1))))
        out_shape.append(jax.ShapeDtypeStruct((t, ncols), dtype))
        col_steps += [(k, col + b * tn, tn) for b in range(nblk)]
        col += ncols
    if w_extra is not None:
        nx = w_extra.shape[1]
        out_specs.append(pl.BlockSpec((tm, nx), lambda i, j: (i, 0)))
        out_shape.append(jax.ShapeDtypeStruct((t, nx), F32))
    one_step = w_extra is None and all(nblk == 1 for _, _, nblk in outs)
    return pl.pallas_call(
        functools.partial(_rms_matmul_kernel, col_steps=tuple(col_steps), has_extra=w_extra is not None,
                          row_parts=2 if one_step else 0),
        grid=(t // tm, 1 if one_step else len(col_steps)),
        in_specs=in_specs, out_specs=out_specs, out_shape=out_shape,
        scratch_shapes=[] if one_step else [pltpu.VMEM((tm, d), BF16)],
        compiler_params=_params("parallel", "arbitrary"),
    )(*args)


def _gated_conv_rows(b_ref, c_ref, hc_ref, cprev_ref, hprev_ref, w_ref, u_ref, first):
    tm = b_ref.shape[0]
    prev = cprev_ref[...] * hprev_ref[...]
    u_ref[0:SUB, :] = jnp.where(first, 0.0, prev)
    u_ref[SUB:, :] = c_ref[...] * hc_ref[...]
    w = w_ref[...]
    y = w[0:1] * u_ref[pl.ds(SUB - 2, tm), :]
    for j in range(1, CONV_WIDTH):
        y = y + w[j:j + 1] * u_ref[pl.ds(SUB - 2 + j, tm), :]
    return b_ref[...] * y


def _band_bias(rel_bias, tq, n_kblk):
    band = (ATT_PAST_CHUNKS + 1) * CHUNK
    heads = rel_bias.shape[0]
    assert rel_bias.shape[1] == CHUNK + REL_CLIP, "table covers distances -(CHUNK - 1) .. REL_CLIP"
    rel_bias = rel_bias.astype(F32) * LOG2_E
    n_far = band - REL_CLIP
    ext = jnp.concatenate([jnp.broadcast_to(rel_bias[:, -1:], (heads, n_far)), rel_bias[:, ::-1][:, 1:]], axis=1)
    rows = jnp.stack([ext[:, CHUNK - 1 - qi:CHUNK - 1 - qi + band] for qi in range(CHUNK)], axis=1)
    tk = n_kblk * tq
    neg = lambda w: jnp.full((heads, CHUNK, w), -1e30, F32)
    blocks = [jnp.concatenate([neg(c * CHUNK), rows, neg(tk - band - c * CHUNK)], axis=2)
              for c in range(tq // CHUNK)]
    base = jnp.concatenate(blocks, axis=1)
    kblk = np.arange(tk)[None, :] // tq
    variants = [jnp.where(kblk + v >= n_kblk - 1, base, -1e30) for v in range(n_kblk)]
    return jnp.stack(variants, axis=0)


def _band_scores(q_ref, k_refs, bias_ref, heads, scale):
    dh = q_ref.shape[1] // heads
    scores = []
    for h in range(heads):
        cols = slice(h * dh, (h + 1) * dh)
        q = (q_ref[:, cols].astype(F32) * scale).astype(BF16)
        k = jnp.concatenate([r[:, cols] for r in k_refs], axis=0).astype(BF16)
        scores.append((cols, _nt_dot(q, k) + bias_ref[h]))
    return scores


def _band_softmax_pv(s, v_refs, cols):
    v = jnp.concatenate([r[:, cols] for r in v_refs], axis=0).astype(BF16)
    dh = v.shape[1]
    v_ones = jnp.concatenate([v, jnp.ones_like(v)], axis=1)
    m = jnp.max(s, axis=-1, keepdims=True)
    p = jnp.exp2(s - m).astype(BF16)
    o_l = jnp.dot(p, v_ones, preferred_element_type=F32)
    return o_l[:, :dh] / o_l[:, dh:dh + 1]


def _conv_attention_outproj_kernel(q_ref, *refs, n_kblk, heads, scale, blocks_per_seq):
    k_refs, v_refs = refs[:n_kblk], refs[n_kblk:2 * n_kblk]
    (bias_ref, b_ref, c_ref, hc_ref, cprev_ref, hprev_ref, cw_ref, w_ref, h_ref, g_ref,
     o_ref, y_ref, u_ref) = refs[2 * n_kblk:]
    step = pl.program_id(0)

    @pl.when(step == 0)
    def _():
        y_ref[...] = jnp.zeros_like(y_ref)

    slot = step % 2
    y = jnp.dot(y_ref[1 - slot], w_ref[...], preferred_element_type=F32)
    o_ref[...] = h_ref[...] + _rms(y, g_ref[...])

    block = jnp.minimum(step, pl.num_programs(0) - 2)
    half = b_ref.shape[1]
    ya = _gated_conv_rows(b_ref, c_ref, hc_ref, cprev_ref, hprev_ref, cw_ref, u_ref, block % blocks_per_seq == 0)
    y_ref[slot, :, 0:half] = ya.astype(BF16)
    for cols, s in _band_scores(q_ref, k_refs, bias_ref, heads, scale):
        o = _band_softmax_pv(s, v_refs, cols)
        y_ref[slot, :, half + cols.start:half + cols.stop] = o.astype(BF16)


def conv_attention_outproj(conv_in, qkv, conv_w, rel_bias, w_out, layer, h, g, seq, *, tq=256):
    t, d = h.shape
    half = conv_w.shape[1]
    heads = rel_bias.shape[0]
    n_kblk = ATT_PAST_CHUNKS * CHUNK // tq + 1
    nq = seq // tq
    nblk = t // tq
    bias = _band_bias(rel_bias, tq, n_kblk)
    blk = lambda s: jnp.minimum(s, nblk - 1)
    prev_blk = lambda s: jnp.maximum(s - 1, 0)

    def kv_spec(col, j):
        def index(s):
            i = blk(s) % nq
            return (blk(s) - i + jnp.maximum(i - (n_kblk - 1) + j, 0), col)
        return pl.BlockSpec((tq, half), index)

    cur = lambda col: pl.BlockSpec((tq, half), lambda s: (blk(s), col))
    prev_rows = lambda col: pl.BlockSpec((SUB, half), lambda s: (jnp.maximum(blk(s) * (tq // SUB) - 1, 0), col))
    in_specs = [cur(0)] + [kv_spec(1, j) for j in range(n_kblk)] + [kv_spec(2, j) for j in range(n_kblk)]
    in_specs += [pl.BlockSpec((None, heads, tq, n_kblk * tq),
                              lambda s: (jnp.minimum(blk(s) % nq, n_kblk - 1), 0, 0, 0)),
                 cur(0), cur(1), cur(2), prev_rows(1), prev_rows(2),
                 pl.BlockSpec((CONV_WIDTH, half), lambda s: (0, 0)),
                 pl.BlockSpec((None, d, d), lambda s: (layer, 0, 0), pipeline_mode=pl.Buffered(1)),
                 pl.BlockSpec((tq, d), lambda s: (prev_blk(s), 0)),
                 pl.BlockSpec((1, d), lambda s: (0, 0))]
    return pl.pallas_call(
        functools.partial(_conv_attention_outproj_kernel, n_kblk=n_kblk, heads=heads,
                          scale=(half // heads) ** -0.5 * LOG2_E, blocks_per_seq=nq),
        grid=(nblk + 1,),
        in_specs=in_specs,
        out_specs=pl.BlockSpec((tq, d), lambda s: (prev_blk(s), 0)),
        out_shape=jax.ShapeDtypeStruct((t, d), F32),
        scratch_shapes=[pltpu.VMEM((2, tq, d), BF16), pltpu.VMEM((tq + SUB, half), F32)],
        compiler_params=_params("arbitrary"),
    )(*([qkv] * (1 + 2 * n_kblk)), bias, *([conv_in] * 5), conv_w, w_out, h, g)


def _outproj_kernel(ya_ref, yb_ref, w_ref, h_ref, g_ref, o_ref):
    y = jnp.concatenate([ya_ref[...], yb_ref[...]], axis=1)
    y = jnp.dot(y, w_ref[...], preferred_element_type=F32)
    o_ref[...] = h_ref[...] + _rms(y, g_ref[...])


def outproj_residual(ya, yb, w, layer, h, g, *, tm=512):
    t, d = h.shape
    row = lambda a: pl.BlockSpec((tm, a.shape[1]), lambda i: (i, 0))
    return pl.pallas_call(
        _outproj_kernel, grid=(t // tm,),
        in_specs=[row(ya), row(yb), pl.BlockSpec((None,) + w.shape[1:], lambda i: (layer, 0, 0)), row(h),
                  pl.BlockSpec((1, d), lambda i: (0, 0))],
        out_specs=row(h),
        out_shape=jax.ShapeDtypeStruct((t, d), F32),
        compiler_params=_params("parallel"),
    )(ya, yb, w, h, g)


def _mlp_kernel(h_ref, g_in_ref, w1_ref, w2_ref, g_out_ref, o_ref, u_ref, *, parts):
    j = pl.program_id(1)
    last = pl.num_programs(1) - 1
    step = o_ref.shape[0] // parts
    row_parts = [slice(p * step, (p + 1) * step) for p in range(parts)]

    def ff_block(u):
        a = jnp.dot(u, w1_ref[...].astype(BF16), preferred_element_type=F32)
        a = jnp.square(jnp.maximum(a, 0.0)).astype(BF16)
        return jnp.dot(a, w2_ref[...].astype(BF16), preferred_element_type=F32)

    @pl.when(j == 0)
    def _():
        for rows in row_parts:
            u = _rms(h_ref[rows, :], g_in_ref[...]).astype(BF16)
            u_ref[rows, :] = u
            o_ref[rows, :] = ff_block(u)

    @pl.when((j > 0) & (j < last))
    def _():
        o_ref[...] += ff_block(u_ref[...])

    @pl.when(j == last)
    def _():
        for rows in row_parts:
            z = o_ref[rows, :] + ff_block(u_ref[rows, :])
            o_ref[rows, :] = h_ref[rows, :] + _rms(z, g_out_ref[...])


def mlp_residual(h, g_in, w1, w2, layer, g_out, *, tm=1024, tf=512, parts=2):
    t, d = h.shape
    ff = w1.shape[2]
    assert ff // tf >= 2, "first and last ff block are handled by different branches"
    vec = pl.BlockSpec((1, d), lambda i, j: (0, 0))
    return pl.pallas_call(
        functools.partial(_mlp_kernel, parts=parts), grid=(t // tm, ff // tf),
        in_specs=[pl.BlockSpec((tm, d), lambda i, j: (i, 0)), vec,
                  pl.BlockSpec((None, d, tf), lambda i, j: (layer, 0, j)),
                  pl.BlockSpec((None, tf, d), lambda i, j: (layer, j, 0)), vec],
        out_specs=pl.BlockSpec((tm, d), lambda i, j: (i, 0)),
        out_shape=jax.ShapeDtypeStruct((t, d), F32),
        scratch_shapes=[pltpu.VMEM((tm, d), BF16)],
        compiler_params=_params("parallel", "arbitrary"),
    )(h, g_in, w1, w2, g_out)


def _split3_bf16(x):
    hi = x.astype(BF16)
    r = x - hi.astype(F32)
    mid = r.astype(BF16)
    lo = (r - mid.astype(F32)).astype(BF16)
    return hi, mid, lo


def _recurrence_operands(q, k, v, log2_a):
    r, dk = q.shape
    nch, nsb = r // CHUNK, r // SUB
    row = lax.broadcasted_iota(jnp.int32, (CHUNK, CHUNK), 0)
    col = lax.broadcasted_iota(jnp.int32, (CHUNK, CHUNK), 1)
    tri = (row >= col).astype(BF16)
    parts = _split3_bf16(log2_a)
    b = jnp.concatenate(
        [sum(jnp.dot(tri, p[c * CHUNK:(c + 1) * CHUNK], preferred_element_type=F32) for p in parts)
         for c in range(nch)], axis=0)

    b3 = b.reshape(nch, CHUNK, dk)
    q3 = q.reshape(nch, CHUNK, dk)
    b_last = b3[:, CHUNK - 1:CHUNK, :]
    qb = (q * jnp.exp2(b)).astype(BF16)
    kb = (k.reshape(nch, CHUNK, dk) * jnp.exp2(b_last - b3)).reshape(r, dk).astype(BF16)
    d_last = jnp.exp2(b_last)

    b4 = b.reshape(nsb, SUB, dk)
    q4 = q.reshape(nsb, SUB, dk)
    k4 = k.reshape(nsb, SUB, dk)

    kt3 = (k4 * jnp.exp2(b4[:, SUB - 1:SUB, :] - b4)).reshape(nch, CHUNK, dk).astype(BF16)
    sub_of_row = lax.broadcasted_iota(jnp.int32, (1, CHUNK, 1), 1) // SUB
    q_slots, k_slots = [], []
    for j in range(N_SUB - 1):
        lo = (j + 1) * SUB
        end_j = b3[:, lo - 1:lo, :]
        qt = (q3[:, lo:, :] * jnp.exp2(b3[:, lo:, :] - end_j)).astype(BF16)
        q_slots.append(jnp.concatenate([jnp.zeros((nch, lo, dk), BF16), qt], axis=1))
        k_slots.append(jnp.where(sub_of_row == j, kt3, jnp.zeros_like(kt3)))
    q_cat = jnp.concatenate(q_slots, axis=2)
    k_cat = jnp.concatenate(k_slots, axis=2)

    c4 = jnp.exp2(log2_a.reshape(nsb, SUB, dk))
    kd = k4
    prods = [(q4 * k4).reshape(r, dk).astype(BF16)]
    for _ in range(1, SUB):
        kd = c4 * pltpu.roll(kd, 1, axis=1)
        prods.append((q4 * kd).reshape(r, dk).astype(BF16))
    slot = lax.broadcasted_iota(jnp.int32, (SUB * dk, LANES), 0) // dk
    lane_w = lax.broadcasted_iota(jnp.int32, (SUB * dk, LANES), 1)
    to_lane = (lane_w == (LANES - slot) % LANES).astype(BF16)
    diag = jnp.dot(jnp.concatenate(prods, axis=1), to_lane, preferred_element_type=F32)
    lane = lax.broadcasted_iota(jnp.int32, (r, LANES), 1)
    t_sub = lax.broadcasted_iota(jnp.int32, (r, LANES), 0) % SUB
    diag = jnp.where((lane == 0) | (lane + t_sub >= LANES), diag, 0.0)
    diag = jnp.concatenate(
        [pltpu.roll(diag[c * CHUNK:(c + 1) * CHUNK], 0, axis=1, stride=1, stride_axis=0) for c in range(nch)], axis=0)

    return dict(q_cat=q_cat, k_cat=k_cat, diag=diag, qb=qb, kb=kb, d_last=d_last, v=v, vb=v.astype(BF16))


def _gated_recurrence_heads(heads, st_ref):
    ops = [_recurrence_operands(*hd) for hd in heads]
    nch = ops[0]["q_cat"].shape[0]
    chunk = lambda c: slice(c * CHUNK, (c + 1) * CHUNK)
    scores = [[_nt_dot(p["q_cat"][c], p["k_cat"][c]) + p["diag"][chunk(c), :CHUNK] for p in ops]
              for c in range(nch)]
    intra = [[jnp.dot(scores[c][i].astype(BF16), p["vb"][chunk(c)], preferred_element_type=F32)
              for i, p in enumerate(ops)] for c in range(nch)]
    upd = [[jnp.dot(p["v"][chunk(c)].T.astype(BF16), p["kb"][chunk(c)], preferred_element_type=F32)
            for p in ops] for c in range(nch)]
    sts = [st_ref[i] for i in range(len(ops))]
    outs = [[] for _ in ops]
    for c in range(nch):
        for i, p in enumerate(ops):
            outs[i].append(intra[c][i] + _nt_dot(p["qb"][chunk(c)], sts[i].astype(BF16)))
            sts[i] = sts[i] * p["d_last"][c] + upd[c][i]
    for i, st in enumerate(sts):
        st_ref[i] = st
    return [jnp.concatenate(o, axis=0) for o in outs]


def _hgrn_kernel(q_ref, f_ref, i_ref, g_ref, lb_ref, ng_ref, o_ref, st_ref, *, layer):
    @pl.when(pl.program_id(2) == 0)
    def _():
        st_ref[...] = jnp.zeros_like(st_ref)

    lb_raw = lb_ref[...]
    e = jnp.exp(lb_raw - jnp.max(lb_raw, axis=0, keepdims=True))
    soft = e / jnp.sum(e, axis=0, keepdims=True)
    lb_all = jnp.sum(soft[:layer + 1], axis=0, keepdims=True) - soft[0:1]

    d = LANES
    head_cols = [slice(hh * d, (hh + 1) * d) for hh in range(st_ref.shape[0])]
    heads = []
    for cols in head_cols:
        lb = lb_all[:, cols]
        q_raw = q_ref[:, cols]
        f = lb + (1.0 - lb) * _sigmoid(f_ref[:, cols])
        heads.append((q_raw * _sigmoid(q_raw), 1.0 - f, i_ref[:, cols], jnp.log2(f)))
    for cols, o in zip(head_cols, _gated_recurrence_heads(heads, st_ref)):
        g_raw = g_ref[:, cols]
        o_ref[:, cols] = (_rms(o, ng_ref[:, cols]) * (g_raw * _sigmoid(g_raw))).astype(o_ref.dtype)


def hgrn2_mixer(proj, lb, norm_g, batch, seq, layer, *, rows=512, heads_per_step=8):
    t = proj.shape[0]
    d = LANES
    w = heads_per_step * d
    groups = HGRN_HEADS // heads_per_step
    nblk = seq // rows
    col = lambda base: pl.BlockSpec((rows, w), lambda b, h, i: (b * nblk + i, base + h))
    return pl.pallas_call(
        functools.partial(_hgrn_kernel, layer=layer),
        grid=(batch, groups, nblk),
        in_specs=[col(0), col(groups), col(2 * groups), col(3 * groups),
                  pl.BlockSpec((lb.shape[0], w), lambda b, h, i: (0, h)),
                  pl.BlockSpec((1, w), lambda b, h, i: (0, h))],
        out_specs=pl.BlockSpec((rows, w), lambda b, h, i: (b * nblk + i, h)),
        out_shape=jax.ShapeDtypeStruct((t, HGRN_HEADS * d), BF16),
        scratch_shapes=[pltpu.VMEM((heads_per_step, d, d), F32)],
        compiler_params=_params("parallel", "parallel", "arbitrary"),
    )(proj, proj, proj, proj, lb, norm_g)


def _gla_kernel(q_ref, k_ref, v_ref, r_ref, a_ref, wa_ref, ba_ref, ng_ref, o_ref, st_ref, *, q_scale):
    @pl.when(pl.program_id(2) == 0)
    def _():
        st_ref[...] = jnp.zeros_like(st_ref)

    x = jnp.dot(a_ref[...].astype(BF16), wa_ref[...], preferred_element_type=F32) + ba_ref[...]
    x2 = x * LOG2_E
    log2_a = (jnp.minimum(x2, 0.0) - jnp.log2(1.0 + jnp.exp2(-jnp.abs(x2)))) / GLA_GATE_NORMALIZER
    nh, dv, dk = st_ref.shape
    k_cols = [slice(hh * dk, (hh + 1) * dk) for hh in range(nh)]
    v_cols = [slice(hh * dv, (hh + 1) * dv) for hh in range(nh)]
    heads = [(q_ref[:, kc] * q_scale, k_ref[:, kc], v_ref[:, vc], log2_a[:, kc]) for kc, vc in zip(k_cols, v_cols)]
    for vc, o in zip(v_cols, _gated_recurrence_heads(heads, st_ref)):
        r_raw = r_ref[:, vc]
        o_ref[:, vc] = (_rms(o, ng_ref[:, vc]) * (r_raw * _sigmoid(r_raw))).astype(o_ref.dtype)


def gla_mixer(proj, a_lr, wa2, ba, norm_g, batch, seq, col0, *, rows=512, heads_per_step=4):
    t = proj.shape[0]
    dk = LANES
    dv = 2 * dk
    wk, wv = heads_per_step * dk, heads_per_step * dv
    groups = GLA_HEADS // heads_per_step
    nblk = seq // rows
    cq, ck = col0 // wk, col0 // wk + groups
    cv = (col0 + 2 * GLA_HEADS * dk) // wv
    cr = cv + groups
    spec = lambda w, base: pl.BlockSpec((rows, w), lambda b, h, i: (b * nblk + i, base + h))
    return pl.pallas_call(
        functools.partial(_gla_kernel, q_scale=dk ** -0.5),
        grid=(batch, groups, nblk),
        in_specs=[spec(wk, cq), spec(wk, ck), spec(wv, cv), spec(wv, cr),
                  pl.BlockSpec((rows, a_lr.shape[1]), lambda b, h, i: (b * nblk + i, 0)),
                  pl.BlockSpec((wa2.shape[0], wk), lambda b, h, i: (0, h)),
                  pl.BlockSpec((1, wk), lambda b, h, i: (0, h)),
                  pl.BlockSpec((1, wv), lambda b, h, i: (0, h))],
        out_specs=pl.BlockSpec((rows, wv), lambda b, h, i: (b * nblk + i, h)),
        out_shape=jax.ShapeDtypeStruct((t, GLA_HEADS * dv), BF16),
        scratch_shapes=[pltpu.VMEM((heads_per_step, dv, dk), F32)],
        compiler_params=_params("parallel", "parallel", "arbitrary"),
    )(proj, proj, proj, proj, a_lr, wa2, ba, norm_g)


def kernel(x, norm_g, even_w_in, even_conv_w, even_rel_bias, even_w_out, odd_w_in, hgrn_lb, hgrn_norm_g,
           gla_wa2, gla_ba, gla_norm_g, odd_w_out, mlp_w1, mlp_w2):
    batch, seq, d = x.shape
    depth = norm_g.shape[0]
    half = d // 2
    even_w_in_b, even_w_out_b = even_w_in.astype(BF16), even_w_out.astype(BF16)
    n_odd_main = odd_w_in.shape[2] - GLA_GATE_RANK
    odd_w_in_b, odd_w_out_b = odd_w_in.astype(BF16), odd_w_out.astype(BF16)
    lane_pad = LANES - GLA_GATE_RANK
    h = x.reshape(batch * seq, d)
    for l in range(depth):
        g = norm_g[l][:, None, :]
        e = l // 2
        if l % 2 == 0:
            conv_in, qkv = rms_matmul(h, g[0], even_w_in_b, e, [(3 * half, F32, 1), (3 * half, BF16, 1)])
            h = conv_attention_outproj(conv_in, qkv, even_conv_w[e], even_rel_bias[e], even_w_out_b, e, h, g[1], seq)
        else:
            w_gate = jnp.pad(odd_w_in[e][:, n_odd_main:], ((0, 0), (0, lane_pad))).astype(BF16)
            proj, a_lr = rms_matmul(h, g[0], odd_w_in_b, e, [(n_odd_main, F32, 2)], w_gate)
            ya = hgrn2_mixer(proj, hgrn_lb, hgrn_norm_g[e][None, :], batch, seq, l)
            wa2 = jnp.pad(gla_wa2[e], ((0, lane_pad), (0, 0))).astype(BF16)
            yb = gla_mixer(proj, a_lr, wa2, gla_ba[e][None, :], gla_norm_g[e][None, :], batch, seq,
                           4 * HGRN_HEADS * LANES)
            h = outproj_residual(ya, yb, odd_w_out_b, e, h, g[1])
        h = mlp_residual(h, g[2], mlp_w1, mlp_w2, l, g[3])
    return h.reshape(batch, seq, d)
```
